```python
import math
import jax, jax.numpy as jnp
from jax import lax
import numpy as np

D_MODEL = 1024
BATCH = 8
SEQ = 2048
DEPTH = 2
DEC_BATCH = 128
DEC_SEQ = 1
PAST_LEN = 16384
PAGE_SIZE = 128

GDN_HEAD_DIM = 128
GDN_HEADS = D_MODEL // GDN_HEAD_DIM
GDN_DK = GDN_HEAD_DIM
GDN_DV = GDN_HEAD_DIM
GDN_CONV = 4
GDN_CHUNK = 64
GDN_QK_W = GDN_HEADS * GDN_DK
GDN_V_W = GDN_HEADS * GDN_DV
GDN_CONV_CH = 2 * GDN_QK_W + GDN_V_W
GDN_IN_COLS = GDN_CONV_CH + GDN_V_W + 2 * GDN_HEADS
SC_WIDTH = 3
D_FF = ((8 * D_MODEL // 3) + 255) // 256 * 256
N_EXPERTS = 8
TOP_K = 2
D_FF_EXPERT = 7 * D_MODEL // 2
N_EVEN_LAYERS = (DEPTH + 1) // 2
N_ODD_LAYERS = DEPTH // 2
ALPHA = (2.0 * DEPTH) ** 0.25
BETA_INIT = (8.0 * DEPTH) ** -0.25
LN_EPS = 1e-5
NORM_EPS = 1e-6

kernel_name = 'hybrid_gdn_shortconv_moe_step'


def layer_norm(x, g, b):
    xf = x.astype(jnp.float32)
    mu = jnp.mean(xf, axis=-1, keepdims=True)
    var = jnp.mean(jnp.square(xf - mu), axis=-1, keepdims=True)
    y = (xf - mu) * lax.rsqrt(var + LN_EPS) * g.astype(jnp.float32) + b.astype(jnp.float32)
    return y.astype(x.dtype)


def l2_normalize(x):
    return x * lax.rsqrt(jnp.sum(jnp.square(x), axis=-1, keepdims=True) + NORM_EPS)


def causal_dwconv(u, buf, w):
    width, ch = w.shape
    full = jnp.concatenate([buf.astype(u.dtype), u], axis=1)
    out = lax.conv_general_dilated(full, w[:, None, :].astype(u.dtype), window_strides=(1,), padding='VALID',
                                   dimension_numbers=('NWC', 'WIO', 'NWC'), feature_group_count=ch)
    return out, full[:, full.shape[1] - (width - 1):]


def gated_delta_scan(q, k, v, g, beta, S0):
    def step(S, inp):
        q_t, k_t, v_t, g_t, b_t = inp
        S = S * jnp.exp(g_t)[..., None, None]
        kv_mem = jnp.einsum('bhkv,bhk->bhv', S, k_t)
        delta = (v_t - kv_mem) * b_t[..., None]
        S = S + jnp.einsum('bhk,bhv->bhkv', k_t, delta)
        o_t = jnp.einsum('bhkv,bhk->bhv', S, q_t)
        return S, o_t
    xs = (jnp.moveaxis(q, 1, 0), jnp.moveaxis(k, 1, 0), jnp.moveaxis(v, 1, 0),
          jnp.moveaxis(g, 1, 0), jnp.moveaxis(beta, 1, 0))
    S, o = lax.scan(step, S0, xs)
    return jnp.moveaxis(o, 0, 1), S


def gated_delta_chunked(q, k, v, g, beta, S0):
    bsz, t, h, dk = q.shape
    dv = v.shape[-1]
    c = GDN_CHUNK
    n = t // c

    def to_chunks(a):
        a = a.reshape((bsz, n, c, h) + a.shape[3:])
        return jnp.moveaxis(a, (1, 3), (0, 2))

    qc, kc, vc, gc, bc = to_chunks(q), to_chunks(k), to_chunks(v), to_chunks(g), to_chunks(beta)
    gcum = jnp.cumsum(gc, axis=-1)
    tril = jnp.tril(jnp.ones((c, c), dtype=bool))
    strict = jnp.tril(jnp.ones((c, c), dtype=bool), -1)
    diff = gcum[..., :, None] - gcum[..., None, :]
    decay_mask = jnp.exp(jnp.where(tril, diff, -jnp.inf))
    kb = kc * bc[..., None]
    vb = vc * bc[..., None]
    a_mat = jnp.where(strict, jnp.einsum('nbhik,nbhjk->nbhij', kb, kc) * decay_mask, 0.0)
    eye = jnp.eye(c, dtype=q.dtype)
    rhs = jnp.concatenate([vb, kb * jnp.exp(gcum)[..., None]], axis=-1)
    sol = lax.linalg.triangular_solve(a_mat + eye, rhs, left_side=True, lower=True)
    u, w = sol[..., :dv], sol[..., dv:]
    qk = jnp.where(tril, jnp.einsum('nbhik,nbhjk->nbhij', qc, kc) * decay_mask, 0.0)

    def step(S, inp):
        q_i, k_i, u_i, w_i, qk_i, g_i = inp
        v_new = u_i - jnp.einsum('bhck,bhkv->bhcv', w_i, S)
        o = (jnp.einsum('bhck,bhkv->bhcv', q_i * jnp.exp(g_i)[..., None], S)
             + jnp.einsum('bhij,bhjv->bhiv', qk_i, v_new))
        g_last = g_i[..., -1]
        k_dec = k_i * jnp.exp(g_last[..., None] - g_i)[..., None]
        S = S * jnp.exp(g_last)[..., None, None] + jnp.einsum('bhck,bhcv->bhkv', k_dec, v_new)
        return S, o

    S, o = lax.scan(step, S0, (qc, kc, u, w, qk, gcum))
    o = jnp.moveaxis(o, (0, 2), (1, 3)).reshape(bsz, t, h, dv)
    return o, S


def gdn_mixer(x, S0, conv_buf, w_in, conv_w, a_log, dt_bias, norm_w, w_out, chunked):
    f32 = jnp.float32
    bsz, t, _ = x.shape
    proj = x @ w_in
    qkv = proj[..., :GDN_CONV_CH]
    z = proj[..., GDN_CONV_CH:GDN_CONV_CH + GDN_V_W]
    a = proj[..., GDN_CONV_CH + GDN_V_W:GDN_CONV_CH + GDN_V_W + GDN_HEADS]
    b = proj[..., GDN_CONV_CH + GDN_V_W + GDN_HEADS:]
    qkv_c, new_buf = causal_dwconv(qkv, conv_buf, conv_w)
    qkv_c = jax.nn.silu(qkv_c.astype(f32))
    q = qkv_c[..., :GDN_QK_W].reshape(bsz, t, GDN_HEADS, GDN_DK)
    k = qkv_c[..., GDN_QK_W:2 * GDN_QK_W].reshape(bsz, t, GDN_HEADS, GDN_DK)
    v = qkv_c[..., 2 * GDN_QK_W:].reshape(bsz, t, GDN_HEADS, GDN_DV)
    q = l2_normalize(q) * (GDN_DK ** -0.5)
    k = l2_normalize(k)
    g = -jnp.exp(a_log.astype(f32)) * jax.nn.softplus(a.astype(f32) + dt_bias.astype(f32))
    beta = jax.nn.sigmoid(b.astype(f32))
    S0f = S0.astype(f32)
    if chunked:
        o, S = gated_delta_chunked(q, k, v, g, beta, S0f)
    else:
        o, S = gated_delta_scan(q, k, v, g, beta, S0f)
    o = o * lax.rsqrt(jnp.mean(jnp.square(o), axis=-1, keepdims=True) + NORM_EPS) * norm_w.astype(f32)
    o = o * jax.nn.silu(z.astype(f32)).reshape(bsz, t, GDN_HEADS, GDN_DV)
    y = o.reshape(bsz, t, GDN_V_W).astype(x.dtype) @ w_out
    return y, S.astype(S0.dtype), new_buf


def shortconv_mixer(x, buf, w_in, conv_w, w_out):
    proj = x @ w_in
    b_gate, c_gate, h = jnp.split(proj, 3, axis=-1)
    conv, new_buf = causal_dwconv(c_gate * h, buf, conv_w)
    return (b_gate * conv) @ w_out, new_buf


def swiglu(x, w_gu, w_down):
    gate, up = jnp.split(x @ w_gu, 2, axis=-1)
    return (jax.nn.silu(gate) * up) @ w_down


def moe_swiglu(x, w_router, w_gu, w_down):
    logits = (x @ w_router).astype(jnp.float32)
    top_v, top_i = lax.top_k(logits, TOP_K)
    gates = jax.nn.softmax(top_v, axis=-1)
    combine = jnp.einsum('...k,...ke->...e', gates, jax.nn.one_hot(top_i, N_EXPERTS, dtype=jnp.float32))
    y = jnp.zeros_like(x)
    for e in range(N_EXPERTS):
        y = y + swiglu(x, w_gu[e], w_down[e]) * combine[..., e:e + 1].astype(x.dtype)
    return y


def setup_inputs(seed: int = 0) -> dict:
    key = jax.random.key(seed)
    ks = jax.random.split(key, 24)
    nrm = lambda k, shape, s: jax.random.normal(k, shape, jnp.float32) * s
    dt = jnp.exp(jax.random.uniform(ks[9], (N_EVEN_LAYERS, GDN_HEADS), jnp.float32,
                                    minval=math.log(1e-3), maxval=math.log(1e-1)))
    return {
        'x_prompt': nrm(ks[0], (BATCH, SEQ, D_MODEL), 1.0),
        'x_sample': nrm(ks[1], (DEC_BATCH, DEC_SEQ, D_MODEL), 1.0),
        'state_gdn_S': nrm(ks[2], (N_EVEN_LAYERS, DEC_BATCH, GDN_HEADS, GDN_DK, GDN_DV), 0.1),
        'state_gdn_conv': nrm(ks[3], (N_EVEN_LAYERS, DEC_BATCH, GDN_CONV - 1, GDN_CONV_CH), 1.0),
        'state_sconv': nrm(ks[4], (N_ODD_LAYERS, DEC_BATCH, SC_WIDTH - 1, D_MODEL), 1.0),
        'ln_g': 1.0 + nrm(ks[5], (DEPTH, 2, D_MODEL), 0.02),
        'ln_b': nrm(ks[6], (DEPTH, 2, D_MODEL), 0.02),
        'gdn_w_in': nrm(ks[7], (N_EVEN_LAYERS, D_MODEL, GDN_IN_COLS), D_MODEL ** -0.5),
        'gdn_conv_w': nrm(ks[8], (N_EVEN_LAYERS, GDN_CONV, GDN_CONV_CH), GDN_CONV ** -0.5),
        'gdn_a_log': jnp.log(jax.random.uniform(ks[10], (N_EVEN_LAYERS, GDN_HEADS), jnp.float32,
                                                minval=1.0, maxval=16.0)),
        'gdn_dt_bias': dt + jnp.log(-jnp.expm1(-dt)),
        'gdn_norm_w': 1.0 + nrm(ks[11], (N_EVEN_LAYERS, GDN_DV), 0.02),
        'gdn_w_out': nrm(ks[12], (N_EVEN_LAYERS, GDN_V_W, D_MODEL), GDN_V_W ** -0.5 * BETA_INIT),
        'sc_w_in': nrm(ks[13], (N_ODD_LAYERS, D_MODEL, 3 * D_MODEL), D_MODEL ** -0.5),
        'sc_conv_w': nrm(ks[14], (N_ODD_LAYERS, SC_WIDTH, D_MODEL), SC_WIDTH ** -0.5),
        'sc_w_out': nrm(ks[15], (N_ODD_LAYERS, D_MODEL, D_MODEL), D_MODEL ** -0.5 * BETA_INIT),
        'ffn_w_gu': nrm(ks[16], (N_EVEN_LAYERS, D_MODEL, 2 * D_FF), D_MODEL ** -0.5),
        'ffn_w_down': nrm(ks[17], (N_EVEN_LAYERS, D_FF, D_MODEL), D_FF ** -0.5 * BETA_INIT),
        'moe_w_router': nrm(ks[18], (N_ODD_LAYERS, D_MODEL, N_EXPERTS), D_MODEL ** -0.5),
        'moe_w_gu': nrm(ks[19], (N_ODD_LAYERS, N_EXPERTS, D_MODEL, 2 * D_FF_EXPERT), D_MODEL ** -0.5),
        'moe_w_down': nrm(ks[20], (N_ODD_LAYERS, N_EXPERTS, D_FF_EXPERT, D_MODEL),
                          D_FF_EXPERT ** -0.5 * BETA_INIT),
    }


def reference(x_prompt, x_sample, state_gdn_S, state_gdn_conv, state_sconv, ln_g, ln_b,
              gdn_w_in, gdn_conv_w, gdn_a_log, gdn_dt_bias, gdn_norm_w, gdn_w_out,
              sc_w_in, sc_conv_w, sc_w_out, ffn_w_gu, ffn_w_down, moe_w_router, moe_w_gu, moe_w_down):
    hp, hs = x_prompt, x_sample
    bp = x_prompt.shape[0]
    s_p, s_s, cv_p, cv_s, sc_p, sc_s = [], [], [], [], [], []
    for i in range(DEPTH):
        j = i // 2
        if i % 2 == 0:
            w = (gdn_w_in[j], gdn_conv_w[j], gdn_a_log[j], gdn_dt_bias[j], gdn_norm_w[j], gdn_w_out[j])
            S0p = jnp.zeros((bp,) + state_gdn_S.shape[2:], state_gdn_S.dtype)
            buf0p = jnp.zeros((bp,) + state_gdn_conv.shape[2:], state_gdn_conv.dtype)
            mp, sp_new, cp_new = gdn_mixer(hp, S0p, buf0p, *w, chunked=True)
            ms, ss_new, cs_new = gdn_mixer(hs, state_gdn_S[j], state_gdn_conv[j], *w, chunked=False)
            s_p.append(sp_new); s_s.append(ss_new); cv_p.append(cp_new); cv_s.append(cs_new)
        else:
            buf0p = jnp.zeros((bp,) + state_sconv.shape[2:], state_sconv.dtype)
            mp, bp_new = shortconv_mixer(hp, buf0p, sc_w_in[j], sc_conv_w[j], sc_w_out[j])
            ms, bs_new = shortconv_mixer(hs, state_sconv[j], sc_w_in[j], sc_conv_w[j], sc_w_out[j])
            sc_p.append(bp_new); sc_s.append(bs_new)
        hp = layer_norm(ALPHA * hp + mp, ln_g[i, 0], ln_b[i, 0])
        hs = layer_norm(ALPHA * hs + ms, ln_g[i, 0], ln_b[i, 0])
        if i % 2 == 0:
            fp = swiglu(hp, ffn_w_gu[j], ffn_w_down[j])
            fs = swiglu(hs, ffn_w_gu[j], ffn_w_down[j])
        else:
            fp = moe_swiglu(hp, moe_w_router[j], moe_w_gu[j], moe_w_down[j])
            fs = moe_swiglu(hs, moe_w_router[j], moe_w_gu[j], moe_w_down[j])
        hp = layer_norm(ALPHA * hp + fp, ln_g[i, 1], ln_b[i, 1])
        hs = layer_norm(ALPHA * hs + fs, ln_g[i, 1], ln_b[i, 1])
    gdn_S_prompt = jnp.stack(s_p)
    gdn_S_sample = jnp.stack(s_s)
    gdn_conv_prompt = jnp.stack(cv_p)
    gdn_conv_sample = jnp.stack(cv_s)
    sconv_prompt = jnp.stack(sc_p)
    sconv_sample = jnp.stack(sc_s)
    return (hp, hs, gdn_S_prompt, gdn_S_sample, gdn_conv_prompt, gdn_conv_sample, sconv_prompt, sconv_sample)
```

```python
import functools

import jax
import jax.numpy as jnp
from jax import lax
from jax.experimental import pallas as pl
from jax.experimental.pallas import tpu as pltpu

F32 = jnp.float32
BF16 = jnp.bfloat16

DEPTH = 2
ALPHA = (2.0 * DEPTH) ** 0.25
LN_EPS = 1e-5
NORM_EPS = 1e-6
GDN_HEAD_DIM = 128
GDN_CHUNK = 64
GDN_SUB = 16
TOP_K = 2

LANES = 128
SUBLANES = 8
VMEM_LIMIT = 56 * 1024 * 1024
NEG_BIG = -1e30

TM_PROMPT = 1024
TM_DOWN = 512
GDN_TT = 512
TM_EXPERT = 512
TM_TOKEN = 384
DEC_BB = 8


def _cparams(n_axes, vmem=VMEM_LIMIT):
    return pltpu.CompilerParams(
        dimension_semantics=("arbitrary",) * n_axes, vmem_limit_bytes=vmem)


def _bdot(a, b):
    return jnp.dot(a.astype(BF16), b.astype(BF16), preferred_element_type=F32)


def _bdot_nt(a, b):
    return lax.dot_general(a.astype(BF16), b.astype(BF16),
                           (((1,), (1,)), ((), ())), preferred_element_type=F32)


def _bdot_tn(a, b):
    return lax.dot_general(a.astype(BF16), b.astype(BF16),
                           (((0,), (0,)), ((), ())), preferred_element_type=F32)


def _silu(x):
    return x * jax.nn.sigmoid(x)


def _layer_norm(r, g, b):
    mu = jnp.mean(r, axis=-1, keepdims=True)
    d = r - mu
    var = jnp.mean(d * d, axis=-1, keepdims=True)
    return d * lax.rsqrt(var + LN_EPS) * g + b


def _mm_kernel(x_ref, w_ref, o_ref, wbf_ref):
    @pl.when(pl.program_id(1) == 0)
    def _():
        wbf_ref[...] = w_ref[...].astype(BF16)

    o_ref[...] = jnp.dot(x_ref[...].astype(BF16), wbf_ref[...],
                         preferred_element_type=F32).astype(o_ref.dtype)


def _matmul(x, w, n_cols, tm, tn, out_dtype, name):
    m, k = x.shape
    return pl.pallas_call(
        _mm_kernel,
        grid=(n_cols // tn, m // tm),
        in_specs=[pl.BlockSpec((tm, k), lambda j, i: (i, 0)),
                  pl.BlockSpec((k, tn), lambda j, i: (0, j))],
        out_specs=pl.BlockSpec((tm, tn), lambda j, i: (i, j)),
        out_shape=jax.ShapeDtypeStruct((m, n_cols), out_dtype),
        scratch_shapes=[pltpu.VMEM((k, tn), BF16)],
        compiler_params=_cparams(2),
        name=name,
    )(x, w)


def _mm_swiglu_kernel(x_ref, wg_ref, wu_ref, o_ref, wg_bf, wu_bf):
    @pl.when(pl.program_id(1) == 0)
    def _():
        wg_bf[...] = wg_ref[...].astype(BF16)
        wu_bf[...] = wu_ref[...].astype(BF16)

    xb = x_ref[...].astype(BF16)
    g = jnp.dot(xb, wg_bf[...], preferred_element_type=F32)
    u = jnp.dot(xb, wu_bf[...], preferred_element_type=F32)
    o_ref[...] = (_silu(g) * u).astype(o_ref.dtype)


def _matmul_swiglu(x, w_gu, tm, tn, name):
    m, k = x.shape
    f = w_gu.shape[1] // 2
    nj = f // tn
    return pl.pallas_call(
        _mm_swiglu_kernel,
        grid=(nj, m // tm),
        in_specs=[pl.BlockSpec((tm, k), lambda j, i: (i, 0)),
                  pl.BlockSpec((k, tn), lambda j, i: (0, j)),
                  pl.BlockSpec((k, tn), lambda j, i: (0, j + nj))],
        out_specs=pl.BlockSpec((tm, tn), lambda j, i: (i, j)),
        out_shape=jax.ShapeDtypeStruct((m, f), BF16),
        scratch_shapes=[pltpu.VMEM((k, tn), BF16), pltpu.VMEM((k, tn), BF16)],
        compiler_params=_cparams(2),
        name=name,
    )(x, w_gu, w_gu)


def _mm_ln_kernel(x_ref, w_ref, res_ref, g_ref, b_ref, o_ref, wbf_ref):
    @pl.when(pl.program_id(0) == 0)
    def _():
        wbf_ref[...] = w_ref[...].astype(BF16)

    y = jnp.dot(x_ref[...].astype(BF16), wbf_ref[...], preferred_element_type=F32)
    o_ref[...] = _layer_norm(ALPHA * res_ref[...] + y, g_ref[...], b_ref[...])


def _matmul_ln(x, w, res, ln_g, ln_b, tm, name):
    m, k = x.shape
    d = w.shape[1]
    return pl.pallas_call(
        _mm_ln_kernel,
        grid=(m // tm,),
        in_specs=[pl.BlockSpec((tm, k), lambda i: (i, 0)),
                  pl.BlockSpec((k, d), lambda i: (0, 0)),
                  pl.BlockSpec((tm, d), lambda i: (i, 0)),
                  pl.BlockSpec((1, d), lambda i: (0, 0)),
                  pl.BlockSpec((1, d), lambda i: (0, 0))],
        out_specs=pl.BlockSpec((tm, d), lambda i: (i, 0)),
        out_shape=jax.ShapeDtypeStruct((m, d), F32),
        scratch_shapes=[pltpu.VMEM((k, d), BF16)],
        compiler_params=_cparams(1),
        name=name,
    )(x, w, res, ln_g.reshape(1, d), ln_b.reshape(1, d))


def _gates_kernel(x_ref, w_ref, alog_ref, dtb_ref, o_ref, *, heads, chunk_cumsum):
    a = _bdot(x_ref[...], w_ref[...])
    z = a + dtb_ref[...]
    softplus = jnp.maximum(z, 0.0) + jnp.log1p(jnp.exp(-jnp.abs(z)))
    g = -jnp.exp(alog_ref[...]) * softplus
    if chunk_cumsum:
        row = jnp.bitwise_and(lax.broadcasted_iota(jnp.int32, g.shape, 0), GDN_CHUNK - 1)
        step = 1
        while step < GDN_CHUNK:
            g = g + jnp.where(row >= step, pltpu.roll(g, step, axis=0), 0.0)
            step *= 2
    lane = lax.broadcasted_iota(jnp.int32, g.shape, 1)
    o_ref[...] = jnp.where(lane < heads, g, jax.nn.sigmoid(a))


def _gdn_gates(x, w_ab, alog_row, dtb_row, heads, tm, chunk_cumsum, name):
    m, k = x.shape
    return pl.pallas_call(
        functools.partial(_gates_kernel, heads=heads, chunk_cumsum=chunk_cumsum),
        grid=(m // tm,),
        in_specs=[pl.BlockSpec((tm, k), lambda i: (i, 0)),
                  pl.BlockSpec((k, LANES), lambda i: (0, 0)),
                  pl.BlockSpec((1, LANES), lambda i: (0, 0)),
                  pl.BlockSpec((1, LANES), lambda i: (0, 0))],
        out_specs=pl.BlockSpec((tm, LANES), lambda i: (i, 0)),
        out_shape=jax.ShapeDtypeStruct((m, LANES), F32),
        compiler_params=_cparams(1),
        name=name,
    )(x, w_ab, alog_row, dtb_row)


def _qkv_finish(conv, j, heads):
    y = _silu(conv)
    inv = lax.rsqrt(jnp.sum(y * y, axis=-1, keepdims=True) + NORM_EPS)
    scale = jnp.where(j < heads, inv * (GDN_HEAD_DIM ** -0.5),
                      jnp.where(j < 2 * heads, inv, 1.0))
    return y * scale


def _gdn_prep_prompt_kernel(x_ref, w_ref, o_ref, *, heads):
    x = x_ref[0]
    w = w_ref[...]
    width = w.shape[0]
    row = lax.broadcasted_iota(jnp.int32, x.shape, 0)
    conv = w[width - 1:width, :] * x
    for s in range(1, width):
        shifted = jnp.where(row >= s, pltpu.roll(x, s, axis=0), 0.0)
        conv = conv + w[width - 1 - s:width - s, :] * shifted
    o_ref[0] = _qkv_finish(conv, pl.program_id(1), heads)


def _gdn_prep_prompt(proj, conv_w, heads):
    bsz, t, _ = proj.shape
    width, ch = conv_w.shape
    return pl.pallas_call(
        functools.partial(_gdn_prep_prompt_kernel, heads=heads),
        grid=(bsz, ch // GDN_HEAD_DIM),
        in_specs=[pl.BlockSpec((1, t, GDN_HEAD_DIM), lambda b, j: (b, 0, j)),
                  pl.BlockSpec((width, GDN_HEAD_DIM), lambda b, j: (0, j))],
        out_specs=pl.BlockSpec((1, t, GDN_HEAD_DIM), lambda b, j: (b, 0, j)),
        out_shape=jax.ShapeDtypeStruct((bsz, t, ch), F32),
        compiler_params=_cparams(2),
        name="gdn_prep_prompt",
    )(proj, conv_w)


def _gdn_prep_sample_kernel(x_ref, buf_ref, w_ref, o_ref, *, heads):
    w = w_ref[...]
    width = w.shape[0]
    conv = w[width - 1:width, :] * x_ref[...]
    for s in range(width - 1):
        conv = conv + w[s:s + 1, :] * buf_ref[s]
    o_ref[...] = _qkv_finish(conv, pl.program_id(0), heads)


def _gdn_prep_sample(proj, buf_t, conv_w, heads):
    bsz = proj.shape[0]
    width, ch = conv_w.shape
    return pl.pallas_call(
        functools.partial(_gdn_prep_sample_kernel, heads=heads),
        grid=(ch // GDN_HEAD_DIM,),
        in_specs=[pl.BlockSpec((bsz, GDN_HEAD_DIM), lambda j: (0, j)),
                  pl.BlockSpec((width - 1, bsz, GDN_HEAD_DIM), lambda j: (0, 0, j)),
                  pl.BlockSpec((width, GDN_HEAD_DIM), lambda j: (0, j))],
        out_specs=pl.BlockSpec((bsz, GDN_HEAD_DIM), lambda j: (0, j)),
        out_shape=jax.ShapeDtypeStruct((bsz, ch), F32),
        compiler_params=_cparams(1),
        name="gdn_prep_sample",
    )(proj, buf_t, conv_w)


def _gated_rmsnorm(o, z, norm_w):
    on = o * lax.rsqrt(jnp.mean(o * o, axis=-1, keepdims=True) + NORM_EPS) * norm_w
    return on * _silu(z)


def _gdn_chunk_kernel(qkv_ref, z_ref, gates_ref, grow_ref, nw_ref, og_ref, sout_ref,
                      s_ref, *, heads):
    c_len = GDN_CHUNK
    dh = GDN_HEAD_DIM
    t_step = pl.program_id(1)

    @pl.when(t_step == 0)
    def _():
        s_ref[...] = jnp.zeros_like(s_ref)

    row = lax.broadcasted_iota(jnp.int32, (c_len, c_len), 0)
    col = lax.broadcasted_iota(jnp.int32, (c_len, c_len), 1)
    tril = row >= col
    strict = row > col
    same_sub = jnp.bitwise_and(row, -GDN_SUB) == jnp.bitwise_and(col, -GDN_SUB)
    norm_w = nw_ref[...]
    n_chunks = qkv_ref.shape[1] // c_len

    def chunk_body(c, carry):
        rows = pl.ds(pl.multiple_of(c * c_len, c_len), c_len)
        gate_tile = gates_ref[0, rows, :]
        grow_tile = grow_ref[0, c]
        for h in range(heads):
            q = qkv_ref[0, rows, h * dh:(h + 1) * dh]
            k = qkv_ref[0, rows, (heads + h) * dh:(heads + h + 1) * dh]
            v = qkv_ref[0, rows, (2 * heads + h) * dh:(2 * heads + h + 1) * dh]
            gcol = gate_tile[:, h:h + 1]
            beta = gate_tile[:, heads + h:heads + h + 1]
            grow = grow_tile[h:h + 1, :]
            decay = jnp.exp(jnp.where(tril, gcol - grow, NEG_BIG))
            kb = k * beta
            a_mat = jnp.where(strict, _bdot_nt(kb, k) * decay, 0.0)
            qk = jnp.where(tril, _bdot_nt(q, k) * decay, 0.0)

            d1 = jnp.where(same_sub, a_mat, 0.0)
            e_mat = a_mat - d1
            x = -d1
            p = d1
            span = 1
            while 2 * span < GDN_SUB:
                p = _bdot(p, p)
                x = x + p + _bdot(x, p)
                span *= 2
            rhs = jnp.concatenate([v * beta, kb * jnp.exp(gcol)], axis=-1)
            n_mat = e_mat + _bdot(x, e_mat)
            r = rhs + _bdot(x, rhs)
            r = r - _bdot(n_mat, r)
            blocks = c_len // GDN_SUB
            span = 2
            while span < blocks:
                n_mat = _bdot(n_mat, n_mat)
                r = r + _bdot(n_mat, r)
                span *= 2
            u_v = r[:, :dh]
            u_w = r[:, dh:]

            s = s_ref[h]
            v_new = u_v - _bdot(u_w, s)
            o = _bdot(q * jnp.exp(gcol), s) + _bdot(qk, v_new)
            g_last = gcol[c_len - 1:c_len, :]
            k_dec = k * jnp.exp(g_last - gcol)
            s_ref[h] = s * jnp.exp(g_last) + _bdot_tn(k_dec, v_new)

            z = z_ref[0, rows, h * dh:(h + 1) * dh]
            og_ref[0, rows, h * dh:(h + 1) * dh] = _gated_rmsnorm(o, z, norm_w).astype(og_ref.dtype)
        return carry

    lax.fori_loop(0, n_chunks, chunk_body, 0)

    @pl.when(t_step == pl.num_programs(1) - 1)
    def _():
        sout_ref[0] = s_ref[...]


def _gdn_chunked(qkvn, proj, gates, grow, norm_w, heads):
    bsz, t, _ = qkvn.shape
    dh = GDN_HEAD_DIM
    vw = heads * dh
    n_t = t // GDN_TT
    cpt = GDN_TT // GDN_CHUNK
    return pl.pallas_call(
        functools.partial(_gdn_chunk_kernel, heads=heads),
        grid=(bsz, n_t),
        in_specs=[pl.BlockSpec((1, GDN_TT, 3 * vw), lambda b, i: (b, i, 0)),
                  pl.BlockSpec((1, GDN_TT, vw), lambda b, i: (b, i, 3)),
                  pl.BlockSpec((1, GDN_TT, LANES), lambda b, i: (b, i, 0)),
                  pl.BlockSpec((1, cpt, heads, GDN_CHUNK), lambda b, i: (b, i, 0, 0)),
                  pl.BlockSpec((1, dh), lambda b, i: (0, 0))],
        out_specs=[pl.BlockSpec((1, GDN_TT, vw), lambda b, i: (b, i, 0)),
                   pl.BlockSpec((1, heads, dh, dh), lambda b, i: (b, 0, 0, 0))],
        out_shape=[jax.ShapeDtypeStruct((bsz, t, vw), BF16),
                   jax.ShapeDtypeStruct((bsz, heads, dh, dh), F32)],
        scratch_shapes=[pltpu.VMEM((heads, dh, dh), F32)],
        compiler_params=_cparams(2),
        name="gdn_chunked",
    )(qkvn, proj, gates, grow, norm_w.reshape(1, dh))


def _gdn_step_kernel(qkv_ref, z_ref, gates_ref, s0_ref, nw_ref, og_ref, sout_ref, *, heads):
    dh = GDN_HEAD_DIM
    pad = 2 * SUBLANES
    norm_w = nw_ref[...]
    prow = lax.broadcasted_iota(jnp.int32, (pad, dh), 0)

    for bi in range(qkv_ref.shape[0]):
        r1 = slice(bi, bi + 1)
        gate_row = gates_ref[r1, :]
        for h in range(heads):
            q = jnp.broadcast_to(qkv_ref[r1, h * dh:(h + 1) * dh], (pad, dh))
            k = jnp.broadcast_to(qkv_ref[r1, (heads + h) * dh:(heads + h + 1) * dh], (pad, dh))
            v = jnp.broadcast_to(qkv_ref[r1, (2 * heads + h) * dh:(2 * heads + h + 1) * dh], (pad, dh))
            a = jnp.exp(gate_row[:, h:h + 1])
            beta = gate_row[:, heads + h:heads + h + 1]
            s = s0_ref[bi, h]
            delta = (v - a * _bdot(k, s)) * beta
            k_hi = k.astype(BF16).astype(F32)
            d_hi = delta.astype(BF16).astype(F32)
            k_parts = jnp.where(prow < 2, k_hi, jnp.where(prow == 2, k - k_hi, 0.0))
            d_parts = jnp.where(prow == 1, delta - d_hi, jnp.where(prow < 3, d_hi, 0.0))
            s_new = s * a + _bdot_tn(k_parts, d_parts)
            sout_ref[bi, h] = s_new
            o = _bdot(q, s_new)[0:1, :]
            og_ref[r1, h * dh:(h + 1) * dh] = _gated_rmsnorm(o, z_ref[r1, h * dh:(h + 1) * dh], norm_w)


def _gdn_step(qkvn, proj, gates, s0, norm_w, heads):
    bsz = qkvn.shape[0]
    dh = GDN_HEAD_DIM
    vw = heads * dh
    bb = DEC_BB
    return pl.pallas_call(
        functools.partial(_gdn_step_kernel, heads=heads),
        grid=(bsz // bb,),
        in_specs=[pl.BlockSpec((bb, 3 * vw), lambda i: (i, 0)),
                  pl.BlockSpec((bb, vw), lambda i: (i, 3)),
                  pl.BlockSpec((bb, LANES), lambda i: (i, 0)),
                  pl.BlockSpec((bb, heads, dh, dh), lambda i: (i, 0, 0, 0)),
                  pl.BlockSpec((1, dh), lambda i: (0, 0))],
        out_specs=[pl.BlockSpec((bb, vw), lambda i: (i, 0)),
                   pl.BlockSpec((bb, heads, dh, dh), lambda i: (i, 0, 0, 0))],
        out_shape=[jax.ShapeDtypeStruct((bsz, vw), F32),
                   jax.ShapeDtypeStruct((bsz, heads, dh, dh), F32)],
        compiler_params=_cparams(1),
        name="gdn_step",
    )(qkvn, proj, gates, s0, norm_w.reshape(1, dh))


def _sconv_prompt_kernel(b_ref, c_ref, h_ref, w_ref, o_ref, last_ref):
    ch = c_ref[0] * h_ref[0]
    w = w_ref[...]
    width = w.shape[0]
    t = ch.shape[0]
    row = lax.broadcasted_iota(jnp.int32, ch.shape, 0)
    conv = w[width - 1:width, :] * ch
    for s in range(1, width):
        conv = conv + w[width - 1 - s:width - s, :] * jnp.where(row >= s, pltpu.roll(ch, s, axis=0), 0.0)
    o_ref[0] = (b_ref[0] * conv).astype(o_ref.dtype)
    last_ref[0] = ch[t - (width - 1):, :]


def _sconv_prompt(proj, conv_w):
    bsz, t, _ = proj.shape
    width, d = conv_w.shape
    nb = d // LANES
    blk = lambda off: pl.BlockSpec((1, t, LANES), lambda b, j: (b, 0, j + off))
    return pl.pallas_call(
        _sconv_prompt_kernel,
        grid=(bsz, nb),
        in_specs=[blk(0), blk(nb), blk(2 * nb),
                  pl.BlockSpec((width, LANES), lambda b, j: (0, j))],
        out_specs=[pl.BlockSpec((1, t, LANES), lambda b, j: (b, 0, j)),
                   pl.BlockSpec((1, width - 1, LANES), lambda b, j: (b, 0, j))],
        out_shape=[jax.ShapeDtypeStruct((bsz, t, d), BF16),
                   jax.ShapeDtypeStruct((bsz, width - 1, d), F32)],
        compiler_params=_cparams(2),
        name="sconv_prompt",
    )(proj, proj, proj, conv_w)


def _sconv_sample_kernel(b_ref, c_ref, h_ref, buf_ref, w_ref, o_ref, ch_ref):
    ch = c_ref[...] * h_ref[...]
    w = w_ref[...]
    width = w.shape[0]
    conv = w[width - 1:width, :] * ch
    for s in range(width - 1):
        conv = conv + w[s:s + 1, :] * buf_ref[s]
    o_ref[...] = b_ref[...] * conv
    ch_ref[...] = ch


def _sconv_sample(proj, buf_t, conv_w):
    bsz = proj.shape[0]
    width, d = conv_w.shape
    nb = d // LANES
    blk = lambda off: pl.BlockSpec((bsz, LANES), lambda j: (0, j + off))
    return pl.pallas_call(
        _sconv_sample_kernel,
        grid=(nb,),
        in_specs=[blk(0), blk(nb), blk(2 * nb),
                  pl.BlockSpec((width - 1, bsz, LANES), lambda j: (0, 0, j)),
                  pl.BlockSpec((width, LANES), lambda j: (0, j))],
        out_specs=[pl.BlockSpec((bsz, LANES), lambda j: (0, j)),
                   pl.BlockSpec((bsz, LANES), lambda j: (0, j))],
        out_shape=[jax.ShapeDtypeStruct((bsz, d), F32),
                   jax.ShapeDtypeStruct((bsz, d), F32)],
        compiler_params=_cparams(1),
        name="sconv_sample",
    )(proj, proj, proj, buf_t, conv_w)


def _router_kernel(x_ref, w_ref, o_ref, *, n_experts):
    logits = jnp.dot(x_ref[...], w_ref[...], preferred_element_type=F32,
                     precision=lax.Precision.HIGHEST)
    lane = lax.broadcasted_iota(jnp.int32, logits.shape, 1).astype(F32)
    lg = jnp.where(lane < n_experts, logits, NEG_BIG)
    m1 = jnp.max(lg, axis=-1, keepdims=True)
    i1 = jnp.min(jnp.where(lg == m1, lane, float(LANES)), axis=-1, keepdims=True)
    lg2 = jnp.where(lane == i1, NEG_BIG, lg)
    m2 = jnp.max(lg2, axis=-1, keepdims=True)
    i2 = jnp.min(jnp.where(lg2 == m2, lane, float(LANES)), axis=-1, keepdims=True)
    e = jnp.exp(m2 - m1)
    g1 = 1.0 / (1.0 + e)
    g2 = e / (1.0 + e)
    o_ref[...] = jnp.where(lane == 0, i1, jnp.where(lane == 1, i2,
                           jnp.where(lane == 2, g1, jnp.where(lane == 3, g2, 0.0))))


def _router(x, w_pad, n_experts):
    m, k = x.shape
    tm = TM_TOKEN
    return pl.pallas_call(
        functools.partial(_router_kernel, n_experts=n_experts),
        grid=(m // tm,),
        in_specs=[pl.BlockSpec((tm, k), lambda i: (i, 0)),
                  pl.BlockSpec((k, LANES), lambda i: (0, 0))],
        out_specs=pl.BlockSpec((tm, LANES), lambda i: (i, 0)),
        out_shape=jax.ShapeDtypeStruct((m, LANES), F32),
        compiler_params=_cparams(1),
        name="moe_router",
    )(x, w_pad)


def _row_copy(src_hbm, src_row, dst_ref, dst_row, sem):
    return pltpu.make_async_copy(src_hbm.at[pl.ds(src_row, 1)], dst_ref.at[pl.ds(dst_row, 1)], sem)


def _gather_kernel(idx_ref, nu_ref, src_hbm, o_ref, sem):
    i = pl.program_id(0)
    tg = o_ref.shape[0]
    base = i * tg

    @pl.when(i < nu_ref[0])
    def _():
        def issue(r, c):
            _row_copy(src_hbm, idx_ref[base + r], o_ref, r, sem).start()
            return c
        lax.fori_loop(0, tg, issue, 0)

        def drain(r, c):
            _row_copy(src_hbm, 0, o_ref, r, sem).wait()
            return c
        lax.fori_loop(0, tg, drain, 0)

    @pl.when(i >= nu_ref[0])
    def _():
        o_ref[...] = jnp.zeros_like(o_ref)


def _gather_rows(src, row_token, n_used, tg):
    rows = row_token.shape[0]
    d = src.shape[1]
    return pl.pallas_call(
        _gather_kernel,
        grid_spec=pltpu.PrefetchScalarGridSpec(
            num_scalar_prefetch=2,
            grid=(rows // tg,),
            in_specs=[pl.BlockSpec(memory_space=pl.ANY)],
            out_specs=pl.BlockSpec((tg, d), lambda i, idx, nu: (i, 0)),
            scratch_shapes=[pltpu.SemaphoreType.DMA(())]),
        out_shape=jax.ShapeDtypeStruct((rows, d), src.dtype),
        compiler_params=_cparams(1),
        name="moe_gather",
    )(row_token, n_used, src)


def _moe_ffn1_kernel(te_ref, nu_ref, x_ref, wg_ref, wu_ref, o_ref, wg_bf, wu_bf):
    i = pl.program_id(1)
    new_expert = jnp.logical_or(i == 0, te_ref[i] != te_ref[jnp.maximum(i - 1, 0)])

    @pl.when(new_expert)
    def _():
        wg_bf[...] = wg_ref[0].astype(BF16)
        wu_bf[...] = wu_ref[0].astype(BF16)

    @pl.when(i < nu_ref[0])
    def _():
        xb = x_ref[...].astype(BF16)
        g = jnp.dot(xb, wg_bf[...], preferred_element_type=F32)
        u = jnp.dot(xb, wu_bf[...], preferred_element_type=F32)
        o_ref[...] = (_silu(g) * u).astype(o_ref.dtype)

    @pl.when(i >= nu_ref[0])
    def _():
        o_ref[...] = jnp.zeros_like(o_ref)


def _moe_ffn1(xs, w_gu, tile_expert, n_used, tm, tn):
    rows, k = xs.shape
    f = w_gu.shape[2] // 2
    nj = f // tn
    return pl.pallas_call(
        _moe_ffn1_kernel,
        grid_spec=pltpu.PrefetchScalarGridSpec(
            num_scalar_prefetch=2,
            grid=(nj, rows // tm),
            in_specs=[pl.BlockSpec((tm, k), lambda j, i, te, nu: (i, 0)),
                      pl.BlockSpec((1, k, tn), lambda j, i, te, nu: (te[i], 0, j)),
                      pl.BlockSpec((1, k, tn), lambda j, i, te, nu: (te[i], 0, j + nj))],
            out_specs=pl.BlockSpec((tm, tn), lambda j, i, te, nu: (i, j)),
            scratch_shapes=[pltpu.VMEM((k, tn), BF16), pltpu.VMEM((k, tn), BF16)]),
        out_shape=jax.ShapeDtypeStruct((rows, f), BF16),
        compiler_params=_cparams(2),
        name="moe_ffn1",
    )(tile_expert, n_used, xs, w_gu, w_gu)


def _moe_ffn2_kernel(te_ref, nu_ref, x_ref, w_ref, o_ref, w_bf):
    i = pl.program_id(1)
    new_expert = jnp.logical_or(i == 0, te_ref[i] != te_ref[jnp.maximum(i - 1, 0)])

    @pl.when(new_expert)
    def _():
        w_bf[...] = w_ref[0].astype(BF16)

    @pl.when(i < nu_ref[0])
    def _():
        o_ref[...] = jnp.dot(x_ref[...], w_bf[...], preferred_element_type=F32)

    @pl.when(i >= nu_ref[0])
    def _():
        o_ref[...] = jnp.zeros_like(o_ref)


def _moe_ffn2(act, w_down, tile_expert, n_used, tm, tn):
    rows, k = act.shape
    d = w_down.shape[2]
    return pl.pallas_call(
        _moe_ffn2_kernel,
        grid_spec=pltpu.PrefetchScalarGridSpec(
            num_scalar_prefetch=2,
            grid=(d // tn, rows // tm),
            in_specs=[pl.BlockSpec((tm, k), lambda j, i, te, nu: (i, 0)),
                      pl.BlockSpec((1, k, tn), lambda j, i, te, nu: (te[i], 0, j))],
            out_specs=pl.BlockSpec((tm, tn), lambda j, i, te, nu: (i, j)),
            scratch_shapes=[pltpu.VMEM((k, tn), BF16)]),
        out_shape=jax.ShapeDtypeStruct((rows, d), F32),
        compiler_params=_cparams(2),
        name="moe_ffn2",
    )(tile_expert, n_used, act, w_down)


def _combine_kernel(pos_ref, ys_hbm, route_ref, res_ref, g_ref, b_ref, o_ref, buf_a, buf_b, sem):
    tm = o_ref.shape[0]
    base = pl.program_id(0) * tm

    def issue(r, c):
        t2 = 2 * (base + r)
        _row_copy(ys_hbm, pos_ref[t2], buf_a, r, sem).start()
        _row_copy(ys_hbm, pos_ref[t2 + 1], buf_b, r, sem).start()
        return c
    lax.fori_loop(0, tm, issue, 0)

    def drain(r, c):
        _row_copy(ys_hbm, 0, buf_a, r, sem).wait()
        _row_copy(ys_hbm, 0, buf_b, r, sem).wait()
        return c
    lax.fori_loop(0, tm, drain, 0)

    route = route_ref[...]
    y = route[:, 2:3] * buf_a[...] + route[:, 3:4] * buf_b[...]
    o_ref[...] = _layer_norm(ALPHA * res_ref[...] + y, g_ref[...], b_ref[...])


def _moe_combine(ys, pos, route, res, ln_g, ln_b):
    m, d = res.shape
    tm = TM_TOKEN
    return pl.pallas_call(
        _combine_kernel,
        grid_spec=pltpu.PrefetchScalarGridSpec(
            num_scalar_prefetch=1,
            grid=(m // tm,),
            in_specs=[pl.BlockSpec(memory_space=pl.ANY),
                      pl.BlockSpec((tm, LANES), lambda i, p: (i, 0)),
                      pl.BlockSpec((tm, d), lambda i, p: (i, 0)),
                      pl.BlockSpec((1, d), lambda i, p: (0, 0)),
                      pl.BlockSpec((1, d), lambda i, p: (0, 0))],
            out_specs=pl.BlockSpec((tm, d), lambda i, p: (i, 0)),
            scratch_shapes=[pltpu.VMEM((tm, d), F32), pltpu.VMEM((tm, d), F32),
                            pltpu.SemaphoreType.DMA(())]),
        out_shape=jax.ShapeDtypeStruct((m, d), F32),
        compiler_params=_cparams(1),
        name="moe_combine",
    )(pos, ys, route, res, ln_g.reshape(1, d), ln_b.reshape(1, d))


def _dispatch_plan(route, n_experts, tm):
    n_tok = route.shape[0]
    ids = route[:, :TOP_K].astype(jnp.int32).reshape(-1)
    n_pairs = ids.shape[0]
    n_tiles = n_pairs // tm + n_experts
    onehot = (ids[:, None] == jnp.arange(n_experts, dtype=jnp.int32)[None, :]).astype(jnp.int32)
    rank = jnp.sum((jnp.cumsum(onehot, axis=0) - onehot) * onehot, axis=1)
    counts = jnp.sum(onehot, axis=0)
    tiles_e = (counts + tm - 1) // tm
    tile_end = jnp.cumsum(tiles_e)
    group_off = (tile_end - tiles_e) * tm
    pos = group_off[ids] + rank
    row_token = jnp.zeros((n_tiles * tm,), jnp.int32).at[pos].set(
        jnp.arange(n_pairs, dtype=jnp.int32) // TOP_K)
    n_used = tile_end[n_experts - 1:]
    tile_ids = jnp.arange(n_tiles, dtype=jnp.int32)
    tile_expert = jnp.sum((tile_ids[:, None] >= tile_end[None, :]).astype(jnp.int32), axis=1)
    last_expert = jnp.sum((n_used - 1 >= tile_end).astype(jnp.int32))
    tile_expert = jnp.minimum(tile_expert, last_expert).astype(jnp.int32)
    return row_token, pos.astype(jnp.int32), tile_expert, n_used.astype(jnp.int32)


def _moe_layer(h_all, w_router, w_gu, w_down, ln_g, ln_b):
    n_experts = w_router.shape[1]
    w_pad = jnp.pad(w_router, ((0, 0), (0, LANES - n_experts)))
    route = _router(h_all, w_pad, n_experts)
    row_token, pos, tile_expert, n_used = _dispatch_plan(route, n_experts, TM_EXPERT)
    xs = _gather_rows(h_all, row_token, n_used, TM_EXPERT)
    act = _moe_ffn1(xs, w_gu, tile_expert, n_used, TM_EXPERT, 896)
    ys = _moe_ffn2(act, w_down, tile_expert, n_used, TM_EXPERT, 512)
    return _moe_combine(ys, pos, route, h_all, ln_g, ln_b)


def kernel(x_prompt, x_sample, state_gdn_S, state_gdn_conv, state_sconv, ln_g, ln_b, gdn_w_in, gdn_conv_w, gdn_a_log, gdn_dt_bias, gdn_norm_w, gdn_w_out, sc_w_in, sc_conv_w, sc_w_out, ffn_w_gu, ffn_w_down, moe_w_router, moe_w_gu, moe_w_down):
    bsz, t, d = x_prompt.shape
    dec = x_sample.shape[0]
    heads = gdn_a_log.shape[1]
    dh = GDN_HEAD_DIM
    vw = heads * dh
    conv_ch = gdn_conv_w.shape[2]
    assert x_sample.shape[1] == 1 and conv_ch == 3 * vw and gdn_w_in.shape[2] == conv_ch + vw + 2 * heads
    assert t % GDN_TT == 0 and (bsz * t) % TM_PROMPT == 0 and (bsz * t + dec) % TM_TOKEN == 0
    n_p = bsz * t

    xp = x_prompt.reshape(n_p, d)
    xs = x_sample.reshape(dec, d)

    w_in = gdn_w_in[0]
    w_ab = jnp.pad(w_in[:, conv_ch + vw:], ((0, 0), (0, LANES - 2 * heads)))
    alog_row = jnp.pad(gdn_a_log[0], (0, LANES - heads)).reshape(1, LANES)
    dtb_row = jnp.pad(gdn_dt_bias[0], (0, LANES - heads)).reshape(1, LANES)

    proj_p = _matmul(xp, w_in, conv_ch + vw, TM_PROMPT, 1024, F32, "gdn_in_prompt")
    proj_s = _matmul(xs, w_in, conv_ch + vw, dec, 1024, F32, "gdn_in_sample")
    gates_p = _gdn_gates(xp, w_ab, alog_row, dtb_row, heads, TM_PROMPT, True, "gdn_gates_prompt")
    gates_s = _gdn_gates(xs, w_ab, alog_row, dtb_row, heads, dec, False, "gdn_gates_sample")

    proj_p3 = proj_p.reshape(bsz, t, conv_ch + vw)
    qkvn_p = _gdn_prep_prompt(proj_p3, gdn_conv_w[0], heads)
    conv_buf_t = jnp.transpose(state_gdn_conv[0], (1, 0, 2))
    qkvn_s = _gdn_prep_sample(proj_s, conv_buf_t, gdn_conv_w[0], heads)

    grow = gates_p[:, :heads].reshape(bsz, t // GDN_CHUNK, GDN_CHUNK, heads).transpose(0, 1, 3, 2)
    og_p, s_prompt = _gdn_chunked(qkvn_p, proj_p3, gates_p.reshape(bsz, t, LANES), grow,
                                  gdn_norm_w[0], heads)
    og_s, s_sample = _gdn_step(qkvn_s, proj_s, gates_s, state_gdn_S[0], gdn_norm_w[0], heads)

    conv_prompt = proj_p3[:, t - (gdn_conv_w.shape[1] - 1):, :conv_ch]
    conv_sample = jnp.concatenate([state_gdn_conv[0][:, 1:], proj_s[:, None, :conv_ch]], axis=1)

    h_p = _matmul_ln(og_p.reshape(n_p, vw), gdn_w_out[0], xp, ln_g[0, 0], ln_b[0, 0], TM_PROMPT, "gdn_out_prompt")
    h_s = _matmul_ln(og_s, gdn_w_out[0], xs, ln_g[0, 0], ln_b[0, 0], dec, "gdn_out_sample")

    d_ff = ffn_w_down.shape[1]
    act_p = _matmul_swiglu(h_p, ffn_w_gu[0], TM_PROMPT, d_ff // 2, "ffn_up_prompt")
    act_s = _matmul_swiglu(h_s, ffn_w_gu[0], dec, d_ff // 2, "ffn_up_sample")
    h_p = _matmul_ln(act_p, ffn_w_down[0], h_p, ln_g[0, 1], ln_b[0, 1], TM_DOWN, "ffn_down_prompt")
    h_s = _matmul_ln(act_s, ffn_w_down[0], h_s, ln_g[0, 1], ln_b[0, 1], dec, "ffn_down_sample")

    sproj_p = _matmul(h_p, sc_w_in[0], 3 * d, TM_PROMPT, 1024, F32, "sc_in_prompt")
    sproj_s = _matmul(h_s, sc_w_in[0], 3 * d, dec, 1024, F32, "sc_in_sample")
    sg_p, sconv_prompt = _sconv_prompt(sproj_p.reshape(bsz, t, 3 * d), sc_conv_w[0])
    sbuf_t = jnp.transpose(state_sconv[0], (1, 0, 2))
    sg_s, ch_s = _sconv_sample(sproj_s, sbuf_t, sc_conv_w[0])
    sconv_sample = jnp.concatenate([state_sconv[0][:, 1:], ch_s[:, None, :]], axis=1)

    h_p = _matmul_ln(sg_p.reshape(n_p, d), sc_w_out[0], h_p, ln_g[1, 0], ln_b[1, 0], TM_PROMPT, "sc_out_prompt")
    h_s = _matmul_ln(sg_s, sc_w_out[0], h_s, ln_g[1, 0], ln_b[1, 0], dec, "sc_out_sample")

    h_all = jnp.concatenate([h_p, h_s], axis=0)
    y_all = _moe_layer(h_all, moe_w_router[0], moe_w_gu[0], moe_w_down[0], ln_g[1, 1], ln_b[1, 1])

    y_prompt = y_all[:n_p].reshape(bsz, t, d)
    y_sample = y_all[n_p:].reshape(dec, 1, d)
    return (y_prompt, y_sample, s_prompt[None], s_sample[None], conv_prompt[None],
            conv_sample[None], sconv_prompt[None], sconv_sample[None])
```

```python
import functools

import jax
import jax.numpy as jnp
from jax import lax
from jax.experimental import pallas as pl
from jax.experimental.pallas import tpu as pltpu

F32 = jnp.float32
BF16 = jnp.bfloat16

DEPTH = 2
ALPHA = (2.0 * DEPTH) ** 0.25
LN_EPS = 1e-5
NORM_EPS = 1e-6
GDN_HEAD_DIM = 128
GDN_CHUNK = 64
GDN_SUB = 16
TOP_K = 2

LANES = 128
SUBLANES = 8
VMEM_LIMIT = 56 * 1024 * 1024
NEG_BIG = -1e30

TM_PROMPT = 1024
TM_DOWN = 512
GDN_TT = 256
GDN_GROUP = 4
TM_EXPERT = 512
TM_TOKEN = 384
DEC_BB = 8


def _cparams(n_axes, vmem=VMEM_LIMIT):
    return pltpu.CompilerParams(
        dimension_semantics=("arbitrary",) * n_axes, vmem_limit_bytes=vmem)


def _bdot(a, b):
    return jnp.dot(a.astype(BF16), b.astype(BF16), preferred_element_type=F32)


def _bdot_nt(a, b):
    return lax.dot_general(a.astype(BF16), b.astype(BF16),
                           (((1,), (1,)), ((), ())), preferred_element_type=F32)


def _bdot_tn(a, b):
    return lax.dot_general(a.astype(BF16), b.astype(BF16),
                           (((0,), (0,)), ((), ())), preferred_element_type=F32)


def _silu(x):
    return x * jax.nn.sigmoid(x)


def _layer_norm(r, g, b):
    mu = jnp.mean(r, axis=-1, keepdims=True)
    d = r - mu
    var = jnp.mean(d * d, axis=-1, keepdims=True)
    return d * lax.rsqrt(var + LN_EPS) * g + b


def _mm_kernel(x_ref, w_ref, o_ref, wbf_ref):
    @pl.when(pl.program_id(1) == 0)
    def _():
        wbf_ref[...] = w_ref[...].astype(BF16)

    o_ref[...] = jnp.dot(x_ref[...].astype(BF16), wbf_ref[...],
                         preferred_element_type=F32).astype(o_ref.dtype)


def _matmul(x, w, n_cols, tm, tn, out_dtype, name):
    m, k = x.shape
    return pl.pallas_call(
        _mm_kernel,
        grid=(n_cols // tn, m // tm),
        in_specs=[pl.BlockSpec((tm, k), lambda j, i: (i, 0)),
                  pl.BlockSpec((k, tn), lambda j, i: (0, j))],
        out_specs=pl.BlockSpec((tm, tn), lambda j, i: (i, j)),
        out_shape=jax.ShapeDtypeStruct((m, n_cols), out_dtype),
        scratch_shapes=[pltpu.VMEM((k, tn), BF16)],
        compiler_params=_cparams(2),
        name=name,
    )(x, w)


def _mm_swiglu_kernel(x_ref, wg_ref, wu_ref, o_ref, wg_bf, wu_bf):
    @pl.when(pl.program_id(1) == 0)
    def _():
        wg_bf[...] = wg_ref[...].astype(BF16)
        wu_bf[...] = wu_ref[...].astype(BF16)

    xb = x_ref[...].astype(BF16)
    g = jnp.dot(xb, wg_bf[...], preferred_element_type=F32)
    u = jnp.dot(xb, wu_bf[...], preferred_element_type=F32)
    o_ref[...] = (_silu(g) * u).astype(o_ref.dtype)


def _matmul_swiglu(x, w_gu, tm, tn, name):
    m, k = x.shape
    f = w_gu.shape[1] // 2
    nj = f // tn
    return pl.pallas_call(
        _mm_swiglu_kernel,
        grid=(nj, m // tm),
        in_specs=[pl.BlockSpec((tm, k), lambda j, i: (i, 0)),
                  pl.BlockSpec((k, tn), lambda j, i: (0, j)),
                  pl.BlockSpec((k, tn), lambda j, i: (0, j + nj))],
        out_specs=pl.BlockSpec((tm, tn), lambda j, i: (i, j)),
        out_shape=jax.ShapeDtypeStruct((m, f), BF16),
        scratch_shapes=[pltpu.VMEM((k, tn), BF16), pltpu.VMEM((k, tn), BF16)],
        compiler_params=_cparams(2),
        name=name,
    )(x, w_gu, w_gu)


def _mm_ln_kernel(x_ref, w_ref, res_ref, g_ref, b_ref, o_ref, wbf_ref):
    @pl.when(pl.program_id(0) == 0)
    def _():
        wbf_ref[...] = w_ref[...].astype(BF16)

    y = jnp.dot(x_ref[...].astype(BF16), wbf_ref[...], preferred_element_type=F32)
    o_ref[...] = _layer_norm(ALPHA * res_ref[...] + y, g_ref[...], b_ref[...])


def _matmul_ln(x, w, res, ln_g, ln_b, tm, name):
    m, k = x.shape
    d = w.shape[1]
    return pl.pallas_call(
        _mm_ln_kernel,
        grid=(m // tm,),
        in_specs=[pl.BlockSpec((tm, k), lambda i: (i, 0)),
                  pl.BlockSpec((k, d), lambda i: (0, 0)),
                  pl.BlockSpec((tm, d), lambda i: (i, 0)),
                  pl.BlockSpec((1, d), lambda i: (0, 0)),
                  pl.BlockSpec((1, d), lambda i: (0, 0))],
        out_specs=pl.BlockSpec((tm, d), lambda i: (i, 0)),
        out_shape=jax.ShapeDtypeStruct((m, d), F32),
        scratch_shapes=[pltpu.VMEM((k, d), BF16)],
        compiler_params=_cparams(1),
        name=name,
    )(x, w, res, ln_g.reshape(1, d), ln_b.reshape(1, d))


def _gates_kernel(x_ref, w_ref, alog_ref, dtb_ref, o_ref, *, heads, chunk_cumsum):
    a = _bdot(x_ref[...], w_ref[...])
    z = a + dtb_ref[...]
    softplus = jnp.maximum(z, 0.0) + jnp.log1p(jnp.exp(-jnp.abs(z)))
    g = -jnp.exp(alog_ref[...]) * softplus
    total = g
    if chunk_cumsum:
        n_rows = g.shape[0]
        row = jnp.bitwise_and(lax.broadcasted_iota(jnp.int32, g.shape, 0), GDN_CHUNK - 1)
        step = 1
        while step < GDN_CHUNK:
            g = g + jnp.where(row >= step, pltpu.roll(g, step, axis=0), 0.0)
            step *= 2
        total = g
        step = 1
        while step < GDN_CHUNK:
            total = jnp.where(row + step < GDN_CHUNK, pltpu.roll(total, n_rows - step, axis=0), total)
            step *= 2
    lane = lax.broadcasted_iota(jnp.int32, g.shape, 1)
    o_ref[...] = jnp.where(lane < heads, g, jnp.where(lane < 2 * heads, jax.nn.sigmoid(a), total))


def _gdn_gates(x, w_ab, alog_row, dtb_row, heads, tm, chunk_cumsum, name):
    m, k = x.shape
    return pl.pallas_call(
        functools.partial(_gates_kernel, heads=heads, chunk_cumsum=chunk_cumsum),
        grid=(m // tm,),
        in_specs=[pl.BlockSpec((tm, k), lambda i: (i, 0)),
                  pl.BlockSpec((k, LANES), lambda i: (0, 0)),
                  pl.BlockSpec((1, LANES), lambda i: (0, 0)),
                  pl.BlockSpec((1, LANES), lambda i: (0, 0))],
        out_specs=pl.BlockSpec((tm, LANES), lambda i: (i, 0)),
        out_shape=jax.ShapeDtypeStruct((m, LANES), F32),
        compiler_params=_cparams(1),
        name=name,
    )(x, w_ab, alog_row, dtb_row)


def _qkv_finish(conv, j, heads):
    y = _silu(conv)
    inv = lax.rsqrt(jnp.sum(y * y, axis=-1, keepdims=True) + NORM_EPS)
    scale = jnp.where(j < heads, inv * (GDN_HEAD_DIM ** -0.5),
                      jnp.where(j < 2 * heads, inv, 1.0))
    return y * scale


def _gdn_prep_sample_kernel(x_ref, buf_ref, w_ref, o_ref, *, heads):
    w = w_ref[...]
    width = w.shape[0]
    conv = w[width - 1:width, :] * x_ref[...]
    for s in range(width - 1):
        conv = conv + w[s:s + 1, :] * buf_ref[s]
    o_ref[...] = _qkv_finish(conv, pl.program_id(0), heads)


def _gdn_prep_sample(proj, buf_t, conv_w, heads):
    bsz = proj.shape[0]
    width, ch = conv_w.shape
    return pl.pallas_call(
        functools.partial(_gdn_prep_sample_kernel, heads=heads),
        grid=(ch // GDN_HEAD_DIM,),
        in_specs=[pl.BlockSpec((bsz, GDN_HEAD_DIM), lambda j: (0, j)),
                  pl.BlockSpec((width - 1, bsz, GDN_HEAD_DIM), lambda j: (0, 0, j)),
                  pl.BlockSpec((width, GDN_HEAD_DIM), lambda j: (0, j))],
        out_specs=pl.BlockSpec((bsz, GDN_HEAD_DIM), lambda j: (0, j)),
        out_shape=jax.ShapeDtypeStruct((bsz, ch), F32),
        compiler_params=_cparams(1),
        name="gdn_prep_sample",
    )(proj, buf_t, conv_w)


def _gated_rmsnorm(o, z, norm_w):
    on = o * lax.rsqrt(jnp.mean(o * o, axis=-1, keepdims=True) + NORM_EPS) * norm_w
    return on * _silu(z)


def _gdn_tile_kernel(q_ref, k_ref, v_ref, z_ref, gates_ref, grow_ref, cw_ref, nw_ref,
                     og_ref, sout_ref, s_ref, halo_ref, *, heads):
    c_len = GDN_CHUNK
    dh = GDN_HEAD_DIM
    tt = q_ref.shape[1]
    n_c = tt // c_len
    width = cw_ref.shape[0]
    tile = pl.program_id(1)

    @pl.when(tile == 0)
    def _():
        s_ref[...] = jnp.zeros_like(s_ref)
        halo_ref[...] = jnp.zeros_like(halo_ref)

    row = lax.broadcasted_iota(jnp.int32, (tt, tt), 0)
    col = lax.broadcasted_iota(jnp.int32, (tt, tt), 1)
    same_chunk = jnp.bitwise_and(row, -c_len) == jnp.bitwise_and(col, -c_len)
    tril = jnp.logical_and(same_chunk, row >= col)
    strict = jnp.logical_and(same_chunk, row > col)
    same_sub = jnp.bitwise_and(row, -GDN_SUB) == jnp.bitwise_and(col, -GDN_SUB)
    halo_row = lax.broadcasted_iota(jnp.int32, (SUBLANES, dh), 0)
    gate_tile = gates_ref[0]
    grow_tile = grow_ref[0, 0]
    norm_w = nw_ref[...]

    def conv_silu(x_ref, h, part):
        lanes = slice(h * dh, (h + 1) * dh)
        wl = slice((part * heads + h) * dh, (part * heads + h + 1) * dh)
        x = x_ref[0, :, lanes]
        halo = halo_ref[:, wl]
        w = cw_ref[:, wl]
        acc = w[width - 1:width, :] * x
        for s in range(1, width):
            xr = pltpu.roll(x, s, axis=0)
            top = jnp.where(halo_row < s, pltpu.roll(halo, s, axis=0), xr[:SUBLANES, :])
            acc = acc + w[width - 1 - s:width - s, :] * jnp.concatenate([top, xr[SUBLANES:, :]], axis=0)
        halo_ref[:, wl] = x[tt - SUBLANES:, :]
        return _silu(acc)

    def l2n(y, scale):
        return y * (lax.rsqrt(jnp.sum(y * y, axis=-1, keepdims=True) + NORM_EPS) * scale)

    for g0 in range(0, heads, GDN_GROUP):
        hs = range(g0, g0 + GDN_GROUP)
        q = [l2n(conv_silu(q_ref, h, 0), dh ** -0.5) for h in hs]
        k = [l2n(conv_silu(k_ref, h, 1), 1.0) for h in hs]
        v = [conv_silu(v_ref, h, 2) for h in hs]
        gcol = [gate_tile[:, h:h + 1] for h in hs]
        beta = [gate_tile[:, heads + h:heads + h + 1] for h in hs]
        glast = [gate_tile[:, 2 * heads + h:2 * heads + h + 1] for h in hs]
        decay = [jnp.exp(jnp.where(tril, gc - grow_tile[h:h + 1, :], NEG_BIG)) for h, gc in zip(hs, gcol)]
        kb = [ki * bi for ki, bi in zip(k, beta)]
        a_mat = [jnp.where(strict, _bdot_nt(kbi, ki) * di, 0.0) for kbi, ki, di in zip(kb, k, decay)]
        qk = [jnp.where(tril, _bdot_nt(qi, ki) * di, 0.0) for qi, ki, di in zip(q, k, decay)]

        p = [jnp.where(same_sub, ai, 0.0) for ai in a_mat]
        e_mat = [ai - pi for ai, pi in zip(a_mat, p)]
        x = [-pi for pi in p]
        span = 1
        while 2 * span < GDN_SUB:
            p = [_bdot(pi, pi) for pi in p]
            xp = [_bdot(xi, pi) for xi, pi in zip(x, p)]
            x = [xi + pi + xpi for xi, pi, xpi in zip(x, p, xp)]
            span *= 2
        rhs = [jnp.concatenate([vi * bi, kbi * jnp.exp(gc)], axis=-1)
               for vi, bi, kbi, gc in zip(v, beta, kb, gcol)]
        n_mat = [ei + _bdot(xi, ei) for ei, xi in zip(e_mat, x)]
        r = [ri + _bdot(xi, ri) for ri, xi in zip(rhs, x)]
        r = [ri - _bdot(ni, ri) for ri, ni in zip(r, n_mat)]
        span = 2
        while span < c_len // GDN_SUB:
            n_mat = [_bdot(ni, ni) for ni in n_mat]
            r = [ri + _bdot(ni, ri) for ri, ni in zip(r, n_mat)]
            span *= 2

        o_uw = [_bdot(qki, ri) for qki, ri in zip(qk, r)]
        q_eff = [qi * jnp.exp(gc) - oi[:, dh:] for qi, gc, oi in zip(q, gcol, o_uw)]
        k_dec = [ki * jnp.exp(gl - gc) for ki, gl, gc in zip(k, glast, gcol)]
        s = [s_ref[h] for h in hs]
        for c in range(n_c):
            rows = slice(c * c_len, (c + 1) * c_len)
            kr = [_bdot_tn(kd[rows, :], ri[rows, :]) for kd, ri in zip(k_dec, r)]
            o = [oi[rows, :dh] + _bdot(qe[rows, :], si) for oi, qe, si in zip(o_uw, q_eff, s)]
            for h, oi in zip(hs, o):
                z = z_ref[0, rows, h * dh:(h + 1) * dh]
                og_ref[0, rows, h * dh:(h + 1) * dh] = _gated_rmsnorm(oi, z, norm_w).astype(og_ref.dtype)
            a_c = [jnp.exp(gl[(c + 1) * c_len - 1:(c + 1) * c_len, :]) for gl in glast]
            s = [si * ai + kri[:, :dh] - _bdot(kri[:, dh:], si) for si, ai, kri in zip(s, a_c, kr)]
        for h, si in zip(hs, s):
            s_ref[h] = si

    @pl.when(tile == pl.num_programs(1) - 1)
    def _():
        sout_ref[0] = s_ref[...]


def _gdn_chunked(proj, gates, grow, conv_w, norm_w, heads):
    bsz, t, _ = proj.shape
    dh = GDN_HEAD_DIM
    vw = heads * dh
    width, ch = conv_w.shape
    tt = GDN_TT
    blk = lambda part: pl.BlockSpec((1, tt, vw), lambda b, i: (b, i, part))
    return pl.pallas_call(
        functools.partial(_gdn_tile_kernel, heads=heads),
        grid=(bsz, t // tt),
        in_specs=[blk(0), blk(1), blk(2), blk(3),
                  pl.BlockSpec((1, tt, LANES), lambda b, i: (b, i, 0)),
                  pl.BlockSpec((1, 1, heads, tt), lambda b, i: (b, i, 0, 0)),
                  pl.BlockSpec((width, ch), lambda b, i: (0, 0)),
                  pl.BlockSpec((1, dh), lambda b, i: (0, 0))],
        out_specs=[pl.BlockSpec((1, tt, vw), lambda b, i: (b, i, 0)),
                   pl.BlockSpec((1, heads, dh, dh), lambda b, i: (b, 0, 0, 0))],
        out_shape=[jax.ShapeDtypeStruct((bsz, t, vw), BF16),
                   jax.ShapeDtypeStruct((bsz, heads, dh, dh), F32)],
        scratch_shapes=[pltpu.VMEM((heads, dh, dh), F32), pltpu.VMEM((SUBLANES, ch), F32)],
        compiler_params=_cparams(2),
        name="gdn_chunked",
    )(proj, proj, proj, proj, gates, grow, conv_w, norm_w.reshape(1, dh))


def _gdn_step_kernel(qkv_ref, z_ref, gates_ref, s0_ref, nw_ref, og_ref, sout_ref, *, heads):
    dh = GDN_HEAD_DIM
    pad = 2 * SUBLANES
    norm_w = nw_ref[...]
    prow = lax.broadcasted_iota(jnp.int32, (pad, dh), 0)

    for bi in range(qkv_ref.shape[0]):
        r1 = slice(bi, bi + 1)
        gate_row = gates_ref[r1, :]
        for h in range(heads):
            q = jnp.broadcast_to(qkv_ref[r1, h * dh:(h + 1) * dh], (pad, dh))
            k = jnp.broadcast_to(qkv_ref[r1, (heads + h) * dh:(heads + h + 1) * dh], (pad, dh))
            v = jnp.broadcast_to(qkv_ref[r1, (2 * heads + h) * dh:(2 * heads + h + 1) * dh], (pad, dh))
            a = jnp.exp(gate_row[:, h:h + 1])
            beta = gate_row[:, heads + h:heads + h + 1]
            s = s0_ref[bi, h]
            delta = (v - a * _bdot(k, s)) * beta
            k_hi = k.astype(BF16).astype(F32)
            d_hi = delta.astype(BF16).astype(F32)
            k_parts = jnp.where(prow < 2, k_hi, jnp.where(prow == 2, k - k_hi, 0.0))
            d_parts = jnp.where(prow == 1, delta - d_hi, jnp.where(prow < 3, d_hi, 0.0))
            s_new = s * a + _bdot_tn(k_parts, d_parts)
            sout_ref[bi, h] = s_new
            o = _bdot(q, s_new)[0:1, :]
            og_ref[r1, h * dh:(h + 1) * dh] = _gated_rmsnorm(o, z_ref[r1, h * dh:(h + 1) * dh], norm_w)


def _gdn_step(qkvn, proj, gates, s0, norm_w, heads):
    bsz = qkvn.shape[0]
    dh = GDN_HEAD_DIM
    vw = heads * dh
    bb = DEC_BB
    return pl.pallas_call(
        functools.partial(_gdn_step_kernel, heads=heads),
        grid=(bsz // bb,),
        in_specs=[pl.BlockSpec((bb, 3 * vw), lambda i: (i, 0)),
                  pl.BlockSpec((bb, vw), lambda i: (i, 3)),
                  pl.BlockSpec((bb, LANES), lambda i: (i, 0)),
                  pl.BlockSpec((bb, heads, dh, dh), lambda i: (i, 0, 0, 0)),
                  pl.BlockSpec((1, dh), lambda i: (0, 0))],
        out_specs=[pl.BlockSpec((bb, vw), lambda i: (i, 0)),
                   pl.BlockSpec((bb, heads, dh, dh), lambda i: (i, 0, 0, 0))],
        out_shape=[jax.ShapeDtypeStruct((bsz, vw), F32),
                   jax.ShapeDtypeStruct((bsz, heads, dh, dh), F32)],
        compiler_params=_cparams(1),
        name="gdn_step",
    )(qkvn, proj, gates, s0, norm_w.reshape(1, dh))


def _sconv_prompt_kernel(b_ref, c_ref, h_ref, w_ref, o_ref, last_ref):
    ch = c_ref[0] * h_ref[0]
    w = w_ref[...]
    width = w.shape[0]
    t = ch.shape[0]
    row = lax.broadcasted_iota(jnp.int32, ch.shape, 0)
    conv = w[width - 1:width, :] * ch
    for s in range(1, width):
        conv = conv + w[width - 1 - s:width - s, :] * jnp.where(row >= s, pltpu.roll(ch, s, axis=0), 0.0)
    o_ref[0] = (b_ref[0] * conv).astype(o_ref.dtype)
    last_ref[0] = ch[t - (width - 1):, :]


def _sconv_prompt(proj, conv_w):
    bsz, t, _ = proj.shape
    width, d = conv_w.shape
    nb = d // LANES
    blk = lambda off: pl.BlockSpec((1, t, LANES), lambda b, j: (b, 0, j + off))
    return pl.pallas_call(
        _sconv_prompt_kernel,
        grid=(bsz, nb),
        in_specs=[blk(0), blk(nb), blk(2 * nb),
                  pl.BlockSpec((width, LANES), lambda b, j: (0, j))],
        out_specs=[pl.BlockSpec((1, t, LANES), lambda b, j: (b, 0, j)),
                   pl.BlockSpec((1, width - 1, LANES), lambda b, j: (b, 0, j))],
        out_shape=[jax.ShapeDtypeStruct((bsz, t, d), BF16),
                   jax.ShapeDtypeStruct((bsz, width - 1, d), F32)],
        compiler_params=_cparams(2),
        name="sconv_prompt",
    )(proj, proj, proj, conv_w)


def _sconv_sample_kernel(b_ref, c_ref, h_ref, buf_ref, w_ref, o_ref, ch_ref):
    ch = c_ref[...] * h_ref[...]
    w = w_ref[...]
    width = w.shape[0]
    conv = w[width - 1:width, :] * ch
    for s in range(width - 1):
        conv = conv + w[s:s + 1, :] * buf_ref[s]
    o_ref[...] = b_ref[...] * conv
    ch_ref[...] = ch


def _sconv_sample(proj, buf_t, conv_w):
    bsz = proj.shape[0]
    width, d = conv_w.shape
    nb = d // LANES
    blk = lambda off: pl.BlockSpec((bsz, LANES), lambda j: (0, j + off))
    return pl.pallas_call(
        _sconv_sample_kernel,
        grid=(nb,),
        in_specs=[blk(0), blk(nb), blk(2 * nb),
                  pl.BlockSpec((width - 1, bsz, LANES), lambda j: (0, 0, j)),
                  pl.BlockSpec((width, LANES), lambda j: (0, j))],
        out_specs=[pl.BlockSpec((bsz, LANES), lambda j: (0, j)),
                   pl.BlockSpec((bsz, LANES), lambda j: (0, j))],
        out_shape=[jax.ShapeDtypeStruct((bsz, d), F32),
                   jax.ShapeDtypeStruct((bsz, d), F32)],
        compiler_params=_cparams(1),
        name="sconv_sample",
    )(proj, proj, proj, buf_t, conv_w)


def _router_kernel(x_ref, w_ref, o_ref, *, n_experts):
    logits = jnp.dot(x_ref[...], w_ref[...], preferred_element_type=F32,
                     precision=lax.Precision.HIGHEST)
    lane = lax.broadcasted_iota(jnp.int32, logits.shape, 1).astype(F32)
    lg = jnp.where(lane < n_experts, logits, NEG_BIG)
    m1 = jnp.max(lg, axis=-1, keepdims=True)
    i1 = jnp.min(jnp.where(lg == m1, lane, float(LANES)), axis=-1, keepdims=True)
    lg2 = jnp.where(lane == i1, NEG_BIG, lg)
    m2 = jnp.max(lg2, axis=-1, keepdims=True)
    i2 = jnp.min(jnp.where(lg2 == m2, lane, float(LANES)), axis=-1, keepdims=True)
    e = jnp.exp(m2 - m1)
    g1 = 1.0 / (1.0 + e)
    g2 = e / (1.0 + e)
    o_ref[...] = jnp.where(lane == 0, i1, jnp.where(lane == 1, i2,
                           jnp.where(lane == 2, g1, jnp.where(lane == 3, g2, 0.0))))


def _router(x, w_pad, n_experts):
    m, k = x.shape
    tm = TM_TOKEN
    return pl.pallas_call(
        functools.partial(_router_kernel, n_experts=n_experts),
        grid=(m // tm,),
        in_specs=[pl.BlockSpec((tm, k), lambda i: (i, 0)),
                  pl.BlockSpec((k, LANES), lambda i: (0, 0))],
        out_specs=pl.BlockSpec((tm, LANES), lambda i: (i, 0)),
        out_shape=jax.ShapeDtypeStruct((m, LANES), F32),
        compiler_params=_cparams(1),
        name="moe_router",
    )(x, w_pad)


def _row_copy(src_hbm, src_row, dst_ref, dst_row, sem):
    return pltpu.make_async_copy(src_hbm.at[pl.ds(src_row, 1)], dst_ref.at[pl.ds(dst_row, 1)], sem)


def _wait_rows(src_hbm, dst_ref, sem):
    pltpu.make_async_copy(src_hbm.at[pl.ds(0, dst_ref.shape[0])], dst_ref, sem).wait()


def _gather_kernel(idx_ref, nu_ref, src_hbm, o_ref, buf, sem):
    i = pl.program_id(0)
    tg = o_ref.shape[0]
    base = i * tg

    @pl.when(i < nu_ref[0])
    def _():
        def issue(r, c):
            _row_copy(src_hbm, idx_ref[base + r], buf, r, sem).start()
            return c
        lax.fori_loop(0, tg, issue, 0, unroll=8)
        _wait_rows(src_hbm, buf, sem)
        o_ref[...] = buf[...].astype(o_ref.dtype)

    @pl.when(i >= nu_ref[0])
    def _():
        o_ref[...] = jnp.zeros_like(o_ref)


def _gather_rows(src, row_token, n_used, tg):
    rows = row_token.shape[0]
    d = src.shape[1]
    return pl.pallas_call(
        _gather_kernel,
        grid_spec=pltpu.PrefetchScalarGridSpec(
            num_scalar_prefetch=2,
            grid=(rows // tg,),
            in_specs=[pl.BlockSpec(memory_space=pl.ANY)],
            out_specs=pl.BlockSpec((tg, d), lambda i, idx, nu: (i, 0)),
            scratch_shapes=[pltpu.VMEM((tg, d), src.dtype), pltpu.SemaphoreType.DMA(())]),
        out_shape=jax.ShapeDtypeStruct((rows, d), BF16),
        compiler_params=_cparams(1),
        name="moe_gather",
    )(row_token, n_used, src)


def _moe_ffn1_kernel(te_ref, nu_ref, x_ref, wg_ref, wu_ref, o_ref, wg_bf, wu_bf):
    i = pl.program_id(1)
    new_expert = jnp.logical_or(i == 0, te_ref[i] != te_ref[jnp.maximum(i - 1, 0)])

    @pl.when(new_expert)
    def _():
        wg_bf[...] = wg_ref[0].astype(BF16)
        wu_bf[...] = wu_ref[0].astype(BF16)

    @pl.when(i < nu_ref[0])
    def _():
        xb = x_ref[...].astype(BF16)
        g = jnp.dot(xb, wg_bf[...], preferred_element_type=F32)
        u = jnp.dot(xb, wu_bf[...], preferred_element_type=F32)
        o_ref[...] = (_silu(g) * u).astype(o_ref.dtype)

    @pl.when(i >= nu_ref[0])
    def _():
        o_ref[...] = jnp.zeros_like(o_ref)


def _moe_ffn1(xs, w_gu, tile_expert, n_used, tm, tn):
    rows, k = xs.shape
    f = w_gu.shape[2] // 2
    nj = f // tn
    return pl.pallas_call(
        _moe_ffn1_kernel,
        grid_spec=pltpu.PrefetchScalarGridSpec(
            num_scalar_prefetch=2,
            grid=(nj, rows // tm),
            in_specs=[pl.BlockSpec((tm, k), lambda j, i, te, nu: (i, 0)),
                      pl.BlockSpec((1, k, tn), lambda j, i, te, nu: (te[i], 0, j)),
                      pl.BlockSpec((1, k, tn), lambda j, i, te, nu: (te[i], 0, j + nj))],
            out_specs=pl.BlockSpec((tm, tn), lambda j, i, te, nu: (i, j)),
            scratch_shapes=[pltpu.VMEM((k, tn), BF16), pltpu.VMEM((k, tn), BF16)]),
        out_shape=jax.ShapeDtypeStruct((rows, f), BF16),
        compiler_params=_cparams(2),
        name="moe_ffn1",
    )(tile_expert, n_used, xs, w_gu, w_gu)


def _moe_ffn2_kernel(te_ref, nu_ref, x_ref, w_ref, o_ref, w_bf):
    i = pl.program_id(1)
    new_expert = jnp.logical_or(i == 0, te_ref[i] != te_ref[jnp.maximum(i - 1, 0)])

    @pl.when(new_expert)
    def _():
        w_bf[...] = w_ref[0].astype(BF16)

    @pl.when(i < nu_ref[0])
    def _():
        o_ref[...] = jnp.dot(x_ref[...], w_bf[...], preferred_element_type=F32)

    @pl.when(i >= nu_ref[0])
    def _():
        o_ref[...] = jnp.zeros_like(o_ref)


def _moe_ffn2(act, w_down, tile_expert, n_used, tm, tn):
    rows, k = act.shape
    d = w_down.shape[2]
    return pl.pallas_call(
        _moe_ffn2_kernel,
        grid_spec=pltpu.PrefetchScalarGridSpec(
            num_scalar_prefetch=2,
            grid=(d // tn, rows // tm),
            in_specs=[pl.BlockSpec((tm, k), lambda j, i, te, nu: (i, 0)),
                      pl.BlockSpec((1, k, tn), lambda j, i, te, nu: (te[i], 0, j))],
            out_specs=pl.BlockSpec((tm, tn), lambda j, i, te, nu: (i, j)),
            scratch_shapes=[pltpu.VMEM((k, tn), BF16)]),
        out_shape=jax.ShapeDtypeStruct((rows, d), F32),
        compiler_params=_cparams(2),
        name="moe_ffn2",
    )(tile_expert, n_used, act, w_down)


def _combine_kernel(pos_ref, ys_hbm, route_ref, res_ref, g_ref, b_ref, o_ref, buf_a, buf_b, sems):
    tm = o_ref.shape[0]
    base = pl.program_id(0) * tm

    def issue(r, c):
        t2 = 2 * (base + r)
        _row_copy(ys_hbm, pos_ref[t2], buf_a, r, sems.at[0]).start()
        _row_copy(ys_hbm, pos_ref[t2 + 1], buf_b, r, sems.at[1]).start()
        return c
    lax.fori_loop(0, tm, issue, 0, unroll=8)
    _wait_rows(ys_hbm, buf_a, sems.at[0])
    _wait_rows(ys_hbm, buf_b, sems.at[1])

    route = route_ref[...]
    y = route[:, 2:3] * buf_a[...] + route[:, 3:4] * buf_b[...]
    o_ref[...] = _layer_norm(ALPHA * res_ref[...] + y, g_ref[...], b_ref[...])


def _moe_combine(ys, pos, route, res, ln_g, ln_b):
    m, d = res.shape
    tm = TM_TOKEN
    return pl.pallas_call(
        _combine_kernel,
        grid_spec=pltpu.PrefetchScalarGridSpec(
            num_scalar_prefetch=1,
            grid=(m // tm,),
            in_specs=[pl.BlockSpec(memory_space=pl.ANY),
                      pl.BlockSpec((tm, LANES), lambda i, p: (i, 0)),
                      pl.BlockSpec((tm, d), lambda i, p: (i, 0)),
                      pl.BlockSpec((1, d), lambda i, p: (0, 0)),
                      pl.BlockSpec((1, d), lambda i, p: (0, 0))],
            out_specs=pl.BlockSpec((tm, d), lambda i, p: (i, 0)),
            scratch_shapes=[pltpu.VMEM((tm, d), F32), pltpu.VMEM((tm, d), F32),
                            pltpu.SemaphoreType.DMA((2,))]),
        out_shape=jax.ShapeDtypeStruct((m, d), F32),
        compiler_params=_cparams(1),
        name="moe_combine",
    )(pos, ys, route, res, ln_g.reshape(1, d), ln_b.reshape(1, d))


def _dispatch_plan(route, n_experts, tm):
    n_tok = route.shape[0]
    ids = route[:, :TOP_K].astype(jnp.int32).reshape(-1)
    n_pairs = ids.shape[0]
    n_tiles = n_pairs // tm + n_experts
    onehot = (ids[:, None] == jnp.arange(n_experts, dtype=jnp.int32)[None, :]).astype(jnp.int32)
    rank = jnp.sum((jnp.cumsum(onehot, axis=0) - onehot) * onehot, axis=1)
    counts = jnp.sum(onehot, axis=0)
    tiles_e = (counts + tm - 1) // tm
    tile_end = jnp.cumsum(tiles_e)
    group_off = (tile_end - tiles_e) * tm
    pos = group_off[ids] + rank
    row_token = jnp.zeros((n_tiles * tm,), jnp.int32).at[pos].set(
        jnp.arange(n_pairs, dtype=jnp.int32) // TOP_K)
    n_used = tile_end[n_experts - 1:]
    tile_ids = jnp.arange(n_tiles, dtype=jnp.int32)
    tile_expert = jnp.sum((tile_ids[:, None] >= tile_end[None, :]).astype(jnp.int32), axis=1)
    last_expert = jnp.sum((n_used - 1 >= tile_end).astype(jnp.int32))
    tile_expert = jnp.minimum(tile_expert, last_expert).astype(jnp.int32)
    return row_token, pos.astype(jnp.int32), tile_expert, n_used.astype(jnp.int32)


def _moe_layer(h_all, w_router, w_gu, w_down, ln_g, ln_b):
    n_experts = w_router.shape[1]
    w_pad = jnp.pad(w_router, ((0, 0), (0, LANES - n_experts)))
    route = _router(h_all, w_pad, n_experts)
    row_token, pos, tile_expert, n_used = _dispatch_plan(route, n_experts, TM_EXPERT)
    xs = _gather_rows(h_all, row_token, n_used, TM_EXPERT)
    act = _moe_ffn1(xs, w_gu, tile_expert, n_used, TM_EXPERT, 896)
    ys = _moe_ffn2(act, w_down, tile_expert, n_used, TM_EXPERT, 512)
    return _moe_combine(ys, pos, route, h_all, ln_g, ln_b)


def kernel(x_prompt, x_sample, state_gdn_S, state_gdn_conv, state_sconv, ln_g, ln_b, gdn_w_in, gdn_conv_w, gdn_a_log, gdn_dt_bias, gdn_norm_w, gdn_w_out, sc_w_in, sc_conv_w, sc_w_out, ffn_w_gu, ffn_w_down, moe_w_router, moe_w_gu, moe_w_down):
    bsz, t, d = x_prompt.shape
    dec = x_sample.shape[0]
    heads = gdn_a_log.shape[1]
    dh = GDN_HEAD_DIM
    vw = heads * dh
    conv_ch = gdn_conv_w.shape[2]
    assert x_sample.shape[1] == 1 and conv_ch == 3 * vw and gdn_w_in.shape[2] == conv_ch + vw + 2 * heads
    assert t % GDN_TT == 0 and (bsz * t) % TM_PROMPT == 0 and (bsz * t + dec) % TM_TOKEN == 0
    n_p = bsz * t

    xp = x_prompt.reshape(n_p, d)
    xs = x_sample.reshape(dec, d)

    w_in = gdn_w_in[0]
    w_a = w_in[:, conv_ch + vw:conv_ch + vw + heads]
    w_b = w_in[:, conv_ch + vw + heads:]
    w_ab = jnp.pad(jnp.concatenate([w_a, w_b, w_a], axis=1), ((0, 0), (0, LANES - 3 * heads)))
    lane_pad = lambda p: jnp.pad(jnp.concatenate([p, p, p]), (0, LANES - 3 * heads)).reshape(1, LANES)
    alog_row = lane_pad(gdn_a_log[0])
    dtb_row = lane_pad(gdn_dt_bias[0])

    proj_p = _matmul(xp, w_in, conv_ch + vw, TM_PROMPT, 1024, F32, "gdn_in_prompt")
    proj_s = _matmul(xs, w_in, conv_ch + vw, dec, 1024, F32, "gdn_in_sample")
    gates_p = _gdn_gates(xp, w_ab, alog_row, dtb_row, heads, TM_PROMPT, True, "gdn_gates_prompt")
    gates_s = _gdn_gates(xs, w_ab, alog_row, dtb_row, heads, dec, False, "gdn_gates_sample")

    proj_p3 = proj_p.reshape(bsz, t, conv_ch + vw)
    conv_buf_t = jnp.transpose(state_gdn_conv[0], (1, 0, 2))
    qkvn_s = _gdn_prep_sample(proj_s, conv_buf_t, gdn_conv_w[0], heads)

    grow = gates_p[:, :heads].reshape(bsz, t // GDN_TT, GDN_TT, heads).transpose(0, 1, 3, 2)
    og_p, s_prompt = _gdn_chunked(proj_p3, gates_p.reshape(bsz, t, LANES), grow,
                                  gdn_conv_w[0], gdn_norm_w[0], heads)
    og_s, s_sample = _gdn_step(qkvn_s, proj_s, gates_s, state_gdn_S[0], gdn_norm_w[0], heads)

    conv_prompt = proj_p3[:, t - (gdn_conv_w.shape[1] - 1):, :conv_ch]
    conv_sample = jnp.concatenate([state_gdn_conv[0][:, 1:], proj_s[:, None, :conv_ch]], axis=1)

    h_p = _matmul_ln(og_p.reshape(n_p, vw), gdn_w_out[0], xp, ln_g[0, 0], ln_b[0, 0], TM_PROMPT, "gdn_out_prompt")
    h_s = _matmul_ln(og_s, gdn_w_out[0], xs, ln_g[0, 0], ln_b[0, 0], dec, "gdn_out_sample")

    d_ff = ffn_w_down.shape[1]
    act_p = _matmul_swiglu(h_p, ffn_w_gu[0], TM_PROMPT, d_ff // 2, "ffn_up_prompt")
    act_s = _matmul_swiglu(h_s, ffn_w_gu[0], dec, d_ff // 2, "ffn_up_sample")
    h_p = _matmul_ln(act_p, ffn_w_down[0], h_p, ln_g[0, 1], ln_b[0, 1], TM_DOWN, "ffn_down_prompt")
    h_s = _matmul_ln(act_s, ffn_w_down[0], h_s, ln_g[0, 1], ln_b[0, 1], dec, "ffn_down_sample")

    sproj_p = _matmul(h_p, sc_w_in[0], 3 * d, TM_PROMPT, 1024, F32, "sc_in_prompt")
    sproj_s = _matmul(h_s, sc_w_in[0], 3 * d, dec, 1024, F32, "sc_in_sample")
    sg_p, sconv_prompt = _sconv_prompt(sproj_p.reshape(bsz, t, 3 * d), sc_conv_w[0])
    sbuf_t = jnp.transpose(state_sconv[0], (1, 0, 2))
    sg_s, ch_s = _sconv_sample(sproj_s, sbuf_t, sc_conv_w[0])
    sconv_sample = jnp.concatenate([state_sconv[0][:, 1:], ch_s[:, None, :]], axis=1)

    h_p = _matmul_ln(sg_p.reshape(n_p, d), sc_w_out[0], h_p, ln_g[1, 0], ln_b[1, 0], TM_PROMPT, "sc_out_prompt")
    h_s = _matmul_ln(sg_s, sc_w_out[0], h_s, ln_g[1, 0], ln_b[1, 0], dec, "sc_out_sample")

    h_all = jnp.concatenate([h_p, h_s], axis=0)
    y_all = _moe_layer(h_all, moe_w_router[0], moe_w_gu[0], moe_w_down[0], ln_g[1, 1], ln_b[1, 1])

    y_prompt = y_all[:n_p].reshape(bsz, t, d)
    y_sample = y_all[n_p:].reshape(dec, 1, d)
    return (y_prompt, y_sample, s_prompt[None], s_sample[None], conv_prompt[None],
            conv_sample[None], sconv_prompt[None], sconv_sample[None])
```

```python
import functools

import jax
import jax.numpy as jnp
from jax import lax
from jax.experimental import pallas as pl
from jax.experimental.pallas import tpu as pltpu

F32 = jnp.float32
BF16 = jnp.bfloat16

DEPTH = 2
ALPHA = (2.0 * DEPTH) ** 0.25
LN_EPS = 1e-5
NORM_EPS = 1e-6
GDN_HEAD_DIM = 128
GDN_CHUNK = 64
GDN_SUB = 16
TOP_K = 2

LANES = 128
SUBLANES = 8
VMEM_LIMIT = 56 * 1024 * 1024
NEG_BIG = -1e30

TM_PROMPT = 1024
TM_DOWN = 512
GDN_TT = 256
GDN_GROUP = 4
TM_EXPERT = 512
TN_EXPERT_UP = 1792
TN_EXPERT_DOWN = 1024
TM_TOKEN = 384
TM_COMBINE = 128
DEC_BB = 8
MM_SLAB = 256


def _cparams(n_axes, vmem=VMEM_LIMIT):
    return pltpu.CompilerParams(
        dimension_semantics=("arbitrary",) * n_axes, vmem_limit_bytes=vmem)


def _bdot(a, b):
    return jnp.dot(a.astype(BF16), b.astype(BF16), preferred_element_type=F32)


def _bdot_nt(a, b):
    return lax.dot_general(a.astype(BF16), b.astype(BF16),
                           (((1,), (1,)), ((), ())), preferred_element_type=F32)


def _bdot_tn(a, b):
    return lax.dot_general(a.astype(BF16), b.astype(BF16),
                           (((0,), (0,)), ((), ())), preferred_element_type=F32)


def _silu(x):
    return x * (0.5 * jnp.tanh(0.5 * x) + 0.5)


def _swiglu_slabs(x_ref, wg_bf, wu_bf, o_ref):
    slab = min(MM_SLAB, x_ref.shape[0])
    for r0 in range(0, x_ref.shape[0], slab):
        xb = x_ref[r0:r0 + slab, :].astype(BF16)
        g = jnp.dot(xb, wg_bf[...], preferred_element_type=F32)
        u = jnp.dot(xb, wu_bf[...], preferred_element_type=F32)
        o_ref[r0:r0 + slab, :] = (_silu(g) * u).astype(o_ref.dtype)


def _layer_norm(r, g, b):
    mu = jnp.mean(r, axis=-1, keepdims=True)
    d = r - mu
    var = jnp.mean(d * d, axis=-1, keepdims=True)
    return d * lax.rsqrt(var + LN_EPS) * g + b


def _mm_kernel(x_ref, w_ref, o_ref, wbf_ref):
    @pl.when(pl.program_id(1) == 0)
    def _():
        wbf_ref[...] = w_ref[...].astype(BF16)

    o_ref[...] = jnp.dot(x_ref[...].astype(BF16), wbf_ref[...],
                         preferred_element_type=F32).astype(o_ref.dtype)


def _matmul(x, w, n_cols, tm, tn, out_dtype, name):
    m, k = x.shape
    return pl.pallas_call(
        _mm_kernel,
        grid=(n_cols // tn, m // tm),
        in_specs=[pl.BlockSpec((tm, k), lambda j, i: (i, 0)),
                  pl.BlockSpec((k, tn), lambda j, i: (0, j))],
        out_specs=pl.BlockSpec((tm, tn), lambda j, i: (i, j)),
        out_shape=jax.ShapeDtypeStruct((m, n_cols), out_dtype),
        scratch_shapes=[pltpu.VMEM((k, tn), BF16)],
        compiler_params=_cparams(2),
        name=name,
    )(x, w)


def _mm_swiglu_kernel(x_ref, wg_ref, wu_ref, o_ref, wg_bf, wu_bf):
    @pl.when(pl.program_id(1) == 0)
    def _():
        wg_bf[...] = wg_ref[...].astype(BF16)
        wu_bf[...] = wu_ref[...].astype(BF16)

    _swiglu_slabs(x_ref, wg_bf, wu_bf, o_ref)


def _matmul_swiglu(x, w_gu, tm, tn, name):
    m, k = x.shape
    f = w_gu.shape[1] // 2
    nj = f // tn
    return pl.pallas_call(
        _mm_swiglu_kernel,
        grid=(nj, m // tm),
        in_specs=[pl.BlockSpec((tm, k), lambda j, i: (i, 0)),
                  pl.BlockSpec((k, tn), lambda j, i: (0, j)),
                  pl.BlockSpec((k, tn), lambda j, i: (0, j + nj))],
        out_specs=pl.BlockSpec((tm, tn), lambda j, i: (i, j)),
        out_shape=jax.ShapeDtypeStruct((m, f), BF16),
        scratch_shapes=[pltpu.VMEM((k, tn), BF16), pltpu.VMEM((k, tn), BF16)],
        compiler_params=_cparams(2),
        name=name,
    )(x, w_gu, w_gu)


def _mm_ln_kernel(x_ref, w_ref, res_ref, g_ref, b_ref, *rest, n_out, has_tail):
    tail_refs = rest[:2] if has_tail else ()
    out_refs = rest[len(tail_refs):len(tail_refs) + n_out]
    wbf_ref = rest[-1]
    i = pl.program_id(0)

    @pl.when(i == 0)
    def _():
        wbf_ref[...] = w_ref[...].astype(BF16)

    def emit(xr, rr, rows):
        y = jnp.dot(xr[...].astype(BF16), wbf_ref[...], preferred_element_type=F32)
        h = _layer_norm(ALPHA * rr[...] + y, g_ref[...], b_ref[...])
        for o_ref in out_refs:
            o_ref[rows, :] = h.astype(o_ref.dtype)

    if has_tail:
        n_main = pl.num_programs(0) - 1

        @pl.when(i < n_main)
        def _():
            emit(x_ref, res_ref, slice(None))

        @pl.when(i == n_main)
        def _():
            emit(tail_refs[0], tail_refs[1], slice(0, tail_refs[0].shape[0]))
    else:
        emit(x_ref, res_ref, slice(None))


def _matmul_ln(x, w, res, ln_g, ln_b, tm, name, also_bf16=False, tail=None):
    m, k = x.shape
    d = w.shape[1]
    n_main = m // tm
    main_idx = lambda i: (jnp.minimum(i, n_main - 1), 0)
    operands = [x, w, res, ln_g.reshape(1, d), ln_b.reshape(1, d)]
    in_specs = [pl.BlockSpec((tm, k), main_idx),
                pl.BlockSpec((k, d), lambda i: (0, 0)),
                pl.BlockSpec((tm, d), main_idx),
                pl.BlockSpec((1, d), lambda i: (0, 0)),
                pl.BlockSpec((1, d), lambda i: (0, 0))]
    m_out = m
    if tail is not None:
        m_tail = tail[0].shape[0]
        assert m_tail < tm and m_tail % (2 * SUBLANES) == 0
        operands += list(tail)
        in_specs += [pl.BlockSpec((m_tail, k), lambda i: (0, 0)), pl.BlockSpec((m_tail, d), lambda i: (0, 0))]
        m_out = m + m_tail
    out_shape = [jax.ShapeDtypeStruct((m_out, d), F32)]
    if also_bf16:
        out_shape.append(jax.ShapeDtypeStruct((m_out, d), BF16))
    out = pl.pallas_call(
        functools.partial(_mm_ln_kernel, n_out=len(out_shape), has_tail=tail is not None),
        grid=(n_main + int(tail is not None),),
        in_specs=in_specs,
        out_specs=[pl.BlockSpec((tm, d), lambda i: (i, 0)) for _ in out_shape],
        out_shape=out_shape,
        scratch_shapes=[pltpu.VMEM((k, d), BF16)],
        compiler_params=_cparams(1),
        name=name,
    )(*operands)
    return out if also_bf16 else out[0]


def _gates_kernel(x_ref, w_ref, alog_ref, dtb_ref, o_ref, *, heads, chunk_cumsum):
    a = _bdot(x_ref[...], w_ref[...])
    z = a + dtb_ref[...]
    softplus = jnp.maximum(z, 0.0) + jnp.log1p(jnp.exp(-jnp.abs(z)))
    g = -jnp.exp(alog_ref[...]) * softplus
    total = g
    if chunk_cumsum:
        n_rows = g.shape[0]
        row = jnp.bitwise_and(lax.broadcasted_iota(jnp.int32, g.shape, 0), GDN_CHUNK - 1)
        step = 1
        while step < GDN_CHUNK:
            g = g + jnp.where(row >= step, pltpu.roll(g, step, axis=0), 0.0)
            step *= 2
        total = g
        step = 1
        while step < GDN_CHUNK:
            total = jnp.where(row + step < GDN_CHUNK, pltpu.roll(total, n_rows - step, axis=0), total)
            step *= 2
    lane = lax.broadcasted_iota(jnp.int32, g.shape, 1)
    o_ref[...] = jnp.where(lane < heads, g, jnp.where(lane < 2 * heads, jax.nn.sigmoid(a), total))


def _gdn_gates(x, w_ab, alog_row, dtb_row, heads, tm, chunk_cumsum, name):
    m, k = x.shape
    return pl.pallas_call(
        functools.partial(_gates_kernel, heads=heads, chunk_cumsum=chunk_cumsum),
        grid=(m // tm,),
        in_specs=[pl.BlockSpec((tm, k), lambda i: (i, 0)),
                  pl.BlockSpec((k, LANES), lambda i: (0, 0)),
                  pl.BlockSpec((1, LANES), lambda i: (0, 0)),
                  pl.BlockSpec((1, LANES), lambda i: (0, 0))],
        out_specs=pl.BlockSpec((tm, LANES), lambda i: (i, 0)),
        out_shape=jax.ShapeDtypeStruct((m, LANES), F32),
        compiler_params=_cparams(1),
        name=name,
    )(x, w_ab, alog_row, dtb_row)


def _qkv_finish(conv, j, heads):
    y = _silu(conv)
    inv = lax.rsqrt(jnp.sum(y * y, axis=-1, keepdims=True) + NORM_EPS)
    scale = jnp.where(j < heads, inv * (GDN_HEAD_DIM ** -0.5),
                      jnp.where(j < 2 * heads, inv, 1.0))
    return y * scale


def _gdn_prep_sample_kernel(x_ref, buf_ref, w_ref, o_ref, *, heads):
    w = w_ref[...]
    width = w.shape[0]
    conv = w[width - 1:width, :] * x_ref[...]
    for s in range(width - 1):
        conv = conv + w[s:s + 1, :] * buf_ref[s]
    o_ref[...] = _qkv_finish(conv, pl.program_id(0), heads)


def _gdn_prep_sample(proj, buf_t, conv_w, heads):
    bsz = proj.shape[0]
    width, ch = conv_w.shape
    return pl.pallas_call(
        functools.partial(_gdn_prep_sample_kernel, heads=heads),
        grid=(ch // GDN_HEAD_DIM,),
        in_specs=[pl.BlockSpec((bsz, GDN_HEAD_DIM), lambda j: (0, j)),
                  pl.BlockSpec((width - 1, bsz, GDN_HEAD_DIM), lambda j: (0, 0, j)),
                  pl.BlockSpec((width, GDN_HEAD_DIM), lambda j: (0, j))],
        out_specs=pl.BlockSpec((bsz, GDN_HEAD_DIM), lambda j: (0, j)),
        out_shape=jax.ShapeDtypeStruct((bsz, ch), F32),
        compiler_params=_cparams(1),
        name="gdn_prep_sample",
    )(proj, buf_t, conv_w)


def _gated_rmsnorm(o, z, norm_w):
    on = o * lax.rsqrt(jnp.mean(o * o, axis=-1, keepdims=True) + NORM_EPS) * norm_w
    return on * _silu(z)


def _gdn_tile_kernel(q_ref, k_ref, v_ref, z_ref, gates_ref, grow_ref, cw_ref, nw_ref,
                     og_ref, sout_ref, s_ref, halo_ref, *, heads):
    c_len = GDN_CHUNK
    dh = GDN_HEAD_DIM
    tt = q_ref.shape[1]
    n_c = tt // c_len
    width = cw_ref.shape[0]
    tile = pl.program_id(1)

    @pl.when(tile == 0)
    def _():
        s_ref[...] = jnp.zeros_like(s_ref)
        halo_ref[...] = jnp.zeros_like(halo_ref)

    row = lax.broadcasted_iota(jnp.int32, (tt, tt), 0)
    col = lax.broadcasted_iota(jnp.int32, (tt, tt), 1)
    same_chunk = jnp.bitwise_and(row, -c_len) == jnp.bitwise_and(col, -c_len)
    tril = jnp.logical_and(same_chunk, row >= col)
    strict = jnp.logical_and(same_chunk, row > col)
    same_sub = jnp.bitwise_and(row, -GDN_SUB) == jnp.bitwise_and(col, -GDN_SUB)
    halo_row = lax.broadcasted_iota(jnp.int32, (SUBLANES, dh), 0)
    gate_tile = gates_ref[0]
    grow_tile = grow_ref[0, 0]
    norm_w = nw_ref[...]

    def conv_silu(x_ref, h, part):
        lanes = slice(h * dh, (h + 1) * dh)
        wl = slice((part * heads + h) * dh, (part * heads + h + 1) * dh)
        x = x_ref[0, :, lanes]
        halo = halo_ref[:, wl]
        w = cw_ref[:, wl]
        acc = w[width - 1:width, :] * x
        for s in range(1, width):
            xr = pltpu.roll(x, s, axis=0)
            top = jnp.where(halo_row < s, pltpu.roll(halo, s, axis=0), xr[:SUBLANES, :])
            acc = acc + w[width - 1 - s:width - s, :] * jnp.concatenate([top, xr[SUBLANES:, :]], axis=0)
        halo_ref[:, wl] = x[tt - SUBLANES:, :]
        return _silu(acc)

    def l2n(y, scale):
        return y * (lax.rsqrt(jnp.sum(y * y, axis=-1, keepdims=True) + NORM_EPS) * scale)

    for g0 in range(0, heads, GDN_GROUP):
        hs = range(g0, g0 + GDN_GROUP)
        q = [l2n(conv_silu(q_ref, h, 0), dh ** -0.5) for h in hs]
        k = [l2n(conv_silu(k_ref, h, 1), 1.0) for h in hs]
        v = [conv_silu(v_ref, h, 2) for h in hs]
        gcol = [gate_tile[:, h:h + 1] for h in hs]
        beta = [gate_tile[:, heads + h:heads + h + 1] for h in hs]
        glast = [gate_tile[:, 2 * heads + h:2 * heads + h + 1] for h in hs]
        decay = [jnp.exp(jnp.where(tril, gc - grow_tile[h:h + 1, :], NEG_BIG)) for h, gc in zip(hs, gcol)]
        kb = [ki * bi for ki, bi in zip(k, beta)]
        a_mat = [jnp.where(strict, _bdot_nt(kbi, ki) * di, 0.0) for kbi, ki, di in zip(kb, k, decay)]
        qk = [jnp.where(tril, _bdot_nt(qi, ki) * di, 0.0) for qi, ki, di in zip(q, k, decay)]

        p = [jnp.where(same_sub, ai, 0.0) for ai in a_mat]
        e_mat = [ai - pi for ai, pi in zip(a_mat, p)]
        x = [-pi for pi in p]
        span = 1
        while 2 * span < GDN_SUB:
            p = [_bdot(pi, pi) for pi in p]
            xp = [_bdot(xi, pi) for xi, pi in zip(x, p)]
            x = [xi + pi + xpi for xi, pi, xpi in zip(x, p, xp)]
            span *= 2
        rhs = [jnp.concatenate([vi * bi, kbi * jnp.exp(gc)], axis=-1)
               for vi, bi, kbi, gc in zip(v, beta, kb, gcol)]
        n_mat = [ei + _bdot(xi, ei) for ei, xi in zip(e_mat, x)]
        r = [ri + _bdot(xi, ri) for ri, xi in zip(rhs, x)]
        r = [ri - _bdot(ni, ri) for ri, ni in zip(r, n_mat)]
        span = 2
        while span < c_len // GDN_SUB:
            n_mat = [_bdot(ni, ni) for ni in n_mat]
            r = [ri + _bdot(ni, ri) for ri, ni in zip(r, n_mat)]
            span *= 2

        o_uw = [_bdot(qki, ri) for qki, ri in zip(qk, r)]
        q_eff = [qi * jnp.exp(gc) - oi[:, dh:] for qi, gc, oi in zip(q, gcol, o_uw)]
        k_dec = [ki * jnp.exp(gl - gc) for ki, gl, gc in zip(k, glast, gcol)]
        s = [s_ref[h] for h in hs]
        for c in range(n_c):
            rows = slice(c * c_len, (c + 1) * c_len)
            kr = [_bdot_tn(kd[rows, :], ri[rows, :]) for kd, ri in zip(k_dec, r)]
            o = [oi[rows, :dh] + _bdot(qe[rows, :], si) for oi, qe, si in zip(o_uw, q_eff, s)]
            for h, oi in zip(hs, o):
                z = z_ref[0, rows, h * dh:(h + 1) * dh]
                og_ref[0, rows, h * dh:(h + 1) * dh] = _gated_rmsnorm(oi, z, norm_w).astype(og_ref.dtype)
            a_c = [jnp.exp(gl[(c + 1) * c_len - 1:(c + 1) * c_len, :]) for gl in glast]
            s = [si * ai + kri[:, :dh] - _bdot(kri[:, dh:], si) for si, ai, kri in zip(s, a_c, kr)]
        for h, si in zip(hs, s):
            s_ref[h] = si

    @pl.when(tile == pl.num_programs(1) - 1)
    def _():
        sout_ref[0] = s_ref[...]


def _gdn_chunked(proj, gates, grow, conv_w, norm_w, heads):
    bsz, t, _ = proj.shape
    dh = GDN_HEAD_DIM
    vw = heads * dh
    width, ch = conv_w.shape
    tt = GDN_TT
    blk = lambda part: pl.BlockSpec((1, tt, vw), lambda b, i: (b, i, part))
    return pl.pallas_call(
        functools.partial(_gdn_tile_kernel, heads=heads),
        grid=(bsz, t // tt),
        in_specs=[blk(0), blk(1), blk(2), blk(3),
                  pl.BlockSpec((1, tt, LANES), lambda b, i: (b, i, 0)),
                  pl.BlockSpec((1, 1, heads, tt), lambda b, i: (b, i, 0, 0)),
                  pl.BlockSpec((width, ch), lambda b, i: (0, 0)),
                  pl.BlockSpec((1, dh), lambda b, i: (0, 0))],
        out_specs=[pl.BlockSpec((1, tt, vw), lambda b, i: (b, i, 0)),
                   pl.BlockSpec((1, heads, dh, dh), lambda b, i: (b, 0, 0, 0))],
        out_shape=[jax.ShapeDtypeStruct((bsz, t, vw), BF16),
                   jax.ShapeDtypeStruct((bsz, heads, dh, dh), F32)],
        scratch_shapes=[pltpu.VMEM((heads, dh, dh), F32), pltpu.VMEM((SUBLANES, ch), F32)],
        compiler_params=_cparams(2),
        name="gdn_chunked",
    )(proj, proj, proj, proj, gates, grow, conv_w, norm_w.reshape(1, dh))


def _gdn_step_kernel(qkv_ref, z_ref, gates_ref, s0_ref, nw_ref, og_ref, sout_ref, *, heads):
    dh = GDN_HEAD_DIM
    pad = 2 * SUBLANES
    norm_w = nw_ref[...]
    prow = lax.broadcasted_iota(jnp.int32, (pad, dh), 0)

    pairs = [(bi, h) for bi in range(qkv_ref.shape[0]) for h in range(heads)]
    rows = {bi: slice(bi, bi + 1) for bi, _ in pairs}
    q = [qkv_ref[rows[bi], h * dh:(h + 1) * dh] for bi, h in pairs]
    k = [qkv_ref[rows[bi], (heads + h) * dh:(heads + h + 1) * dh] for bi, h in pairs]
    a = [jnp.exp(gates_ref[rows[bi], h:h + 1]) for bi, h in pairs]
    s_kq = [_bdot(jnp.where(prow == 0, ki, jnp.where(prow == 1, qi, 0.0)), s0_ref[bi, h])
            for (bi, h), ki, qi in zip(pairs, k, q)]
    k_parts, d_parts = [], []
    for (bi, h), qi, ki, ai, si in zip(pairs, q, k, a, s_kq):
        v = qkv_ref[rows[bi], (2 * heads + h) * dh:(2 * heads + h + 1) * dh]
        beta = gates_ref[rows[bi], heads + h:heads + h + 1]
        delta = (v - ai * si[0:1, :]) * beta
        o = ai * si[1:2, :] + jnp.sum(ki * qi, axis=-1, keepdims=True) * delta
        og_ref[rows[bi], h * dh:(h + 1) * dh] = _gated_rmsnorm(o, z_ref[rows[bi], h * dh:(h + 1) * dh], norm_w)
        k_hi = ki.astype(BF16).astype(F32)
        d_hi = delta.astype(BF16).astype(F32)
        k_parts.append(jnp.where(prow < 2, k_hi, jnp.where(prow == 2, ki - k_hi, 0.0)))
        d_parts.append(jnp.where(prow == 1, delta - d_hi, jnp.where(prow < 3, d_hi, 0.0)))
    for (bi, h), ai, kp, dp in zip(pairs, a, k_parts, d_parts):
        sout_ref[bi, h] = s0_ref[bi, h] * ai + _bdot_tn(kp, dp)


def _gdn_step(qkvn, proj, gates, s0, norm_w, heads):
    bsz = qkvn.shape[0]
    dh = GDN_HEAD_DIM
    vw = heads * dh
    bb = DEC_BB
    return pl.pallas_call(
        functools.partial(_gdn_step_kernel, heads=heads),
        grid=(bsz // bb,),
        in_specs=[pl.BlockSpec((bb, 3 * vw), lambda i: (i, 0)),
                  pl.BlockSpec((bb, vw), lambda i: (i, 3)),
                  pl.BlockSpec((bb, LANES), lambda i: (i, 0)),
                  pl.BlockSpec((bb, heads, dh, dh), lambda i: (i, 0, 0, 0)),
                  pl.BlockSpec((1, dh), lambda i: (0, 0))],
        out_specs=[pl.BlockSpec((bb, vw), lambda i: (i, 0)),
                   pl.BlockSpec((bb, heads, dh, dh), lambda i: (i, 0, 0, 0))],
        out_shape=[jax.ShapeDtypeStruct((bsz, vw), F32),
                   jax.ShapeDtypeStruct((bsz, heads, dh, dh), F32)],
        compiler_params=_cparams(1),
        name="gdn_step",
    )(qkvn, proj, gates, s0, norm_w.reshape(1, dh))


def _sconv_prompt_kernel(b_ref, c_ref, h_ref, w_ref, o_ref, last_ref):
    ch = c_ref[0] * h_ref[0]
    w = w_ref[...]
    width = w.shape[0]
    t = ch.shape[0]
    row = lax.broadcasted_iota(jnp.int32, ch.shape, 0)
    conv = w[width - 1:width, :] * ch
    for s in range(1, width):
        conv = conv + w[width - 1 - s:width - s, :] * jnp.where(row >= s, pltpu.roll(ch, s, axis=0), 0.0)
    o_ref[0] = (b_ref[0] * conv).astype(o_ref.dtype)
    last_ref[0] = ch[t - (width - 1):, :]


def _sconv_prompt(proj, conv_w):
    bsz, t, _ = proj.shape
    width, d = conv_w.shape
    nb = d // LANES
    blk = lambda off: pl.BlockSpec((1, t, LANES), lambda b, j: (b, 0, j + off))
    return pl.pallas_call(
        _sconv_prompt_kernel,
        grid=(bsz, nb),
        in_specs=[blk(0), blk(nb), blk(2 * nb),
                  pl.BlockSpec((width, LANES), lambda b, j: (0, j))],
        out_specs=[pl.BlockSpec((1, t, LANES), lambda b, j: (b, 0, j)),
                   pl.BlockSpec((1, width - 1, LANES), lambda b, j: (b, 0, j))],
        out_shape=[jax.ShapeDtypeStruct((bsz, t, d), BF16),
                   jax.ShapeDtypeStruct((bsz, width - 1, d), F32)],
        compiler_params=_cparams(2),
        name="sconv_prompt",
    )(proj, proj, proj, conv_w)


def _sconv_sample_kernel(b_ref, c_ref, h_ref, buf_ref, w_ref, o_ref, ch_ref):
    ch = c_ref[...] * h_ref[...]
    w = w_ref[...]
    width = w.shape[0]
    conv = w[width - 1:width, :] * ch
    for s in range(width - 1):
        conv = conv + w[s:s + 1, :] * buf_ref[s]
    o_ref[...] = b_ref[...] * conv
    ch_ref[...] = ch


def _sconv_sample(proj, buf_t, conv_w):
    bsz = proj.shape[0]
    width, d = conv_w.shape
    nb = d // LANES
    blk = lambda off: pl.BlockSpec((bsz, LANES), lambda j: (0, j + off))
    return pl.pallas_call(
        _sconv_sample_kernel,
        grid=(nb,),
        in_specs=[blk(0), blk(nb), blk(2 * nb),
                  pl.BlockSpec((width - 1, bsz, LANES), lambda j: (0, 0, j)),
                  pl.BlockSpec((width, LANES), lambda j: (0, j))],
        out_specs=[pl.BlockSpec((bsz, LANES), lambda j: (0, j)),
                   pl.BlockSpec((bsz, LANES), lambda j: (0, j))],
        out_shape=[jax.ShapeDtypeStruct((bsz, d), F32),
                   jax.ShapeDtypeStruct((bsz, d), F32)],
        compiler_params=_cparams(1),
        name="sconv_sample",
    )(proj, proj, proj, buf_t, conv_w)


def _router_kernel(x_ref, w_ref, o_ref, *, n_experts):
    logits = jnp.dot(x_ref[...], w_ref[...], preferred_element_type=F32,
                     precision=lax.Precision.HIGHEST)
    lane = lax.broadcasted_iota(jnp.int32, logits.shape, 1).astype(F32)
    lg = jnp.where(lane < n_experts, logits, NEG_BIG)
    m1 = jnp.max(lg, axis=-1, keepdims=True)
    i1 = jnp.min(jnp.where(lg == m1, lane, float(LANES)), axis=-1, keepdims=True)
    lg2 = jnp.where(lane == i1, NEG_BIG, lg)
    m2 = jnp.max(lg2, axis=-1, keepdims=True)
    i2 = jnp.min(jnp.where(lg2 == m2, lane, float(LANES)), axis=-1, keepdims=True)
    e = jnp.exp(m2 - m1)
    g1 = 1.0 / (1.0 + e)
    g2 = e / (1.0 + e)
    o_ref[...] = jnp.where(lane == 0, i1, jnp.where(lane == 1, i2,
                           jnp.where(lane == 2, g1, jnp.where(lane == 3, g2, 0.0))))


def _router(x, w_pad, n_experts):
    m, k = x.shape
    tm = TM_TOKEN
    return pl.pallas_call(
        functools.partial(_router_kernel, n_experts=n_experts),
        grid=(m // tm,),
        in_specs=[pl.BlockSpec((tm, k), lambda i: (i, 0)),
                  pl.BlockSpec((k, LANES), lambda i: (0, 0))],
        out_specs=pl.BlockSpec((tm, LANES), lambda i: (i, 0)),
        out_shape=jax.ShapeDtypeStruct((m, LANES), F32),
        compiler_params=_cparams(1),
        name="moe_router",
    )(x, w_pad)


def _row_copy(src_hbm, src_row, dst_ref, dst_row, sem):
    return pltpu.make_async_copy(src_hbm.at[pl.ds(src_row, 1)], dst_ref.at[pl.ds(dst_row, 1)], sem)


def _wait_rows(src_hbm, dst_ref, sem):
    pltpu.make_async_copy(src_hbm.at[pl.ds(0, dst_ref.shape[0])], dst_ref, sem).wait()


def _gather_kernel(idx_ref, nu_ref, src_hbm, o_ref, buf, sem):
    i = pl.program_id(0)
    tg = o_ref.shape[0]
    base = i * tg

    @pl.when(i < nu_ref[0])
    def _():
        for r in range(tg):
            _row_copy(src_hbm, idx_ref[base + r], buf, r, sem).start()
        _wait_rows(src_hbm, buf, sem)
        o_ref[...] = buf[...].astype(o_ref.dtype)

    @pl.when(i >= nu_ref[0])
    def _():
        o_ref[...] = jnp.zeros_like(o_ref)


def _gather_rows(src, row_token, n_used, tg):
    rows = row_token.shape[0]
    d = src.shape[1]
    return pl.pallas_call(
        _gather_kernel,
        grid_spec=pltpu.PrefetchScalarGridSpec(
            num_scalar_prefetch=2,
            grid=(rows // tg,),
            in_specs=[pl.BlockSpec(memory_space=pl.ANY)],
            out_specs=pl.BlockSpec((tg, d), lambda i, idx, nu: (i, 0)),
            scratch_shapes=[pltpu.VMEM((tg, d), src.dtype), pltpu.SemaphoreType.DMA(())]),
        out_shape=jax.ShapeDtypeStruct((rows, d), BF16),
        compiler_params=_cparams(1),
        name="moe_gather",
    )(row_token, n_used, src)


def _moe_ffn1_kernel(te_ref, nu_ref, x_ref, wg_ref, wu_ref, o_ref, wg_bf, wu_bf):
    i = pl.program_id(1)
    new_expert = jnp.logical_or(i == 0, te_ref[i] != te_ref[jnp.maximum(i - 1, 0)])

    @pl.when(new_expert)
    def _():
        wg_bf[...] = wg_ref[0].astype(BF16)
        wu_bf[...] = wu_ref[0].astype(BF16)

    @pl.when(i < nu_ref[0])
    def _():
        _swiglu_slabs(x_ref, wg_bf, wu_bf, o_ref)

    @pl.when(i >= nu_ref[0])
    def _():
        o_ref[...] = jnp.zeros_like(o_ref)


def _moe_ffn1(xs, w_gu, tile_expert, n_used, tm, tn):
    rows, k = xs.shape
    f = w_gu.shape[2] // 2
    nj = f // tn
    return pl.pallas_call(
        _moe_ffn1_kernel,
        grid_spec=pltpu.PrefetchScalarGridSpec(
            num_scalar_prefetch=2,
            grid=(nj, rows // tm),
            in_specs=[pl.BlockSpec((tm, k), lambda j, i, te, nu: (i, 0)),
                      pl.BlockSpec((1, k, tn), lambda j, i, te, nu: (te[i], 0, j)),
                      pl.BlockSpec((1, k, tn), lambda j, i, te, nu: (te[i], 0, j + nj))],
            out_specs=pl.BlockSpec((tm, tn), lambda j, i, te, nu: (i, j)),
            scratch_shapes=[pltpu.VMEM((k, tn), BF16), pltpu.VMEM((k, tn), BF16)]),
        out_shape=jax.ShapeDtypeStruct((rows, f), BF16),
        compiler_params=_cparams(2),
        name="moe_ffn1",
    )(tile_expert, n_used, xs, w_gu, w_gu)


def _moe_ffn2_kernel(te_ref, nu_ref, x_ref, w_ref, o_ref, w_bf):
    i = pl.program_id(1)
    new_expert = jnp.logical_or(i == 0, te_ref[i] != te_ref[jnp.maximum(i - 1, 0)])

    @pl.when(new_expert)
    def _():
        w_bf[...] = w_ref[0].astype(BF16)

    @pl.when(i < nu_ref[0])
    def _():
        o_ref[...] = jnp.dot(x_ref[...], w_bf[...], preferred_element_type=F32)

    @pl.when(i >= nu_ref[0])
    def _():
        o_ref[...] = jnp.zeros_like(o_ref)


def _moe_ffn2(act, w_down, tile_expert, n_used, tm, tn):
    rows, k = act.shape
    d = w_down.shape[2]
    return pl.pallas_call(
        _moe_ffn2_kernel,
        grid_spec=pltpu.PrefetchScalarGridSpec(
            num_scalar_prefetch=2,
            grid=(d // tn, rows // tm),
            in_specs=[pl.BlockSpec((tm, k), lambda j, i, te, nu: (i, 0)),
                      pl.BlockSpec((1, k, tn), lambda j, i, te, nu: (te[i], 0, j))],
            out_specs=pl.BlockSpec((tm, tn), lambda j, i, te, nu: (i, j)),
            scratch_shapes=[pltpu.VMEM((k, tn), BF16)]),
        out_shape=jax.ShapeDtypeStruct((rows, d), F32),
        compiler_params=_cparams(2),
        name="moe_ffn2",
    )(tile_expert, n_used, act, w_down)


def _combine_kernel(pos_ref, ys_hbm, route_ref, res_ref, g_ref, b_ref, o_head, o_tail, buf_a, buf_b, sems,
                    *, n_head_tiles):
    tm = res_ref.shape[0]
    base2 = pl.program_id(0) * (TOP_K * tm)

    for r in range(tm):
        _row_copy(ys_hbm, pos_ref[base2 + TOP_K * r], buf_a, r, sems.at[0]).start()
        _row_copy(ys_hbm, pos_ref[base2 + TOP_K * r + 1], buf_b, r, sems.at[1]).start()
    _wait_rows(ys_hbm, buf_a, sems.at[0])
    _wait_rows(ys_hbm, buf_b, sems.at[1])

    route = route_ref[...]
    y = route[:, 2:3] * buf_a[...] + route[:, 3:4] * buf_b[...]
    out = _layer_norm(ALPHA * res_ref[...] + y, g_ref[...], b_ref[...])

    @pl.when(pl.program_id(0) < n_head_tiles)
    def _():
        o_head[...] = out

    @pl.when(pl.program_id(0) >= n_head_tiles)
    def _():
        o_tail[...] = out


def _moe_combine(ys, pos, route, res, ln_g, ln_b, n_head):
    m, d = res.shape
    tm = TM_COMBINE
    head_tiles = n_head // tm
    return pl.pallas_call(
        functools.partial(_combine_kernel, n_head_tiles=head_tiles),
        grid_spec=pltpu.PrefetchScalarGridSpec(
            num_scalar_prefetch=1,
            grid=(m // tm,),
            in_specs=[pl.BlockSpec(memory_space=pl.ANY),
                      pl.BlockSpec((tm, LANES), lambda i, p: (i, 0)),
                      pl.BlockSpec((tm, d), lambda i, p: (i, 0)),
                      pl.BlockSpec((1, d), lambda i, p: (0, 0)),
                      pl.BlockSpec((1, d), lambda i, p: (0, 0))],
            out_specs=[pl.BlockSpec((tm, d), lambda i, p: (jnp.minimum(i, head_tiles - 1), 0)),
                       pl.BlockSpec((tm, d), lambda i, p: (jnp.maximum(i - head_tiles, 0), 0))],
            scratch_shapes=[pltpu.VMEM((tm, d), F32), pltpu.VMEM((tm, d), F32),
                            pltpu.SemaphoreType.DMA((2,))]),
        out_shape=[jax.ShapeDtypeStruct((n_head, d), F32),
                   jax.ShapeDtypeStruct((m - n_head, d), F32)],
        compiler_params=_cparams(1),
        name="moe_combine",
    )(pos, ys, route, res, ln_g.reshape(1, d), ln_b.reshape(1, d))


def _dispatch_plan(route, n_experts, tm):
    n_tok = route.shape[0]
    ids = route[:, :TOP_K].astype(jnp.int32).reshape(-1)
    n_pairs = ids.shape[0]
    n_tiles = n_pairs // tm + n_experts
    onehot = (ids[:, None] == jnp.arange(n_experts, dtype=jnp.int32)[None, :]).astype(jnp.int32)
    rank = jnp.sum((jnp.cumsum(onehot, axis=0) - onehot) * onehot, axis=1)
    counts = jnp.sum(onehot, axis=0)
    tiles_e = (counts + tm - 1) // tm
    tile_end = jnp.cumsum(tiles_e)
    group_off = (tile_end - tiles_e) * tm
    pos = group_off[ids] + rank
    row_token = jnp.zeros((n_tiles * tm,), jnp.int32).at[pos].set(
        jnp.arange(n_pairs, dtype=jnp.int32) // TOP_K)
    n_used = tile_end[n_experts - 1:]
    tile_ids = jnp.arange(n_tiles, dtype=jnp.int32)
    tile_expert = jnp.sum((tile_ids[:, None] >= tile_end[None, :]).astype(jnp.int32), axis=1)
    last_expert = jnp.sum((n_used - 1 >= tile_end).astype(jnp.int32))
    tile_expert = jnp.minimum(tile_expert, last_expert).astype(jnp.int32)
    return row_token, pos.astype(jnp.int32), tile_expert, n_used.astype(jnp.int32)


def _moe_layer(h_all, n_head, w_router, w_gu, w_down, ln_g, ln_b):
    n_experts = w_router.shape[1]
    w_pad = jnp.pad(w_router, ((0, 0), (0, LANES - n_experts)))
    route = _router(h_all, w_pad, n_experts)
    row_token, pos, tile_expert, n_used = _dispatch_plan(route, n_experts, TM_EXPERT)
    xs = _gather_rows(h_all, row_token, n_used, TM_EXPERT)
    act = _moe_ffn1(xs, w_gu, tile_expert, n_used, TM_EXPERT, TN_EXPERT_UP)
    ys = _moe_ffn2(act, w_down, tile_expert, n_used, TM_EXPERT, TN_EXPERT_DOWN)
    return _moe_combine(ys, pos, route, h_all, ln_g, ln_b, n_head)


def kernel(x_prompt, x_sample, state_gdn_S, state_gdn_conv, state_sconv, ln_g, ln_b, gdn_w_in, gdn_conv_w, gdn_a_log, gdn_dt_bias, gdn_norm_w, gdn_w_out, sc_w_in, sc_conv_w, sc_w_out, ffn_w_gu, ffn_w_down, moe_w_router, moe_w_gu, moe_w_down):
    bsz, t, d = x_prompt.shape
    dec = x_sample.shape[0]
    heads = gdn_a_log.shape[1]
    dh = GDN_HEAD_DIM
    vw = heads * dh
    conv_ch = gdn_conv_w.shape[2]
    assert x_sample.shape[1] == 1 and conv_ch == 3 * vw and gdn_w_in.shape[2] == conv_ch + vw + 2 * heads
    assert t % GDN_TT == 0 and (bsz * t) % TM_PROMPT == 0 and (bsz * t + dec) % TM_TOKEN == 0
    assert (bsz * t) % TM_COMBINE == 0 and dec % TM_COMBINE == 0
    n_p = bsz * t

    xp = x_prompt.reshape(n_p, d)
    xs = x_sample.reshape(dec, d)

    w_in = gdn_w_in[0]
    w_a = w_in[:, conv_ch + vw:conv_ch + vw + heads]
    w_b = w_in[:, conv_ch + vw + heads:]
    w_ab = jnp.pad(jnp.concatenate([w_a, w_b, w_a], axis=1), ((0, 0), (0, LANES - 3 * heads)))
    lane_pad = lambda p: jnp.pad(jnp.concatenate([p, p, p]), (0, LANES - 3 * heads)).reshape(1, LANES)
    alog_row = lane_pad(gdn_a_log[0])
    dtb_row = lane_pad(gdn_dt_bias[0])

    proj_p = _matmul(xp, w_in, conv_ch + vw, TM_PROMPT, 2048, F32, "gdn_in_prompt")
    proj_s = _matmul(xs, w_in, conv_ch + vw, dec, 1024, F32, "gdn_in_sample")
    gates_p = _gdn_gates(xp, w_ab, alog_row, dtb_row, heads, TM_PROMPT, True, "gdn_gates_prompt")
    gates_s = _gdn_gates(xs, w_ab, alog_row, dtb_row, heads, dec, False, "gdn_gates_sample")

    proj_p3 = proj_p.reshape(bsz, t, conv_ch + vw)
    conv_buf_t = jnp.transpose(state_gdn_conv[0], (1, 0, 2))
    qkvn_s = _gdn_prep_sample(proj_s, conv_buf_t, gdn_conv_w[0], heads)

    grow = gates_p[:, :heads].reshape(bsz, t // GDN_TT, GDN_TT, heads).transpose(0, 1, 3, 2)
    og_p, s_prompt = _gdn_chunked(proj_p3, gates_p.reshape(bsz, t, LANES), grow,
                                  gdn_conv_w[0], gdn_norm_w[0], heads)
    og_s, s_sample = _gdn_step(qkvn_s, proj_s, gates_s, state_gdn_S[0], gdn_norm_w[0], heads)

    conv_prompt = proj_p3[:, t - (gdn_conv_w.shape[1] - 1):, :conv_ch]
    conv_sample = jnp.concatenate([state_gdn_conv[0][:, 1:], proj_s[:, None, :conv_ch]], axis=1)

    h_p, hb_p = _matmul_ln(og_p.reshape(n_p, vw), gdn_w_out[0], xp, ln_g[0, 0], ln_b[0, 0], TM_PROMPT,
                           "gdn_out_prompt", also_bf16=True)
    h_s, hb_s = _matmul_ln(og_s, gdn_w_out[0], xs, ln_g[0, 0], ln_b[0, 0], dec, "gdn_out_sample", also_bf16=True)

    d_ff = ffn_w_down.shape[1]
    act_p = _matmul_swiglu(hb_p, ffn_w_gu[0], TM_PROMPT, d_ff // 2, "ffn_up_prompt")
    act_s = _matmul_swiglu(hb_s, ffn_w_gu[0], dec, d_ff // 2, "ffn_up_sample")
    h_p, hb_p = _matmul_ln(act_p, ffn_w_down[0], h_p, ln_g[0, 1], ln_b[0, 1], TM_DOWN, "ffn_down_prompt",
                           also_bf16=True)
    h_s, hb_s = _matmul_ln(act_s, ffn_w_down[0], h_s, ln_g[0, 1], ln_b[0, 1], dec, "ffn_down_sample",
                           also_bf16=True)

    sproj_p = _matmul(hb_p, sc_w_in[0], 3 * d, TM_PROMPT, 1024, F32, "sc_in_prompt")
    sproj_s = _matmul(hb_s, sc_w_in[0], 3 * d, dec, 1024, F32, "sc_in_sample")
    sg_p, sconv_prompt = _sconv_prompt(sproj_p.reshape(bsz, t, 3 * d), sc_conv_w[0])
    sbuf_t = jnp.transpose(state_sconv[0], (1, 0, 2))
    sg_s, ch_s = _sconv_sample(sproj_s, sbuf_t, sc_conv_w[0])
    sconv_sample = jnp.concatenate([state_sconv[0][:, 1:], ch_s[:, None, :]], axis=1)

    h_all = _matmul_ln(sg_p.reshape(n_p, d), sc_w_out[0], h_p, ln_g[1, 0], ln_b[1, 0], TM_PROMPT,
                       "sc_out", tail=(sg_s, h_s))

    y_p, y_s = _moe_layer(h_all, n_p, moe_w_router[0], moe_w_gu[0], moe_w_down[0], ln_g[1, 1], ln_b[1, 1])

    return (y_p.reshape(bsz, t, d), y_s.reshape(dec, 1, d), s_prompt[None], s_sample[None],
            conv_prompt[None], conv_sample[None], sconv_prompt[None], sconv_sample[None])
```

```python
import functools

import jax
import jax.numpy as jnp
from jax import lax
from jax.experimental import pallas as pl
from jax.experimental.pallas import tpu as pltpu

F32 = jnp.float32
BF16 = jnp.bfloat16

DEPTH = 2
ALPHA = (2.0 * DEPTH) ** 0.25
LN_EPS = 1e-5
NORM_EPS = 1e-6
GDN_HEAD_DIM = 128
GDN_CHUNK = 64
GDN_SUB = 16
TOP_K = 2

LANES = 128
SUBLANES = 8
VMEM_LIMIT = 56 * 1024 * 1024
NEG_BIG = -1e30

TM_PROMPT = 1024
TM_DOWN = 512
GDN_TT = 256
GDN_GROUP = 4
TM_EXPERT = 512
TN_EXPERT_UP = 1792
TN_EXPERT_DOWN = 1024
TM_TOKEN = 384
TM_COMBINE = 1024
DMA_SUB = 128
DEC_BB = 8
MM_SLAB = 256
TN_SCONV = 512


def _cparams(n_axes, vmem=VMEM_LIMIT):
    return pltpu.CompilerParams(
        dimension_semantics=("arbitrary",) * n_axes, vmem_limit_bytes=vmem)


def _bdot(a, b):
    return jnp.dot(a.astype(BF16), b.astype(BF16), preferred_element_type=F32)


def _bdot_nt(a, b):
    return lax.dot_general(a.astype(BF16), b.astype(BF16),
                           (((1,), (1,)), ((), ())), preferred_element_type=F32)


def _bdot_tn(a, b):
    return lax.dot_general(a.astype(BF16), b.astype(BF16),
                           (((0,), (0,)), ((), ())), preferred_element_type=F32)


def _silu(x):
    return x * (0.5 * jnp.tanh(0.5 * x) + 0.5)


def _swiglu_slabs(x_ref, wg_bf, wu_bf, o_ref):
    slab = min(MM_SLAB, x_ref.shape[0])
    for r0 in range(0, x_ref.shape[0], slab):
        xb = x_ref[r0:r0 + slab, :].astype(BF16)
        g = jnp.dot(xb, wg_bf[...], preferred_element_type=F32)
        u = jnp.dot(xb, wu_bf[...], preferred_element_type=F32)
        o_ref[r0:r0 + slab, :] = (_silu(g) * u).astype(o_ref.dtype)


def _layer_norm(r, g, b):
    mu = jnp.mean(r, axis=-1, keepdims=True)
    d = r - mu
    var = jnp.mean(d * d, axis=-1, keepdims=True)
    return d * lax.rsqrt(var + LN_EPS) * g + b


def _mm_kernel(x_ref, w_ref, o_ref, wbf_ref):
    @pl.when(pl.program_id(1) == 0)
    def _():
        wbf_ref[...] = w_ref[...].astype(BF16)

    o_ref[...] = jnp.dot(x_ref[...].astype(BF16), wbf_ref[...],
                         preferred_element_type=F32).astype(o_ref.dtype)


def _matmul(x, w, n_cols, tm, tn, out_dtype, name):
    m, k = x.shape
    return pl.pallas_call(
        _mm_kernel,
        grid=(n_cols // tn, m // tm),
        in_specs=[pl.BlockSpec((tm, k), lambda j, i: (i, 0)),
                  pl.BlockSpec((k, tn), lambda j, i: (0, j))],
        out_specs=pl.BlockSpec((tm, tn), lambda j, i: (i, j)),
        out_shape=jax.ShapeDtypeStruct((m, n_cols), out_dtype),
        scratch_shapes=[pltpu.VMEM((k, tn), BF16)],
        compiler_params=_cparams(2),
        name=name,
    )(x, w)


def _mm_swiglu_kernel(x_ref, wg_ref, wu_ref, o_ref, wg_bf, wu_bf):
    @pl.when(pl.program_id(1) == 0)
    def _():
        wg_bf[...] = wg_ref[...].astype(BF16)
        wu_bf[...] = wu_ref[...].astype(BF16)

    _swiglu_slabs(x_ref, wg_bf, wu_bf, o_ref)


def _matmul_swiglu(x, w_gu, tm, tn, name):
    m, k = x.shape
    f = w_gu.shape[1] // 2
    nj = f // tn
    w_mode = dict(pipeline_mode=pl.Buffered(1)) if nj == 1 else {}
    return pl.pallas_call(
        _mm_swiglu_kernel,
        grid=(nj, m // tm),
        in_specs=[pl.BlockSpec((tm, k), lambda j, i: (i, 0)),
                  pl.BlockSpec((k, tn), lambda j, i: (0, j), **w_mode),
                  pl.BlockSpec((k, tn), lambda j, i: (0, j + nj), **w_mode)],
        out_specs=pl.BlockSpec((tm, tn), lambda j, i: (i, j)),
        out_shape=jax.ShapeDtypeStruct((m, f), BF16),
        scratch_shapes=[pltpu.VMEM((k, tn), BF16), pltpu.VMEM((k, tn), BF16)],
        compiler_params=_cparams(2),
        name=name,
    )(x, w_gu, w_gu)


def _mm_ln_kernel(x_ref, w_ref, res_ref, g_ref, b_ref, *rest, n_out, has_tail):
    tail_refs = rest[:2] if has_tail else ()
    out_refs = rest[len(tail_refs):len(tail_refs) + n_out]
    wbf_ref = rest[-1]
    i = pl.program_id(0)

    @pl.when(i == 0)
    def _():
        wbf_ref[...] = w_ref[...].astype(BF16)

    def emit(xr, rr):
        slab = min(MM_SLAB, xr.shape[0])
        for r0 in range(0, xr.shape[0], slab):
            rows = slice(r0, r0 + slab)
            y = jnp.dot(xr[rows, :].astype(BF16), wbf_ref[...], preferred_element_type=F32)
            h = _layer_norm(ALPHA * rr[rows, :] + y, g_ref[...], b_ref[...])
            for o_ref in out_refs:
                o_ref[rows, :] = h.astype(o_ref.dtype)

    if has_tail:
        n_main = pl.num_programs(0) - 1

        @pl.when(i < n_main)
        def _():
            emit(x_ref, res_ref)

        @pl.when(i == n_main)
        def _():
            emit(*tail_refs)
    else:
        emit(x_ref, res_ref)


def _matmul_ln(x, w, res, ln_g, ln_b, tm, name, also_bf16=False, tail=None):
    m, k = x.shape
    d = w.shape[1]
    n_main = m // tm
    main_idx = lambda i: (jnp.minimum(i, n_main - 1), 0)
    operands = [x, w, res, ln_g.reshape(1, d), ln_b.reshape(1, d)]
    in_specs = [pl.BlockSpec((tm, k), main_idx),
                pl.BlockSpec((k, d), lambda i: (0, 0)),
                pl.BlockSpec((tm, d), main_idx),
                pl.BlockSpec((1, d), lambda i: (0, 0)),
                pl.BlockSpec((1, d), lambda i: (0, 0))]
    m_out = m
    if tail is not None:
        m_tail = tail[0].shape[0]
        assert m_tail < tm and m_tail % (2 * SUBLANES) == 0
        operands += list(tail)
        in_specs += [pl.BlockSpec((m_tail, k), lambda i: (0, 0)), pl.BlockSpec((m_tail, d), lambda i: (0, 0))]
        m_out = m + m_tail
    out_shape = [jax.ShapeDtypeStruct((m_out, d), F32)]
    if also_bf16:
        out_shape.append(jax.ShapeDtypeStruct((m_out, d), BF16))
    out = pl.pallas_call(
        functools.partial(_mm_ln_kernel, n_out=len(out_shape), has_tail=tail is not None),
        grid=(n_main + int(tail is not None),),
        in_specs=in_specs,
        out_specs=[pl.BlockSpec((tm, d), lambda i: (i, 0)) for _ in out_shape],
        out_shape=out_shape,
        scratch_shapes=[pltpu.VMEM((k, d), BF16)],
        compiler_params=_cparams(1),
        name=name,
    )(*operands)
    return out if also_bf16 else out[0]


def _gates_kernel(x_ref, w_ref, alog_ref, dtb_ref, o_ref, *, heads, chunk_cumsum):
    a = _bdot(x_ref[...], w_ref[...])
    z = a + dtb_ref[...]
    softplus = jnp.maximum(z, 0.0) + jnp.log1p(jnp.exp(-jnp.abs(z)))
    g = -jnp.exp(alog_ref[...]) * softplus
    total = g
    if chunk_cumsum:
        n_rows = g.shape[0]
        row = jnp.bitwise_and(lax.broadcasted_iota(jnp.int32, g.shape, 0), GDN_CHUNK - 1)
        step = 1
        while step < GDN_CHUNK:
            g = g + jnp.where(row >= step, pltpu.roll(g, step, axis=0), 0.0)
            step *= 2
        total = g
        step = 1
        while step < GDN_CHUNK:
            total = jnp.where(row + step < GDN_CHUNK, pltpu.roll(total, n_rows - step, axis=0), total)
            step *= 2
    lane = lax.broadcasted_iota(jnp.int32, g.shape, 1)
    o_ref[...] = jnp.where(lane < heads, g, jnp.where(lane < 2 * heads, jax.nn.sigmoid(a), total))


def _gdn_gates(x, w_ab, alog_row, dtb_row, heads, tm, chunk_cumsum, name):
    m, k = x.shape
    return pl.pallas_call(
        functools.partial(_gates_kernel, heads=heads, chunk_cumsum=chunk_cumsum),
        grid=(m // tm,),
        in_specs=[pl.BlockSpec((tm, k), lambda i: (i, 0)),
                  pl.BlockSpec((k, LANES), lambda i: (0, 0)),
                  pl.BlockSpec((1, LANES), lambda i: (0, 0)),
                  pl.BlockSpec((1, LANES), lambda i: (0, 0))],
        out_specs=pl.BlockSpec((tm, LANES), lambda i: (i, 0)),
        out_shape=jax.ShapeDtypeStruct((m, LANES), F32),
        compiler_params=_cparams(1),
        name=name,
    )(x, w_ab, alog_row, dtb_row)


def _qkv_finish(conv, j, heads):
    y = _silu(conv)
    inv = lax.rsqrt(jnp.sum(y * y, axis=-1, keepdims=True) + NORM_EPS)
    scale = jnp.where(j < heads, inv * (GDN_HEAD_DIM ** -0.5),
                      jnp.where(j < 2 * heads, inv, 1.0))
    return y * scale


def _gdn_prep_sample_kernel(x_ref, buf_ref, w_ref, o_ref, *, heads):
    w = w_ref[...]
    width = w.shape[0]
    conv = w[width - 1:width, :] * x_ref[...]
    for s in range(width - 1):
        conv = conv + w[s:s + 1, :] * buf_ref[s]
    o_ref[...] = _qkv_finish(conv, pl.program_id(0), heads)


def _gdn_prep_sample(proj, buf_t, conv_w, heads):
    bsz = proj.shape[0]
    width, ch = conv_w.shape
    return pl.pallas_call(
        functools.partial(_gdn_prep_sample_kernel, heads=heads),
        grid=(ch // GDN_HEAD_DIM,),
        in_specs=[pl.BlockSpec((bsz, GDN_HEAD_DIM), lambda j: (0, j)),
                  pl.BlockSpec((width - 1, bsz, GDN_HEAD_DIM), lambda j: (0, 0, j)),
                  pl.BlockSpec((width, GDN_HEAD_DIM), lambda j: (0, j))],
        out_specs=pl.BlockSpec((bsz, GDN_HEAD_DIM), lambda j: (0, j)),
        out_shape=jax.ShapeDtypeStruct((bsz, ch), F32),
        compiler_params=_cparams(1),
        name="gdn_prep_sample",
    )(proj, buf_t, conv_w)


def _gated_rmsnorm(o, z, norm_w):
    on = o * lax.rsqrt(jnp.mean(o * o, axis=-1, keepdims=True) + NORM_EPS) * norm_w
    return on * _silu(z)


def _gdn_tile_kernel(q_ref, k_ref, v_ref, z_ref, gates_ref, grow_ref, cw_ref, nw_ref,
                     og_ref, sout_ref, s_ref, halo_ref, *, heads):
    c_len = GDN_CHUNK
    dh = GDN_HEAD_DIM
    tt = q_ref.shape[1]
    n_c = tt // c_len
    width = cw_ref.shape[0]
    tile = pl.program_id(1)

    @pl.when(tile == 0)
    def _():
        s_ref[...] = jnp.zeros_like(s_ref)
        halo_ref[...] = jnp.zeros_like(halo_ref)

    row = lax.broadcasted_iota(jnp.int32, (tt, tt), 0)
    col = lax.broadcasted_iota(jnp.int32, (tt, tt), 1)
    same_chunk = jnp.bitwise_and(row, -c_len) == jnp.bitwise_and(col, -c_len)
    tril = jnp.logical_and(same_chunk, row >= col)
    strict = jnp.logical_and(same_chunk, row > col)
    same_sub = jnp.bitwise_and(row, -GDN_SUB) == jnp.bitwise_and(col, -GDN_SUB)
    halo_row = lax.broadcasted_iota(jnp.int32, (SUBLANES, dh), 0)
    gate_tile = gates_ref[0]
    grow_tile = grow_ref[0, 0]
    norm_w = nw_ref[...]

    def conv_silu(x_ref, h, part):
        lanes = slice(h * dh, (h + 1) * dh)
        wl = slice((part * heads + h) * dh, (part * heads + h + 1) * dh)
        x = x_ref[0, :, lanes]
        halo = halo_ref[:, wl]
        w = cw_ref[:, wl]
        acc = w[width - 1:width, :] * x
        for s in range(1, width):
            xr = pltpu.roll(x, s, axis=0)
            top = jnp.where(halo_row < s, pltpu.roll(halo, s, axis=0), xr[:SUBLANES, :])
            acc = acc + w[width - 1 - s:width - s, :] * jnp.concatenate([top, xr[SUBLANES:, :]], axis=0)
        halo_ref[:, wl] = x[tt - SUBLANES:, :]
        return _silu(acc)

    def l2n(y, scale):
        return y * (lax.rsqrt(jnp.sum(y * y, axis=-1, keepdims=True) + NORM_EPS) * scale)

    for g0 in range(0, heads, GDN_GROUP):
        hs = range(g0, g0 + GDN_GROUP)
        q = [l2n(conv_silu(q_ref, h, 0), dh ** -0.5) for h in hs]
        k = [l2n(conv_silu(k_ref, h, 1), 1.0) for h in hs]
        v = [conv_silu(v_ref, h, 2) for h in hs]
        gcol = [gate_tile[:, h:h + 1] for h in hs]
        beta = [gate_tile[:, heads + h:heads + h + 1] for h in hs]
        glast = [gate_tile[:, 2 * heads + h:2 * heads + h + 1] for h in hs]
        decay = [jnp.exp(jnp.where(tril, gc - grow_tile[h:h + 1, :], NEG_BIG)) for h, gc in zip(hs, gcol)]
        kb = [ki * bi for ki, bi in zip(k, beta)]
        a_mat = [jnp.where(strict, _bdot_nt(kbi, ki) * di, 0.0) for kbi, ki, di in zip(kb, k, decay)]
        qk = [jnp.where(tril, _bdot_nt(qi, ki) * di, 0.0) for qi, ki, di in zip(q, k, decay)]

        p = [jnp.where(same_sub, ai, 0.0) for ai in a_mat]
        e_mat = [ai - pi for ai, pi in zip(a_mat, p)]
        x = [-pi for pi in p]
        span = 1
        while 2 * span < GDN_SUB:
            p = [_bdot(pi, pi) for pi in p]
            xp = [_bdot(xi, pi) for xi, pi in zip(x, p)]
            x = [xi + pi + xpi for xi, pi, xpi in zip(x, p, xp)]
            span *= 2
        rhs = [jnp.concatenate([vi * bi, kbi * jnp.exp(gc)], axis=-1)
               for vi, bi, kbi, gc in zip(v, beta, kb, gcol)]
        n_mat = [ei + _bdot(xi, ei) for ei, xi in zip(e_mat, x)]
        r = [ri + _bdot(xi, ri) for ri, xi in zip(rhs, x)]
        r = [ri - _bdot(ni, ri) for ri, ni in zip(r, n_mat)]
        span = 2
        while span < c_len // GDN_SUB:
            n_mat = [_bdot(ni, ni) for ni in n_mat]
            r = [ri + _bdot(ni, ri) for ri, ni in zip(r, n_mat)]
            span *= 2

        o_uw = [_bdot(qki, ri) for qki, ri in zip(qk, r)]
        q_eff = [qi * jnp.exp(gc) - oi[:, dh:] for qi, gc, oi in zip(q, gcol, o_uw)]
        k_dec = [ki * jnp.exp(gl - gc) for ki, gl, gc in zip(k, glast, gcol)]
        s = [s_ref[h] for h in hs]
        for c in range(n_c):
            rows = slice(c * c_len, (c + 1) * c_len)
            kr = [_bdot_tn(kd[rows, :], ri[rows, :]) for kd, ri in zip(k_dec, r)]
            o = [oi[rows, :dh] + _bdot(qe[rows, :], si) for oi, qe, si in zip(o_uw, q_eff, s)]
            for h, oi in zip(hs, o):
                z = z_ref[0, rows, h * dh:(h + 1) * dh]
                og_ref[0, rows, h * dh:(h + 1) * dh] = _gated_rmsnorm(oi, z, norm_w).astype(og_ref.dtype)
            a_c = [jnp.exp(gl[(c + 1) * c_len - 1:(c + 1) * c_len, :]) for gl in glast]
            s = [si * ai + kri[:, :dh] - _bdot(kri[:, dh:], si) for si, ai, kri in zip(s, a_c, kr)]
        for h, si in zip(hs, s):
            s_ref[h] = si

    @pl.when(tile == pl.num_programs(1) - 1)
    def _():
        sout_ref[0] = s_ref[...]


def _gdn_chunked(proj, gates, grow, conv_w, norm_w, heads):
    bsz, t, _ = proj.shape
    dh = GDN_HEAD_DIM
    vw = heads * dh
    width, ch = conv_w.shape
    tt = GDN_TT
    blk = lambda part: pl.BlockSpec((1, tt, vw), lambda b, i: (b, i, part))
    return pl.pallas_call(
        functools.partial(_gdn_tile_kernel, heads=heads),
        grid=(bsz, t // tt),
        in_specs=[blk(0), blk(1), blk(2), blk(3),
                  pl.BlockSpec((1, tt, LANES), lambda b, i: (b, i, 0)),
                  pl.BlockSpec((1, 1, heads, tt), lambda b, i: (b, i, 0, 0)),
                  pl.BlockSpec((width, ch), lambda b, i: (0, 0)),
                  pl.BlockSpec((1, dh), lambda b, i: (0, 0))],
        out_specs=[pl.BlockSpec((1, tt, vw), lambda b, i: (b, i, 0)),
                   pl.BlockSpec((1, heads, dh, dh), lambda b, i: (b, 0, 0, 0))],
        out_shape=[jax.ShapeDtypeStruct((bsz, t, vw), BF16),
                   jax.ShapeDtypeStruct((bsz, heads, dh, dh), F32)],
        scratch_shapes=[pltpu.VMEM((heads, dh, dh), F32), pltpu.VMEM((SUBLANES, ch), F32)],
        compiler_params=_cparams(2),
        name="gdn_chunked",
    )(proj, proj, proj, proj, gates, grow, conv_w, norm_w.reshape(1, dh))


def _gdn_step_kernel(qkv_ref, z_ref, gates_ref, s0_ref, nw_ref, og_ref, sout_ref, *, heads):
    dh = GDN_HEAD_DIM
    pad = 2 * SUBLANES
    norm_w = nw_ref[...]
    prow = lax.broadcasted_iota(jnp.int32, (pad, dh), 0)

    pairs = [(bi, h) for bi in range(qkv_ref.shape[0]) for h in range(heads)]
    rows = {bi: slice(bi, bi + 1) for bi, _ in pairs}
    q = [qkv_ref[rows[bi], h * dh:(h + 1) * dh] for bi, h in pairs]
    k = [qkv_ref[rows[bi], (heads + h) * dh:(heads + h + 1) * dh] for bi, h in pairs]
    a = [jnp.exp(gates_ref[rows[bi], h:h + 1]) for bi, h in pairs]
    s_kq = [_bdot(jnp.where(prow == 0, ki, jnp.where(prow == 1, qi, 0.0)), s0_ref[bi, h])
            for (bi, h), ki, qi in zip(pairs, k, q)]
    k_parts, d_parts = [], []
    for (bi, h), qi, ki, ai, si in zip(pairs, q, k, a, s_kq):
        v = qkv_ref[rows[bi], (2 * heads + h) * dh:(2 * heads + h + 1) * dh]
        beta = gates_ref[rows[bi], heads + h:heads + h + 1]
        delta = (v - ai * si[0:1, :]) * beta
        o = ai * si[1:2, :] + jnp.sum(ki * qi, axis=-1, keepdims=True) * delta
        og_ref[rows[bi], h * dh:(h + 1) * dh] = _gated_rmsnorm(o, z_ref[rows[bi], h * dh:(h + 1) * dh], norm_w)
        k_hi = ki.astype(BF16).astype(F32)
        d_hi = delta.astype(BF16).astype(F32)
        k_parts.append(jnp.where(prow < 2, k_hi, jnp.where(prow == 2, ki - k_hi, 0.0)))
        d_parts.append(jnp.where(prow == 1, delta - d_hi, jnp.where(prow < 3, d_hi, 0.0)))
    for (bi, h), ai, kp, dp in zip(pairs, a, k_parts, d_parts):
        sout_ref[bi, h] = s0_ref[bi, h] * ai + _bdot_tn(kp, dp)


def _gdn_step(qkvn, proj, gates, s0, norm_w, heads):
    bsz = qkvn.shape[0]
    dh = GDN_HEAD_DIM
    vw = heads * dh
    bb = DEC_BB
    return pl.pallas_call(
        functools.partial(_gdn_step_kernel, heads=heads),
        grid=(bsz // bb,),
        in_specs=[pl.BlockSpec((bb, 3 * vw), lambda i: (i, 0)),
                  pl.BlockSpec((bb, vw), lambda i: (i, 3)),
                  pl.BlockSpec((bb, LANES), lambda i: (i, 0)),
                  pl.BlockSpec((bb, heads, dh, dh), lambda i: (i, 0, 0, 0)),
                  pl.BlockSpec((1, dh), lambda i: (0, 0))],
        out_specs=[pl.BlockSpec((bb, vw), lambda i: (i, 0)),
                   pl.BlockSpec((bb, heads, dh, dh), lambda i: (i, 0, 0, 0))],
        out_shape=[jax.ShapeDtypeStruct((bsz, vw), F32),
                   jax.ShapeDtypeStruct((bsz, heads, dh, dh), F32)],
        compiler_params=_cparams(1),
        name="gdn_step",
    )(qkvn, proj, gates, s0, norm_w.reshape(1, dh))


def _sconv_prompt_kernel(x_ref, wb_ref, wc_ref, wh_ref, cw_ref, o_ref, last_ref,
                         wb_bf, wc_bf, wh_bf, halo_ref, *, tiles_per_seq):
    i = pl.program_id(1)

    @pl.when(i == 0)
    def _():
        wb_bf[...] = wb_ref[...].astype(BF16)
        wc_bf[...] = wc_ref[...].astype(BF16)
        wh_bf[...] = wh_ref[...].astype(BF16)

    @pl.when(lax.rem(i, tiles_per_seq) == 0)
    def _():
        halo_ref[...] = jnp.zeros_like(halo_ref)

    xb = x_ref[...].astype(BF16)
    gate = jnp.dot(xb, wb_bf[...], preferred_element_type=F32)
    ch = (jnp.dot(xb, wc_bf[...], preferred_element_type=F32)
          * jnp.dot(xb, wh_bf[...], preferred_element_type=F32))
    w = cw_ref[...]
    width = w.shape[0]
    tm = ch.shape[0]
    halo = halo_ref[...]
    halo_row = lax.broadcasted_iota(jnp.int32, halo.shape, 0)
    conv = w[width - 1:width, :] * ch
    for s in range(1, width):
        xr = pltpu.roll(ch, s, axis=0)
        top = jnp.where(halo_row < s, pltpu.roll(halo, s, axis=0), xr[:SUBLANES, :])
        conv = conv + w[width - 1 - s:width - s, :] * jnp.concatenate([top, xr[SUBLANES:, :]], axis=0)
    halo_ref[...] = ch[tm - SUBLANES:, :]
    o_ref[...] = (gate * conv).astype(o_ref.dtype)
    last_ref[0] = ch[tm - (width - 1):, :]


def _sconv_prompt(x, w_in, conv_w, bsz, tm, tn):
    m, k = x.shape
    width, d = conv_w.shape
    t = m // bsz
    tiles_per_seq = t // tm
    nb = d // tn
    wblk = lambda off: pl.BlockSpec((k, tn), lambda j, i: (0, j + off))
    return pl.pallas_call(
        functools.partial(_sconv_prompt_kernel, tiles_per_seq=tiles_per_seq),
        grid=(nb, m // tm),
        in_specs=[pl.BlockSpec((tm, k), lambda j, i: (i, 0)),
                  wblk(0), wblk(nb), wblk(2 * nb),
                  pl.BlockSpec((width, tn), lambda j, i: (0, j))],
        out_specs=[pl.BlockSpec((tm, tn), lambda j, i: (i, j)),
                   pl.BlockSpec((1, width - 1, tn), lambda j, i: (i // tiles_per_seq, 0, j))],
        out_shape=[jax.ShapeDtypeStruct((m, d), BF16),
                   jax.ShapeDtypeStruct((bsz, width - 1, d), F32)],
        scratch_shapes=[pltpu.VMEM((k, tn), BF16), pltpu.VMEM((k, tn), BF16), pltpu.VMEM((k, tn), BF16),
                        pltpu.VMEM((SUBLANES, tn), F32)],
        compiler_params=_cparams(2),
        name="sconv_prompt",
    )(x, w_in, w_in, w_in, conv_w)


def _sconv_sample_kernel(b_ref, c_ref, h_ref, buf_ref, w_ref, o_ref, ch_ref):
    ch = c_ref[...] * h_ref[...]
    w = w_ref[...]
    width = w.shape[0]
    conv = w[width - 1:width, :] * ch
    for s in range(width - 1):
        conv = conv + w[s:s + 1, :] * buf_ref[s]
    o_ref[...] = b_ref[...] * conv
    ch_ref[...] = ch


def _sconv_sample(proj, buf_t, conv_w):
    bsz = proj.shape[0]
    width, d = conv_w.shape
    nb = d // LANES
    blk = lambda off: pl.BlockSpec((bsz, LANES), lambda j: (0, j + off))
    return pl.pallas_call(
        _sconv_sample_kernel,
        grid=(nb,),
        in_specs=[blk(0), blk(nb), blk(2 * nb),
                  pl.BlockSpec((width - 1, bsz, LANES), lambda j: (0, 0, j)),
                  pl.BlockSpec((width, LANES), lambda j: (0, j))],
        out_specs=[pl.BlockSpec((bsz, LANES), lambda j: (0, j)),
                   pl.BlockSpec((bsz, LANES), lambda j: (0, j))],
        out_shape=[jax.ShapeDtypeStruct((bsz, d), F32),
                   jax.ShapeDtypeStruct((bsz, d), F32)],
        compiler_params=_cparams(1),
        name="sconv_sample",
    )(proj, proj, proj, buf_t, conv_w)


def _router_kernel(x_ref, w_ref, o_ref, *, n_experts):
    logits = jnp.dot(x_ref[...], w_ref[...], preferred_element_type=F32,
                     precision=lax.Precision.HIGHEST)
    lane = lax.broadcasted_iota(jnp.int32, logits.shape, 1).astype(F32)
    lg = jnp.where(lane < n_experts, logits, NEG_BIG)
    m1 = jnp.max(lg, axis=-1, keepdims=True)
    i1 = jnp.min(jnp.where(lg == m1, lane, float(LANES)), axis=-1, keepdims=True)
    lg2 = jnp.where(lane == i1, NEG_BIG, lg)
    m2 = jnp.max(lg2, axis=-1, keepdims=True)
    i2 = jnp.min(jnp.where(lg2 == m2, lane, float(LANES)), axis=-1, keepdims=True)
    e = jnp.exp(m2 - m1)
    g1 = 1.0 / (1.0 + e)
    g2 = e / (1.0 + e)
    o_ref[...] = jnp.where(lane == 0, i1, jnp.where(lane == 1, i2,
                           jnp.where(lane == 2, g1, jnp.where(lane == 3, g2, 0.0))))


def _router(x, w_pad, n_experts):
    m, k = x.shape
    tm = TM_TOKEN
    return pl.pallas_call(
        functools.partial(_router_kernel, n_experts=n_experts),
        grid=(m // tm,),
        in_specs=[pl.BlockSpec((tm, k), lambda i: (i, 0)),
                  pl.BlockSpec((k, LANES), lambda i: (0, 0))],
        out_specs=pl.BlockSpec((tm, LANES), lambda i: (i, 0)),
        out_shape=jax.ShapeDtypeStruct((m, LANES), F32),
        compiler_params=_cparams(1),
        name="moe_router",
    )(x, w_pad)


def _row_copy(src_hbm, src_row, dst_ref, dst_row, sem):
    return pltpu.make_async_copy(src_hbm.at[pl.ds(src_row, 1)], dst_ref.at[pl.ds(dst_row, 1)], sem)


def _wait_rows(src_hbm, dst_ref, sem):
    pltpu.make_async_copy(src_hbm.at[pl.ds(0, dst_ref.shape[0])], dst_ref, sem).wait()


def _gather_kernel(idx_ref, nu_ref, src_hbm, o_ref, buf, sems):
    i = pl.program_id(0)
    tg = o_ref.shape[0]
    n_sub = tg // DMA_SUB
    base = i * tg

    def issue(j, slot):
        for r in range(DMA_SUB):
            _row_copy(src_hbm, idx_ref[base + j * DMA_SUB + r], buf.at[slot], r, sems.at[slot]).start()

    @pl.when(i < nu_ref[0])
    def _():
        issue(0, 0)

        def body(j, c):
            slot = jnp.bitwise_and(j, 1)

            @pl.when(j + 1 < n_sub)
            def _():
                issue(j + 1, 1 - slot)

            _wait_rows(src_hbm, buf.at[slot], sems.at[slot])
            o_ref[pl.ds(pl.multiple_of(j * DMA_SUB, DMA_SUB), DMA_SUB), :] = buf[slot].astype(o_ref.dtype)
            return c
        lax.fori_loop(0, n_sub, body, 0)

    @pl.when(i >= nu_ref[0])
    def _():
        o_ref[...] = jnp.zeros_like(o_ref)


def _gather_rows(src, row_token, n_used, tg):
    rows = row_token.shape[0]
    d = src.shape[1]
    return pl.pallas_call(
        _gather_kernel,
        grid_spec=pltpu.PrefetchScalarGridSpec(
            num_scalar_prefetch=2,
            grid=(rows // tg,),
            in_specs=[pl.BlockSpec(memory_space=pl.ANY)],
            out_specs=pl.BlockSpec((tg, d), lambda i, idx, nu: (i, 0)),
            scratch_shapes=[pltpu.VMEM((2, DMA_SUB, d), src.dtype), pltpu.SemaphoreType.DMA((2,))]),
        out_shape=jax.ShapeDtypeStruct((rows, d), BF16),
        compiler_params=_cparams(1),
        name="moe_gather",
    )(row_token, n_used, src)


def _moe_ffn1_kernel(te_ref, nu_ref, x_ref, wg_ref, wu_ref, o_ref, wg_bf, wu_bf):
    i = pl.program_id(1)
    new_expert = jnp.logical_or(i == 0, te_ref[i] != te_ref[jnp.maximum(i - 1, 0)])

    @pl.when(new_expert)
    def _():
        wg_bf[...] = wg_ref[0].astype(BF16)
        wu_bf[...] = wu_ref[0].astype(BF16)

    @pl.when(i < nu_ref[0])
    def _():
        _swiglu_slabs(x_ref, wg_bf, wu_bf, o_ref)

    @pl.when(i >= nu_ref[0])
    def _():
        o_ref[...] = jnp.zeros_like(o_ref)


def _moe_ffn1(xs, w_gu, tile_expert, n_used, tm, tn):
    rows, k = xs.shape
    f = w_gu.shape[2] // 2
    nj = f // tn
    return pl.pallas_call(
        _moe_ffn1_kernel,
        grid_spec=pltpu.PrefetchScalarGridSpec(
            num_scalar_prefetch=2,
            grid=(nj, rows // tm),
            in_specs=[pl.BlockSpec((tm, k), lambda j, i, te, nu: (i, 0)),
                      pl.BlockSpec((1, k, tn), lambda j, i, te, nu: (te[i], 0, j)),
                      pl.BlockSpec((1, k, tn), lambda j, i, te, nu: (te[i], 0, j + nj))],
            out_specs=pl.BlockSpec((tm, tn), lambda j, i, te, nu: (i, j)),
            scratch_shapes=[pltpu.VMEM((k, tn), BF16), pltpu.VMEM((k, tn), BF16)]),
        out_shape=jax.ShapeDtypeStruct((rows, f), BF16),
        compiler_params=_cparams(2),
        name="moe_ffn1",
    )(tile_expert, n_used, xs, w_gu, w_gu)


def _moe_ffn2_kernel(te_ref, nu_ref, x_ref, w_ref, o_ref, w_bf):
    i = pl.program_id(1)
    new_expert = jnp.logical_or(i == 0, te_ref[i] != te_ref[jnp.maximum(i - 1, 0)])

    @pl.when(new_expert)
    def _():
        w_bf[...] = w_ref[0].astype(BF16)

    @pl.when(i < nu_ref[0])
    def _():
        o_ref[...] = jnp.dot(x_ref[...], w_bf[...], preferred_element_type=F32)

    @pl.when(i >= nu_ref[0])
    def _():
        o_ref[...] = jnp.zeros_like(o_ref)


def _moe_ffn2(act, w_down, tile_expert, n_used, tm, tn):
    rows, k = act.shape
    d = w_down.shape[2]
    return pl.pallas_call(
        _moe_ffn2_kernel,
        grid_spec=pltpu.PrefetchScalarGridSpec(
            num_scalar_prefetch=2,
            grid=(d // tn, rows // tm),
            in_specs=[pl.BlockSpec((tm, k), lambda j, i, te, nu: (i, 0)),
                      pl.BlockSpec((1, k, tn), lambda j, i, te, nu: (te[i], 0, j))],
            out_specs=pl.BlockSpec((tm, tn), lambda j, i, te, nu: (i, j)),
            scratch_shapes=[pltpu.VMEM((k, tn), BF16)]),
        out_shape=jax.ShapeDtypeStruct((rows, d), F32),
        compiler_params=_cparams(2),
        name="moe_ffn2",
    )(tile_expert, n_used, act, w_down)


def _combine_kernel(pos_ref, ys_hbm, route_ref, res_ref, g_ref, b_ref, o_head, o_tail, buf_a, buf_b, sems,
                    *, n_head_tiles):
    i = pl.program_id(0)
    tm = res_ref.shape[0]
    n_sub = jnp.where(i < n_head_tiles, tm // DMA_SUB, o_tail.shape[0] // DMA_SUB)
    base2 = i * (TOP_K * tm)

    def issue(j, slot):
        p0 = base2 + j * (TOP_K * DMA_SUB)
        for r in range(DMA_SUB):
            _row_copy(ys_hbm, pos_ref[p0 + TOP_K * r], buf_a.at[slot], r, sems.at[0, slot]).start()
            _row_copy(ys_hbm, pos_ref[p0 + TOP_K * r + 1], buf_b.at[slot], r, sems.at[1, slot]).start()

    issue(0, 0)

    def body(j, c):
        slot = jnp.bitwise_and(j, 1)

        @pl.when(j + 1 < n_sub)
        def _():
            issue(j + 1, 1 - slot)

        _wait_rows(ys_hbm, buf_a.at[slot], sems.at[0, slot])
        _wait_rows(ys_hbm, buf_b.at[slot], sems.at[1, slot])
        rows = pl.ds(pl.multiple_of(j * DMA_SUB, DMA_SUB), DMA_SUB)
        route = route_ref[rows, :]
        y = route[:, 2:3] * buf_a[slot] + route[:, 3:4] * buf_b[slot]
        out = _layer_norm(ALPHA * res_ref[rows, :] + y, g_ref[...], b_ref[...])

        @pl.when(i < n_head_tiles)
        def _():
            o_head[rows, :] = out

        @pl.when(i >= n_head_tiles)
        def _():
            o_tail[rows, :] = out
        return c
    lax.fori_loop(0, n_sub, body, 0)


def _moe_combine(ys, pos, route, res, ln_g, ln_b, n_head):
    m, d = res.shape
    tm = TM_COMBINE
    head_tiles = n_head // tm
    n_tail = m - n_head
    assert n_head % tm == 0 and 0 < n_tail <= tm and n_tail % DMA_SUB == 0
    return pl.pallas_call(
        functools.partial(_combine_kernel, n_head_tiles=head_tiles),
        grid_spec=pltpu.PrefetchScalarGridSpec(
            num_scalar_prefetch=1,
            grid=(head_tiles + 1,),
            in_specs=[pl.BlockSpec(memory_space=pl.ANY),
                      pl.BlockSpec((tm, LANES), lambda i, p: (i, 0)),
                      pl.BlockSpec((tm, d), lambda i, p: (i, 0)),
                      pl.BlockSpec((1, d), lambda i, p: (0, 0)),
                      pl.BlockSpec((1, d), lambda i, p: (0, 0))],
            out_specs=[pl.BlockSpec((tm, d), lambda i, p: (jnp.minimum(i, head_tiles - 1), 0)),
                       pl.BlockSpec((n_tail, d), lambda i, p: (0, 0))],
            scratch_shapes=[pltpu.VMEM((2, DMA_SUB, d), F32), pltpu.VMEM((2, DMA_SUB, d), F32),
                            pltpu.SemaphoreType.DMA((2, 2))]),
        out_shape=[jax.ShapeDtypeStruct((n_head, d), F32),
                   jax.ShapeDtypeStruct((n_tail, d), F32)],
        compiler_params=_cparams(1),
        name="moe_combine",
    )(pos, ys, route, res, ln_g.reshape(1, d), ln_b.reshape(1, d))


def _dispatch_plan(route, n_experts, tm):
    n_tok = route.shape[0]
    ids = route[:, :TOP_K].astype(jnp.int32).reshape(-1)
    n_pairs = ids.shape[0]
    n_tiles = n_pairs // tm + n_experts
    onehot = (ids[:, None] == jnp.arange(n_experts, dtype=jnp.int32)[None, :]).astype(jnp.int32)
    rank = jnp.sum((jnp.cumsum(onehot, axis=0) - onehot) * onehot, axis=1)
    counts = jnp.sum(onehot, axis=0)
    tiles_e = (counts + tm - 1) // tm
    tile_end = jnp.cumsum(tiles_e)
    group_off = (tile_end - tiles_e) * tm
    pos = group_off[ids] + rank
    row_token = jnp.zeros((n_tiles * tm,), jnp.int32).at[pos].set(
        jnp.arange(n_pairs, dtype=jnp.int32) // TOP_K)
    n_used = tile_end[n_experts - 1:]
    tile_ids = jnp.arange(n_tiles, dtype=jnp.int32)
    tile_expert = jnp.sum((tile_ids[:, None] >= tile_end[None, :]).astype(jnp.int32), axis=1)
    last_expert = jnp.sum((n_used - 1 >= tile_end).astype(jnp.int32))
    tile_expert = jnp.minimum(tile_expert, last_expert).astype(jnp.int32)
    return row_token, pos.astype(jnp.int32), tile_expert, n_used.astype(jnp.int32)


def _moe_layer(h_all, n_head, w_router, w_gu, w_down, ln_g, ln_b):
    n_experts = w_router.shape[1]
    w_pad = jnp.pad(w_router, ((0, 0), (0, LANES - n_experts)))
    route = _router(h_all, w_pad, n_experts)
    row_token, pos, tile_expert, n_used = _dispatch_plan(route, n_experts, TM_EXPERT)
    xs = _gather_rows(h_all, row_token, n_used, TM_EXPERT)
    act = _moe_ffn1(xs, w_gu, tile_expert, n_used, TM_EXPERT, TN_EXPERT_UP)
    ys = _moe_ffn2(act, w_down, tile_expert, n_used, TM_EXPERT, TN_EXPERT_DOWN)
    return _moe_combine(ys, pos, route, h_all, ln_g, ln_b, n_head)


def kernel(x_prompt, x_sample, state_gdn_S, state_gdn_conv, state_sconv, ln_g, ln_b, gdn_w_in, gdn_conv_w, gdn_a_log, gdn_dt_bias, gdn_norm_w, gdn_w_out, sc_w_in, sc_conv_w, sc_w_out, ffn_w_gu, ffn_w_down, moe_w_router, moe_w_gu, moe_w_down):
    bsz, t, d = x_prompt.shape
    dec = x_sample.shape[0]
    heads = gdn_a_log.shape[1]
    dh = GDN_HEAD_DIM
    vw = heads * dh
    conv_ch = gdn_conv_w.shape[2]
    assert x_sample.shape[1] == 1 and conv_ch == 3 * vw and gdn_w_in.shape[2] == conv_ch + vw + 2 * heads
    assert t % GDN_TT == 0 and t % TM_PROMPT == 0 and (bsz * t + dec) % TM_TOKEN == 0
    assert dec % DMA_SUB == 0 and dec < TM_PROMPT
    n_p = bsz * t

    xp = x_prompt.reshape(n_p, d)
    xs = x_sample.reshape(dec, d)

    w_in = gdn_w_in[0]
    w_a = w_in[:, conv_ch + vw:conv_ch + vw + heads]
    w_b = w_in[:, conv_ch + vw + heads:]
    w_ab = jnp.pad(jnp.concatenate([w_a, w_b, w_a], axis=1), ((0, 0), (0, LANES - 3 * heads)))
    lane_pad = lambda p: jnp.pad(jnp.concatenate([p, p, p]), (0, LANES - 3 * heads)).reshape(1, LANES)
    alog_row = lane_pad(gdn_a_log[0])
    dtb_row = lane_pad(gdn_dt_bias[0])

    proj_p = _matmul(xp, w_in, conv_ch + vw, TM_PROMPT, 2048, F32, "gdn_in_prompt")
    proj_s = _matmul(xs, w_in, conv_ch + vw, dec, 1024, F32, "gdn_in_sample")
    gates_p = _gdn_gates(xp, w_ab, alog_row, dtb_row, heads, TM_PROMPT, True, "gdn_gates_prompt")
    gates_s = _gdn_gates(xs, w_ab, alog_row, dtb_row, heads, dec, False, "gdn_gates_sample")

    proj_p3 = proj_p.reshape(bsz, t, conv_ch + vw)
    conv_buf_t = jnp.transpose(state_gdn_conv[0], (1, 0, 2))
    qkvn_s = _gdn_prep_sample(proj_s, conv_buf_t, gdn_conv_w[0], heads)

    grow = gates_p[:, :heads].reshape(bsz, t // GDN_TT, GDN_TT, heads).transpose(0, 1, 3, 2)
    og_p, s_prompt = _gdn_chunked(proj_p3, gates_p.reshape(bsz, t, LANES), grow,
                                  gdn_conv_w[0], gdn_norm_w[0], heads)
    og_s, s_sample = _gdn_step(qkvn_s, proj_s, gates_s, state_gdn_S[0], gdn_norm_w[0], heads)

    conv_prompt = proj_p3[:, t - (gdn_conv_w.shape[1] - 1):, :conv_ch]
    conv_sample = jnp.concatenate([state_gdn_conv[0][:, 1:], proj_s[:, None, :conv_ch]], axis=1)

    h_p, hb_p = _matmul_ln(og_p.reshape(n_p, vw), gdn_w_out[0], xp, ln_g[0, 0], ln_b[0, 0], TM_PROMPT,
                           "gdn_out_prompt", also_bf16=True)
    h_s, hb_s = _matmul_ln(og_s, gdn_w_out[0], xs, ln_g[0, 0], ln_b[0, 0], dec, "gdn_out_sample", also_bf16=True)

    d_ff = ffn_w_down.shape[1]
    act_p = _matmul_swiglu(hb_p, ffn_w_gu[0], TM_DOWN, d_ff, "ffn_up_prompt")
    act_s = _matmul_swiglu(hb_s, ffn_w_gu[0], dec, d_ff // 2, "ffn_up_sample")
    h_p, hb_p = _matmul_ln(act_p, ffn_w_down[0], h_p, ln_g[0, 1], ln_b[0, 1], TM_DOWN, "ffn_down_prompt",
                           also_bf16=True)
    h_s, hb_s = _matmul_ln(act_s, ffn_w_down[0], h_s, ln_g[0, 1], ln_b[0, 1], dec, "ffn_down_sample",
                           also_bf16=True)

    sg_p, sconv_prompt = _sconv_prompt(hb_p, sc_w_in[0], sc_conv_w[0], bsz, TM_PROMPT, TN_SCONV)
    sproj_s = _matmul(hb_s, sc_w_in[0], 3 * d, dec, 1024, F32, "sc_in_sample")
    sbuf_t = jnp.transpose(state_sconv[0], (1, 0, 2))
    sg_s, ch_s = _sconv_sample(sproj_s, sbuf_t, sc_conv_w[0])
    sconv_sample = jnp.concatenate([state_sconv[0][:, 1:], ch_s[:, None, :]], axis=1)

    h_all = _matmul_ln(sg_p, sc_w_out[0], h_p, ln_g[1, 0], ln_b[1, 0], TM_PROMPT,
                       "sc_out", tail=(sg_s, h_s))

    y_p, y_s = _moe_layer(h_all, n_p, moe_w_router[0], moe_w_gu[0], moe_w_down[0], ln_g[1, 1], ln_b[1, 1])

    return (y_p.reshape(bsz, t, d), y_s.reshape(dec, 1, d), s_prompt[None], s_sample[None],
            conv_prompt[None], conv_sample[None], sconv_prompt[None], sconv_sample[None])
```

```python
import functools

import jax
import jax.numpy as jnp
from jax import lax
from jax.experimental import pallas as pl
from jax.experimental.pallas import tpu as pltpu

F32 = jnp.float32
BF16 = jnp.bfloat16

DEPTH = 2
ALPHA = (2.0 * DEPTH) ** 0.25
LN_EPS = 1e-5
NORM_EPS = 1e-6
GDN_HEAD_DIM = 128
GDN_CHUNK = 64
GDN_SUB = 16
TOP_K = 2

LANES = 128
SUBLANES = 8
VMEM_LIMIT = 56 * 1024 * 1024
NEG_BIG = -1e30

TM_PROMPT = 1024
TM_DOWN = 512
GDN_TT = 256
GDN_GROUP = 8
TM_EXPERT = 512
TN_EXPERT_UP = 1792
TN_EXPERT_DOWN = 1024
TM_TOKEN = 384
TM_COMBINE = 1024
DMA_SUB = 128
DMA_QUEUES = 2
DEC_BB = 8
MM_SLAB = 256
TN_SCONV = 512


def _cparams(n_axes, vmem=VMEM_LIMIT):
    return pltpu.CompilerParams(
        dimension_semantics=("arbitrary",) * n_axes, vmem_limit_bytes=vmem)


def _bdot(a, b):
    return jnp.dot(a.astype(BF16), b.astype(BF16), preferred_element_type=F32)


def _bdot_nt(a, b):
    return lax.dot_general(a.astype(BF16), b.astype(BF16),
                           (((1,), (1,)), ((), ())), preferred_element_type=F32)


def _bdot_tn(a, b):
    return lax.dot_general(a.astype(BF16), b.astype(BF16),
                           (((0,), (0,)), ((), ())), preferred_element_type=F32)


def _silu(x):
    return x * (0.5 * jnp.tanh(0.5 * x) + 0.5)


def _swiglu_slabs(x_ref, wg_bf, wu_bf, o_ref):
    slab = min(MM_SLAB, x_ref.shape[0])
    for r0 in range(0, x_ref.shape[0], slab):
        xb = x_ref[r0:r0 + slab, :].astype(BF16)
        g = jnp.dot(xb, wg_bf[...], preferred_element_type=F32)
        u = jnp.dot(xb, wu_bf[...], preferred_element_type=F32)
        o_ref[r0:r0 + slab, :] = (_silu(g) * u).astype(o_ref.dtype)


def _layer_norm(r, g, b):
    mu = jnp.mean(r, axis=-1, keepdims=True)
    d = r - mu
    var = jnp.mean(d * d, axis=-1, keepdims=True)
    return d * lax.rsqrt(var + LN_EPS) * g + b


def _mm_kernel(x_ref, w_ref, o_ref, wbf_ref):
    @pl.when(pl.program_id(1) == 0)
    def _():
        wbf_ref[...] = w_ref[...].astype(BF16)

    o_ref[...] = jnp.dot(x_ref[...].astype(BF16), wbf_ref[...],
                         preferred_element_type=F32).astype(o_ref.dtype)


def _matmul(x, w, n_cols, tm, tn, out_dtype, name):
    m, k = x.shape
    return pl.pallas_call(
        _mm_kernel,
        grid=(n_cols // tn, m // tm),
        in_specs=[pl.BlockSpec((tm, k), lambda j, i: (i, 0)),
                  pl.BlockSpec((k, tn), lambda j, i: (0, j))],
        out_specs=pl.BlockSpec((tm, tn), lambda j, i: (i, j)),
        out_shape=jax.ShapeDtypeStruct((m, n_cols), out_dtype),
        scratch_shapes=[pltpu.VMEM((k, tn), BF16)],
        compiler_params=_cparams(2),
        name=name,
    )(x, w)


def _mm_swiglu_kernel(x_ref, wg_ref, wu_ref, o_ref, wg_bf, wu_bf):
    @pl.when(pl.program_id(1) == 0)
    def _():
        wg_bf[...] = wg_ref[...].astype(BF16)
        wu_bf[...] = wu_ref[...].astype(BF16)

    _swiglu_slabs(x_ref, wg_bf, wu_bf, o_ref)


def _matmul_swiglu(x, w_gu, tm, tn, name):
    m, k = x.shape
    f = w_gu.shape[1] // 2
    nj = f // tn
    w_mode = dict(pipeline_mode=pl.Buffered(1)) if nj == 1 else {}
    return pl.pallas_call(
        _mm_swiglu_kernel,
        grid=(nj, m // tm),
        in_specs=[pl.BlockSpec((tm, k), lambda j, i: (i, 0)),
                  pl.BlockSpec((k, tn), lambda j, i: (0, j), **w_mode),
                  pl.BlockSpec((k, tn), lambda j, i: (0, j + nj), **w_mode)],
        out_specs=pl.BlockSpec((tm, tn), lambda j, i: (i, j)),
        out_shape=jax.ShapeDtypeStruct((m, f), BF16),
        scratch_shapes=[pltpu.VMEM((k, tn), BF16), pltpu.VMEM((k, tn), BF16)],
        compiler_params=_cparams(2),
        name=name,
    )(x, w_gu, w_gu)


def _mm_ln_kernel(x_ref, w_ref, res_ref, g_ref, b_ref, *rest, n_out, has_tail):
    tail_refs = rest[:2] if has_tail else ()
    out_refs = rest[len(tail_refs):len(tail_refs) + n_out]
    wbf_ref = rest[-1]
    i = pl.program_id(0)

    @pl.when(i == 0)
    def _():
        wbf_ref[...] = w_ref[...].astype(BF16)

    def emit(xr, rr):
        slab = min(MM_SLAB, xr.shape[0])
        for r0 in range(0, xr.shape[0], slab):
            rows = slice(r0, r0 + slab)
            y = jnp.dot(xr[rows, :].astype(BF16), wbf_ref[...], preferred_element_type=F32)
            h = _layer_norm(ALPHA * rr[rows, :] + y, g_ref[...], b_ref[...])
            for o_ref in out_refs:
                o_ref[rows, :] = h.astype(o_ref.dtype)

    if has_tail:
        n_main = pl.num_programs(0) - 1

        @pl.when(i < n_main)
        def _():
            emit(x_ref, res_ref)

        @pl.when(i == n_main)
        def _():
            emit(*tail_refs)
    else:
        emit(x_ref, res_ref)


def _matmul_ln(x, w, res, ln_g, ln_b, tm, name, also_bf16=False, tail=None):
    m, k = x.shape
    d = w.shape[1]
    n_main = m // tm
    main_idx = lambda i: (jnp.minimum(i, n_main - 1), 0)
    operands = [x, w, res, ln_g.reshape(1, d), ln_b.reshape(1, d)]
    in_specs = [pl.BlockSpec((tm, k), main_idx),
                pl.BlockSpec((k, d), lambda i: (0, 0)),
                pl.BlockSpec((tm, d), main_idx),
                pl.BlockSpec((1, d), lambda i: (0, 0)),
                pl.BlockSpec((1, d), lambda i: (0, 0))]
    m_out = m
    if tail is not None:
        m_tail = tail[0].shape[0]
        assert m_tail < tm and m_tail % (2 * SUBLANES) == 0
        operands += list(tail)
        in_specs += [pl.BlockSpec((m_tail, k), lambda i: (0, 0)), pl.BlockSpec((m_tail, d), lambda i: (0, 0))]
        m_out = m + m_tail
    out_shape = [jax.ShapeDtypeStruct((m_out, d), F32)]
    if also_bf16:
        out_shape.append(jax.ShapeDtypeStruct((m_out, d), BF16))
    out = pl.pallas_call(
        functools.partial(_mm_ln_kernel, n_out=len(out_shape), has_tail=tail is not None),
        grid=(n_main + int(tail is not None),),
        in_specs=in_specs,
        out_specs=[pl.BlockSpec((tm, d), lambda i: (i, 0)) for _ in out_shape],
        out_shape=out_shape,
        scratch_shapes=[pltpu.VMEM((k, d), BF16)],
        compiler_params=_cparams(1),
        name=name,
    )(*operands)
    return out if also_bf16 else out[0]


def _gates_kernel(x_ref, w_ref, alog_ref, dtb_ref, o_ref, *, heads, chunk_cumsum):
    a = _bdot(x_ref[...], w_ref[...])
    z = a + dtb_ref[...]
    softplus = jnp.maximum(z, 0.0) + jnp.log1p(jnp.exp(-jnp.abs(z)))
    g = -jnp.exp(alog_ref[...]) * softplus
    total = g
    if chunk_cumsum:
        n_rows = g.shape[0]
        row = jnp.bitwise_and(lax.broadcasted_iota(jnp.int32, g.shape, 0), GDN_CHUNK - 1)
        step = 1
        while step < GDN_CHUNK:
            g = g + jnp.where(row >= step, pltpu.roll(g, step, axis=0), 0.0)
            step *= 2
        total = g
        step = 1
        while step < GDN_CHUNK:
            total = jnp.where(row + step < GDN_CHUNK, pltpu.roll(total, n_rows - step, axis=0), total)
            step *= 2
    lane = lax.broadcasted_iota(jnp.int32, g.shape, 1)
    o_ref[...] = jnp.where(lane < heads, g, jnp.where(lane < 2 * heads, jax.nn.sigmoid(a), total))


def _gdn_gates(x, w_ab, alog_row, dtb_row, heads, tm, chunk_cumsum, name):
    m, k = x.shape
    return pl.pallas_call(
        functools.partial(_gates_kernel, heads=heads, chunk_cumsum=chunk_cumsum),
        grid=(m // tm,),
        in_specs=[pl.BlockSpec((tm, k), lambda i: (i, 0)),
                  pl.BlockSpec((k, LANES), lambda i: (0, 0)),
                  pl.BlockSpec((1, LANES), lambda i: (0, 0)),
                  pl.BlockSpec((1, LANES), lambda i: (0, 0))],
        out_specs=pl.BlockSpec((tm, LANES), lambda i: (i, 0)),
        out_shape=jax.ShapeDtypeStruct((m, LANES), F32),
        compiler_params=_cparams(1),
        name=name,
    )(x, w_ab, alog_row, dtb_row)


def _qkv_finish(conv, j, heads):
    y = _silu(conv)
    inv = lax.rsqrt(jnp.sum(y * y, axis=-1, keepdims=True) + NORM_EPS)
    scale = jnp.where(j < heads, inv * (GDN_HEAD_DIM ** -0.5),
                      jnp.where(j < 2 * heads, inv, 1.0))
    return y * scale


def _gdn_prep_sample_kernel(x_ref, buf_ref, w_ref, o_ref, *, heads):
    w = w_ref[...]
    width = w.shape[0]
    conv = w[width - 1:width, :] * x_ref[...]
    for s in range(width - 1):
        conv = conv + w[s:s + 1, :] * buf_ref[s]
    o_ref[...] = _qkv_finish(conv, pl.program_id(0), heads)


def _gdn_prep_sample(proj, buf_t, conv_w, heads):
    bsz = proj.shape[0]
    width, ch = conv_w.shape
    return pl.pallas_call(
        functools.partial(_gdn_prep_sample_kernel, heads=heads),
        grid=(ch // GDN_HEAD_DIM,),
        in_specs=[pl.BlockSpec((bsz, GDN_HEAD_DIM), lambda j: (0, j)),
                  pl.BlockSpec((width - 1, bsz, GDN_HEAD_DIM), lambda j: (0, 0, j)),
                  pl.BlockSpec((width, GDN_HEAD_DIM), lambda j: (0, j))],
        out_specs=pl.BlockSpec((bsz, GDN_HEAD_DIM), lambda j: (0, j)),
        out_shape=jax.ShapeDtypeStruct((bsz, ch), F32),
        compiler_params=_cparams(1),
        name="gdn_prep_sample",
    )(proj, buf_t, conv_w)


def _gated_rmsnorm(o, z, norm_w):
    on = o * lax.rsqrt(jnp.mean(o * o, axis=-1, keepdims=True) + NORM_EPS) * norm_w
    return on * _silu(z)


def _gdn_in_kernel(x_ref, w_ref, cw_ref, o_ref, last_ref, wbf_ref, halo_ref, *, kinds, tiles_per_seq):
    i = pl.program_id(0)
    dh = GDN_HEAD_DIM
    width = cw_ref.shape[0]
    tm = x_ref.shape[0]
    slab = min(MM_SLAB, tm)

    @pl.when(i == 0)
    def _():
        wbf_ref[...] = w_ref[...].astype(BF16)

    @pl.when(lax.rem(i, tiles_per_seq) == 0)
    def _():
        halo_ref[...] = jnp.zeros_like(halo_ref)

    halo_row = lax.broadcasted_iota(jnp.int32, (SUBLANES, dh), 0)
    for r0 in range(0, tm, slab):
        rows = slice(r0, r0 + slab)
        proj = jnp.dot(x_ref[rows, :].astype(BF16), wbf_ref[...], preferred_element_type=F32)
        for b, kind in enumerate(kinds):
            lanes = slice(b * dh, (b + 1) * dh)
            x = proj[:, lanes]
            if kind == "z":
                o_ref[rows, lanes] = x
                continue
            halo = halo_ref[:, lanes]
            w = cw_ref[:, lanes]
            acc = w[width - 1:width, :] * x
            for s in range(1, width):
                xr = pltpu.roll(x, s, axis=0)
                top = jnp.where(halo_row < s, pltpu.roll(halo, s, axis=0), xr[:SUBLANES, :])
                acc = acc + w[width - 1 - s:width - s, :] * jnp.concatenate([top, xr[SUBLANES:, :]], axis=0)
            halo_ref[:, lanes] = x[slab - SUBLANES:, :]
            y = _silu(acc)
            if kind in "qk":
                scale = dh ** -0.5 if kind == "q" else 1.0
                y = y * (lax.rsqrt(jnp.sum(y * y, axis=-1, keepdims=True) + NORM_EPS) * scale)
            o_ref[rows, lanes] = y
            if r0 + slab == tm:
                last_ref[0, :, lanes] = x[slab - (width - 1):, :]


def _gdn_in(x, w_in, conv_w, col0, kinds, bsz, tm, name):
    m, k = x.shape
    width = conv_w.shape[0]
    tn = GDN_HEAD_DIM * len(kinds)
    n_conv = GDN_HEAD_DIM * sum(kind != "z" for kind in kinds)
    assert col0 % tn == 0 and all(kind != "z" for kind in kinds[:n_conv // GDN_HEAD_DIM])
    tiles_per_seq = (m // bsz) // tm
    return pl.pallas_call(
        functools.partial(_gdn_in_kernel, kinds=kinds, tiles_per_seq=tiles_per_seq),
        grid=(m // tm,),
        in_specs=[pl.BlockSpec((tm, k), lambda i: (i, 0)),
                  pl.BlockSpec((k, tn), lambda i: (0, col0 // tn), pipeline_mode=pl.Buffered(1)),
                  pl.BlockSpec((width, n_conv), lambda i: (0, col0 // n_conv))],
        out_specs=[pl.BlockSpec((tm, tn), lambda i: (i, 0)),
                   pl.BlockSpec((1, width - 1, n_conv), lambda i: (i // tiles_per_seq, 0, 0))],
        out_shape=[jax.ShapeDtypeStruct((m, tn), F32),
                   jax.ShapeDtypeStruct((bsz, width - 1, n_conv), F32)],
        scratch_shapes=[pltpu.VMEM((k, tn), BF16), pltpu.VMEM((SUBLANES, n_conv), F32)],
        compiler_params=_cparams(1),
        name=name,
    )(x, w_in, conv_w)


def _gdn_tile_kernel(q_ref, k_ref, v_ref, z_ref, gates_ref, grow_ref, nw_ref,
                     og_ref, sout_ref, s_ref, *, heads):
    c_len = GDN_CHUNK
    dh = GDN_HEAD_DIM
    tt = q_ref.shape[1]
    n_c = tt // c_len
    tile = pl.program_id(1)

    @pl.when(tile == 0)
    def _():
        s_ref[...] = jnp.zeros_like(s_ref)

    row = lax.broadcasted_iota(jnp.int32, (tt, tt), 0)
    col = lax.broadcasted_iota(jnp.int32, (tt, tt), 1)
    same_chunk = jnp.bitwise_and(row, -c_len) == jnp.bitwise_and(col, -c_len)
    tril = jnp.logical_and(same_chunk, row >= col)
    strict = jnp.logical_and(same_chunk, row > col)
    same_sub = jnp.bitwise_and(row, -GDN_SUB) == jnp.bitwise_and(col, -GDN_SUB)
    gate_tile = gates_ref[0]
    grow_tile = grow_ref[0, 0]
    norm_w = nw_ref[...]

    for g0 in range(0, heads, GDN_GROUP):
        hs = range(g0, g0 + GDN_GROUP)
        q = [q_ref[0, :, h * dh:(h + 1) * dh] for h in hs]
        k = [k_ref[0, :, h * dh:(h + 1) * dh] for h in hs]
        v = [v_ref[0, :, h * dh:(h + 1) * dh] for h in hs]
        gcol = [gate_tile[:, h:h + 1] for h in hs]
        beta = [gate_tile[:, heads + h:heads + h + 1] for h in hs]
        glast = [gate_tile[:, 2 * heads + h:2 * heads + h + 1] for h in hs]
        decay = [jnp.exp(jnp.where(tril, gc - grow_tile[h:h + 1, :], NEG_BIG)) for h, gc in zip(hs, gcol)]
        kb = [ki * bi for ki, bi in zip(k, beta)]
        a_mat = [jnp.where(strict, _bdot_nt(kbi, ki) * di, 0.0) for kbi, ki, di in zip(kb, k, decay)]
        qk = [jnp.where(tril, _bdot_nt(qi, ki) * di, 0.0) for qi, ki, di in zip(q, k, decay)]

        p = [jnp.where(same_sub, ai, 0.0) for ai in a_mat]
        e_mat = [ai - pi for ai, pi in zip(a_mat, p)]
        x = [-pi for pi in p]
        span = 1
        while 2 * span < GDN_SUB:
            p = [_bdot(pi, pi) for pi in p]
            xp = [_bdot(xi, pi) for xi, pi in zip(x, p)]
            x = [xi + pi + xpi for xi, pi, xpi in zip(x, p, xp)]
            span *= 2
        rhs = [jnp.concatenate([vi * bi, kbi * jnp.exp(gc)], axis=-1)
               for vi, bi, kbi, gc in zip(v, beta, kb, gcol)]
        n_mat = [ei + _bdot(xi, ei) for ei, xi in zip(e_mat, x)]
        r = [ri + _bdot(xi, ri) for ri, xi in zip(rhs, x)]
        r = [ri - _bdot(ni, ri) for ri, ni in zip(r, n_mat)]
        span = 2
        while span < c_len // GDN_SUB:
            n_mat = [_bdot(ni, ni) for ni in n_mat]
            r = [ri + _bdot(ni, ri) for ri, ni in zip(r, n_mat)]
            span *= 2

        o_uw = [_bdot(qki, ri) for qki, ri in zip(qk, r)]
        q_eff = [qi * jnp.exp(gc) - oi[:, dh:] for qi, gc, oi in zip(q, gcol, o_uw)]
        k_dec = [ki * jnp.exp(gl - gc) for ki, gl, gc in zip(k, glast, gcol)]
        s = [s_ref[h] for h in hs]
        for c in range(n_c):
            rows = slice(c * c_len, (c + 1) * c_len)
            kr = [_bdot_tn(kd[rows, :], ri[rows, :]) for kd, ri in zip(k_dec, r)]
            o = [oi[rows, :dh] + _bdot(qe[rows, :], si) for oi, qe, si in zip(o_uw, q_eff, s)]
            for h, oi in zip(hs, o):
                z = z_ref[0, rows, h * dh:(h + 1) * dh]
                og_ref[0, rows, h * dh:(h + 1) * dh] = _gated_rmsnorm(oi, z, norm_w).astype(og_ref.dtype)
            a_c = [jnp.exp(gl[(c + 1) * c_len - 1:(c + 1) * c_len, :]) for gl in glast]
            s = [si * ai + kri[:, :dh] - _bdot(kri[:, dh:], si) for si, ai, kri in zip(s, a_c, kr)]
        for h, si in zip(hs, s):
            s_ref[h] = si

    @pl.when(tile == pl.num_programs(1) - 1)
    def _():
        sout_ref[0] = s_ref[...]


def _gdn_chunked(qk, vz, gates, grow, norm_w, heads):
    bsz, t, _ = qk.shape
    dh = GDN_HEAD_DIM
    vw = heads * dh
    tt = GDN_TT
    blk = lambda part: pl.BlockSpec((1, tt, vw), lambda b, i: (b, i, part))
    return pl.pallas_call(
        functools.partial(_gdn_tile_kernel, heads=heads),
        grid=(bsz, t // tt),
        in_specs=[blk(0), blk(1), blk(0), blk(1),
                  pl.BlockSpec((1, tt, LANES), lambda b, i: (b, i, 0)),
                  pl.BlockSpec((1, 1, heads, tt), lambda b, i: (b, i, 0, 0)),
                  pl.BlockSpec((1, dh), lambda b, i: (0, 0))],
        out_specs=[pl.BlockSpec((1, tt, vw), lambda b, i: (b, i, 0)),
                   pl.BlockSpec((1, heads, dh, dh), lambda b, i: (b, 0, 0, 0))],
        out_shape=[jax.ShapeDtypeStruct((bsz, t, vw), BF16),
                   jax.ShapeDtypeStruct((bsz, heads, dh, dh), F32)],
        scratch_shapes=[pltpu.VMEM((heads, dh, dh), F32)],
        compiler_params=_cparams(2),
        name="gdn_chunked",
    )(qk, qk, vz, vz, gates, grow, norm_w.reshape(1, dh))


def _gdn_step_kernel(qkv_ref, z_ref, gates_ref, s0_ref, nw_ref, og_ref, sout_ref, *, heads):
    dh = GDN_HEAD_DIM
    pad = 2 * SUBLANES
    norm_w = nw_ref[...]
    prow = lax.broadcasted_iota(jnp.int32, (pad, dh), 0)

    pairs = [(bi, h) for bi in range(qkv_ref.shape[0]) for h in range(heads)]
    rows = {bi: slice(bi, bi + 1) for bi, _ in pairs}
    q = [qkv_ref[rows[bi], h * dh:(h + 1) * dh] for bi, h in pairs]
    k = [qkv_ref[rows[bi], (heads + h) * dh:(heads + h + 1) * dh] for bi, h in pairs]
    a = [jnp.exp(gates_ref[rows[bi], h:h + 1]) for bi, h in pairs]
    s_kq = [_bdot(jnp.where(prow == 0, ki, jnp.where(prow == 1, qi, 0.0)), s0_ref[bi, h])
            for (bi, h), ki, qi in zip(pairs, k, q)]
    k_parts, d_parts = [], []
    for (bi, h), qi, ki, ai, si in zip(pairs, q, k, a, s_kq):
        v = qkv_ref[rows[bi], (2 * heads + h) * dh:(2 * heads + h + 1) * dh]
        beta = gates_ref[rows[bi], heads + h:heads + h + 1]
        delta = (v - ai * si[0:1, :]) * beta
        o = ai * si[1:2, :] + jnp.sum(ki * qi, axis=-1, keepdims=True) * delta
        og_ref[rows[bi], h * dh:(h + 1) * dh] = _gated_rmsnorm(o, z_ref[rows[bi], h * dh:(h + 1) * dh], norm_w)
        k_hi = ki.astype(BF16).astype(F32)
        d_hi = delta.astype(BF16).astype(F32)
        k_parts.append(jnp.where(prow < 2, k_hi, jnp.where(prow == 2, ki - k_hi, 0.0)))
        d_parts.append(jnp.where(prow == 1, delta - d_hi, jnp.where(prow < 3, d_hi, 0.0)))
    for (bi, h), ai, kp, dp in zip(pairs, a, k_parts, d_parts):
        sout_ref[bi, h] = s0_ref[bi, h] * ai + _bdot_tn(kp, dp)


def _gdn_step(qkvn, proj, gates, s0, norm_w, heads):
    bsz = qkvn.shape[0]
    dh = GDN_HEAD_DIM
    vw = heads * dh
    bb = DEC_BB
    return pl.pallas_call(
        functools.partial(_gdn_step_kernel, heads=heads),
        grid=(bsz // bb,),
        in_specs=[pl.BlockSpec((bb, 3 * vw), lambda i: (i, 0)),
                  pl.BlockSpec((bb, vw), lambda i: (i, 3)),
                  pl.BlockSpec((bb, LANES), lambda i: (i, 0)),
                  pl.BlockSpec((bb, heads, dh, dh), lambda i: (i, 0, 0, 0)),
                  pl.BlockSpec((1, dh), lambda i: (0, 0))],
        out_specs=[pl.BlockSpec((bb, vw), lambda i: (i, 0)),
                   pl.BlockSpec((bb, heads, dh, dh), lambda i: (i, 0, 0, 0))],
        out_shape=[jax.ShapeDtypeStruct((bsz, vw), F32),
                   jax.ShapeDtypeStruct((bsz, heads, dh, dh), F32)],
        compiler_params=_cparams(1),
        name="gdn_step",
    )(qkvn, proj, gates, s0, norm_w.reshape(1, dh))


def _sconv_prompt_kernel(x_ref, wb_ref, wc_ref, wh_ref, cw_ref, o_ref, last_ref,
                         wb_bf, wc_bf, wh_bf, halo_ref, *, tiles_per_seq):
    i = pl.program_id(1)

    @pl.when(i == 0)
    def _():
        wb_bf[...] = wb_ref[...].astype(BF16)
        wc_bf[...] = wc_ref[...].astype(BF16)
        wh_bf[...] = wh_ref[...].astype(BF16)

    @pl.when(lax.rem(i, tiles_per_seq) == 0)
    def _():
        halo_ref[...] = jnp.zeros_like(halo_ref)

    xb = x_ref[...].astype(BF16)
    gate = jnp.dot(xb, wb_bf[...], preferred_element_type=F32)
    ch = (jnp.dot(xb, wc_bf[...], preferred_element_type=F32)
          * jnp.dot(xb, wh_bf[...], preferred_element_type=F32))
    w = cw_ref[...]
    width = w.shape[0]
    tm = ch.shape[0]
    halo = halo_ref[...]
    halo_row = lax.broadcasted_iota(jnp.int32, halo.shape, 0)
    conv = w[width - 1:width, :] * ch
    for s in range(1, width):
        xr = pltpu.roll(ch, s, axis=0)
        top = jnp.where(halo_row < s, pltpu.roll(halo, s, axis=0), xr[:SUBLANES, :])
        conv = conv + w[width - 1 - s:width - s, :] * jnp.concatenate([top, xr[SUBLANES:, :]], axis=0)
    halo_ref[...] = ch[tm - SUBLANES:, :]
    o_ref[...] = (gate * conv).astype(o_ref.dtype)
    last_ref[0] = ch[tm - (width - 1):, :]


def _sconv_prompt(x, w_in, conv_w, bsz, tm, tn):
    m, k = x.shape
    width, d = conv_w.shape
    t = m // bsz
    tiles_per_seq = t // tm
    nb = d // tn
    wblk = lambda off: pl.BlockSpec((k, tn), lambda j, i: (0, j + off))
    return pl.pallas_call(
        functools.partial(_sconv_prompt_kernel, tiles_per_seq=tiles_per_seq),
        grid=(nb, m // tm),
        in_specs=[pl.BlockSpec((tm, k), lambda j, i: (i, 0)),
                  wblk(0), wblk(nb), wblk(2 * nb),
                  pl.BlockSpec((width, tn), lambda j, i: (0, j))],
        out_specs=[pl.BlockSpec((tm, tn), lambda j, i: (i, j)),
                   pl.BlockSpec((1, width - 1, tn), lambda j, i: (i // tiles_per_seq, 0, j))],
        out_shape=[jax.ShapeDtypeStruct((m, d), BF16),
                   jax.ShapeDtypeStruct((bsz, width - 1, d), F32)],
        scratch_shapes=[pltpu.VMEM((k, tn), BF16), pltpu.VMEM((k, tn), BF16), pltpu.VMEM((k, tn), BF16),
                        pltpu.VMEM((SUBLANES, tn), F32)],
        compiler_params=_cparams(2),
        name="sconv_prompt",
    )(x, w_in, w_in, w_in, conv_w)


def _sconv_sample_kernel(b_ref, c_ref, h_ref, buf_ref, w_ref, o_ref, ch_ref):
    ch = c_ref[...] * h_ref[...]
    w = w_ref[...]
    width = w.shape[0]
    conv = w[width - 1:width, :] * ch
    for s in range(width - 1):
        conv = conv + w[s:s + 1, :] * buf_ref[s]
    o_ref[...] = b_ref[...] * conv
    ch_ref[...] = ch


def _sconv_sample(proj, buf_t, conv_w):
    bsz = proj.shape[0]
    width, d = conv_w.shape
    nb = d // LANES
    blk = lambda off: pl.BlockSpec((bsz, LANES), lambda j: (0, j + off))
    return pl.pallas_call(
        _sconv_sample_kernel,
        grid=(nb,),
        in_specs=[blk(0), blk(nb), blk(2 * nb),
                  pl.BlockSpec((width - 1, bsz, LANES), lambda j: (0, 0, j)),
                  pl.BlockSpec((width, LANES), lambda j: (0, j))],
        out_specs=[pl.BlockSpec((bsz, LANES), lambda j: (0, j)),
                   pl.BlockSpec((bsz, LANES), lambda j: (0, j))],
        out_shape=[jax.ShapeDtypeStruct((bsz, d), F32),
                   jax.ShapeDtypeStruct((bsz, d), F32)],
        compiler_params=_cparams(1),
        name="sconv_sample",
    )(proj, proj, proj, buf_t, conv_w)


def _router_kernel(x_ref, w_ref, o_ref, *, n_experts):
    x = x_ref[...]
    w = w_ref[...]
    x_hi = x.astype(BF16)
    w_hi = w.astype(BF16)
    x_lo = x - x_hi.astype(F32)
    w_lo = w - w_hi.astype(F32)
    logits = (jnp.dot(x_hi, w_hi, preferred_element_type=F32) + _bdot(x_hi, w_lo) + _bdot(x_lo, w_hi))
    lane = lax.broadcasted_iota(jnp.int32, logits.shape, 1).astype(F32)
    lg = jnp.where(lane < n_experts, logits, NEG_BIG)
    m1 = jnp.max(lg, axis=-1, keepdims=True)
    i1 = jnp.min(jnp.where(lg == m1, lane, float(LANES)), axis=-1, keepdims=True)
    lg2 = jnp.where(lane == i1, NEG_BIG, lg)
    m2 = jnp.max(lg2, axis=-1, keepdims=True)
    i2 = jnp.min(jnp.where(lg2 == m2, lane, float(LANES)), axis=-1, keepdims=True)
    e = jnp.exp(m2 - m1)
    g1 = 1.0 / (1.0 + e)
    g2 = e / (1.0 + e)
    o_ref[...] = jnp.where(lane == 0, i1, jnp.where(lane == 1, i2,
                           jnp.where(lane == 2, g1, jnp.where(lane == 3, g2, 0.0))))


def _router(x, w_pad, n_experts):
    m, k = x.shape
    tm = TM_TOKEN
    return pl.pallas_call(
        functools.partial(_router_kernel, n_experts=n_experts),
        grid=(m // tm,),
        in_specs=[pl.BlockSpec((tm, k), lambda i: (i, 0)),
                  pl.BlockSpec((k, LANES), lambda i: (0, 0))],
        out_specs=pl.BlockSpec((tm, LANES), lambda i: (i, 0)),
        out_shape=jax.ShapeDtypeStruct((m, LANES), F32),
        compiler_params=_cparams(1),
        name="moe_router",
    )(x, w_pad)


def _row_copy(src_hbm, src_row, dst_ref, dst_row, sem):
    return pltpu.make_async_copy(src_hbm.at[pl.ds(src_row, 1)], dst_ref.at[pl.ds(dst_row, 1)], sem)


def _wait_rows(src_hbm, dst_ref, sem):
    pltpu.make_async_copy(src_hbm.at[pl.ds(0, dst_ref.shape[0])], dst_ref, sem).wait()


def _gather_kernel(idx_ref, nu_ref, src_hbm, o_ref, buf, sems):
    i = pl.program_id(0)
    tg = o_ref.shape[0]
    n_sub = tg // DMA_SUB
    base = i * tg

    def issue(j, slot):
        for r in range(DMA_SUB):
            _row_copy(src_hbm, idx_ref[base + j * DMA_SUB + r], buf.at[slot], r,
                      sems.at[slot]).start(priority=r % DMA_QUEUES)

    @pl.when(i < nu_ref[0])
    def _():
        issue(0, 0)

        def body(j, c):
            slot = jnp.bitwise_and(j, 1)

            @pl.when(j + 1 < n_sub)
            def _():
                issue(j + 1, 1 - slot)

            _wait_rows(src_hbm, buf.at[slot], sems.at[slot])
            o_ref[pl.ds(pl.multiple_of(j * DMA_SUB, DMA_SUB), DMA_SUB), :] = buf[slot].astype(o_ref.dtype)
            return c
        lax.fori_loop(0, n_sub, body, 0)

    @pl.when(i >= nu_ref[0])
    def _():
        o_ref[...] = jnp.zeros_like(o_ref)


def _gather_rows(src, row_token, n_used, tg):
    rows = row_token.shape[0]
    d = src.shape[1]
    return pl.pallas_call(
        _gather_kernel,
        grid_spec=pltpu.PrefetchScalarGridSpec(
            num_scalar_prefetch=2,
            grid=(rows // tg,),
            in_specs=[pl.BlockSpec(memory_space=pl.ANY)],
            out_specs=pl.BlockSpec((tg, d), lambda i, idx, nu: (i, 0)),
            scratch_shapes=[pltpu.VMEM((2, DMA_SUB, d), src.dtype), pltpu.SemaphoreType.DMA((2,))]),
        out_shape=jax.ShapeDtypeStruct((rows, d), BF16),
        compiler_params=_cparams(1),
        name="moe_gather",
    )(row_token, n_used, src)


def _moe_ffn1_kernel(te_ref, nu_ref, x_ref, wg_ref, wu_ref, o_ref, wg_bf, wu_bf):
    i = pl.program_id(1)
    new_expert = jnp.logical_or(i == 0, te_ref[i] != te_ref[jnp.maximum(i - 1, 0)])

    @pl.when(new_expert)
    def _():
        wg_bf[...] = wg_ref[0].astype(BF16)
        wu_bf[...] = wu_ref[0].astype(BF16)

    @pl.when(i < nu_ref[0])
    def _():
        _swiglu_slabs(x_ref, wg_bf, wu_bf, o_ref)

    @pl.when(i >= nu_ref[0])
    def _():
        o_ref[...] = jnp.zeros_like(o_ref)


def _moe_ffn1(xs, w_gu, tile_expert, n_used, tm, tn):
    rows, k = xs.shape
    f = w_gu.shape[2] // 2
    nj = f // tn
    return pl.pallas_call(
        _moe_ffn1_kernel,
        grid_spec=pltpu.PrefetchScalarGridSpec(
            num_scalar_prefetch=2,
            grid=(nj, rows // tm),
            in_specs=[pl.BlockSpec((tm, k), lambda j, i, te, nu: (i, 0)),
                      pl.BlockSpec((1, k, tn), lambda j, i, te, nu: (te[i], 0, j)),
                      pl.BlockSpec((1, k, tn), lambda j, i, te, nu: (te[i], 0, j + nj))],
            out_specs=pl.BlockSpec((tm, tn), lambda j, i, te, nu: (i, j)),
            scratch_shapes=[pltpu.VMEM((k, tn), BF16), pltpu.VMEM((k, tn), BF16)]),
        out_shape=jax.ShapeDtypeStruct((rows, f), BF16),
        compiler_params=_cparams(2),
        name="moe_ffn1",
    )(tile_expert, n_used, xs, w_gu, w_gu)


def _moe_ffn2_kernel(te_ref, nu_ref, x_ref, w_ref, o_ref, w_bf):
    i = pl.program_id(1)
    new_expert = jnp.logical_or(i == 0, te_ref[i] != te_ref[jnp.maximum(i - 1, 0)])

    @pl.when(new_expert)
    def _():
        w_bf[...] = w_ref[0].astype(BF16)

    @pl.when(i < nu_ref[0])
    def _():
        o_ref[...] = jnp.dot(x_ref[...], w_bf[...], preferred_element_type=F32)

    @pl.when(i >= nu_ref[0])
    def _():
        o_ref[...] = jnp.zeros_like(o_ref)


def _moe_ffn2(act, w_down, tile_expert, n_used, tm, tn):
    rows, k = act.shape
    d = w_down.shape[2]
    return pl.pallas_call(
        _moe_ffn2_kernel,
        grid_spec=pltpu.PrefetchScalarGridSpec(
            num_scalar_prefetch=2,
            grid=(d // tn, rows // tm),
            in_specs=[pl.BlockSpec((tm, k), lambda j, i, te, nu: (i, 0)),
                      pl.BlockSpec((1, k, tn), lambda j, i, te, nu: (te[i], 0, j))],
            out_specs=pl.BlockSpec((tm, tn), lambda j, i, te, nu: (i, j)),
            scratch_shapes=[pltpu.VMEM((k, tn), BF16)]),
        out_shape=jax.ShapeDtypeStruct((rows, d), F32),
        compiler_params=_cparams(2),
        name="moe_ffn2",
    )(tile_expert, n_used, act, w_down)


def _combine_kernel(pos_ref, ys_hbm, route_ref, res_ref, g_ref, b_ref, o_head, o_tail, buf_a, buf_b, sems,
                    *, n_head_tiles):
    i = pl.program_id(0)
    tm = res_ref.shape[0]
    n_sub = jnp.where(i < n_head_tiles, tm // DMA_SUB, o_tail.shape[0] // DMA_SUB)
    base2 = i * (TOP_K * tm)

    def issue(j, slot):
        p0 = base2 + j * (TOP_K * DMA_SUB)
        for r in range(DMA_SUB):
            _row_copy(ys_hbm, pos_ref[p0 + TOP_K * r], buf_a.at[slot], r, sems.at[0, slot]).start(priority=0)
            _row_copy(ys_hbm, pos_ref[p0 + TOP_K * r + 1], buf_b.at[slot], r, sems.at[1, slot]).start(priority=1)

    issue(0, 0)

    def body(j, c):
        slot = jnp.bitwise_and(j, 1)

        @pl.when(j + 1 < n_sub)
        def _():
            issue(j + 1, 1 - slot)

        _wait_rows(ys_hbm, buf_a.at[slot], sems.at[0, slot])
        _wait_rows(ys_hbm, buf_b.at[slot], sems.at[1, slot])
        rows = pl.ds(pl.multiple_of(j * DMA_SUB, DMA_SUB), DMA_SUB)
        route = route_ref[rows, :]
        y = route[:, 2:3] * buf_a[slot] + route[:, 3:4] * buf_b[slot]
        out = _layer_norm(ALPHA * res_ref[rows, :] + y, g_ref[...], b_ref[...])

        @pl.when(i < n_head_tiles)
        def _():
            o_head[rows, :] = out

        @pl.when(i >= n_head_tiles)
        def _():
            o_tail[rows, :] = out
        return c
    lax.fori_loop(0, n_sub, body, 0)


def _moe_combine(ys, pos, route, res, ln_g, ln_b, n_head):
    m, d = res.shape
    tm = TM_COMBINE
    head_tiles = n_head // tm
    n_tail = m - n_head
    assert n_head % tm == 0 and 0 < n_tail <= tm and n_tail % DMA_SUB == 0
    return pl.pallas_call(
        functools.partial(_combine_kernel, n_head_tiles=head_tiles),
        grid_spec=pltpu.PrefetchScalarGridSpec(
            num_scalar_prefetch=1,
            grid=(head_tiles + 1,),
            in_specs=[pl.BlockSpec(memory_space=pl.ANY),
                      pl.BlockSpec((tm, LANES), lambda i, p: (i, 0)),
                      pl.BlockSpec((tm, d), lambda i, p: (i, 0)),
                      pl.BlockSpec((1, d), lambda i, p: (0, 0)),
                      pl.BlockSpec((1, d), lambda i, p: (0, 0))],
            out_specs=[pl.BlockSpec((tm, d), lambda i, p: (jnp.minimum(i, head_tiles - 1), 0)),
                       pl.BlockSpec((n_tail, d), lambda i, p: (0, 0))],
            scratch_shapes=[pltpu.VMEM((2, DMA_SUB, d), F32), pltpu.VMEM((2, DMA_SUB, d), F32),
                            pltpu.SemaphoreType.DMA((2, 2))]),
        out_shape=[jax.ShapeDtypeStruct((n_head, d), F32),
                   jax.ShapeDtypeStruct((n_tail, d), F32)],
        compiler_params=_cparams(1),
        name="moe_combine",
    )(pos, ys, route, res, ln_g.reshape(1, d), ln_b.reshape(1, d))


def _dispatch_plan(route, n_experts, tm):
    n_tok = route.shape[0]
    ids = route[:, :TOP_K].astype(jnp.int32).reshape(-1)
    n_pairs = ids.shape[0]
    n_tiles = n_pairs // tm + n_experts
    onehot = (ids[:, None] == jnp.arange(n_experts, dtype=jnp.int32)[None, :]).astype(jnp.int32)
    rank = jnp.sum((jnp.cumsum(onehot, axis=0) - onehot) * onehot, axis=1)
    counts = jnp.sum(onehot, axis=0)
    tiles_e = (counts + tm - 1) // tm
    tile_end = jnp.cumsum(tiles_e)
    group_off = (tile_end - tiles_e) * tm
    pos = group_off[ids] + rank
    row_token = jnp.zeros((n_tiles * tm,), jnp.int32).at[pos].set(
        jnp.arange(n_pairs, dtype=jnp.int32) // TOP_K)
    n_used = tile_end[n_experts - 1:]
    tile_ids = jnp.arange(n_tiles, dtype=jnp.int32)
    tile_expert = jnp.sum((tile_ids[:, None] >= tile_end[None, :]).astype(jnp.int32), axis=1)
    last_expert = jnp.sum((n_used - 1 >= tile_end).astype(jnp.int32))
    tile_expert = jnp.minimum(tile_expert, last_expert).astype(jnp.int32)
    return row_token, pos.astype(jnp.int32), tile_expert, n_used.astype(jnp.int32)


def _moe_layer(h_all, n_head, w_router, w_gu, w_down, ln_g, ln_b):
    n_experts = w_router.shape[1]
    w_pad = jnp.pad(w_router, ((0, 0), (0, LANES - n_experts)))
    route = _router(h_all, w_pad, n_experts)
    row_token, pos, tile_expert, n_used = _dispatch_plan(route, n_experts, TM_EXPERT)
    xs = _gather_rows(h_all, row_token, n_used, TM_EXPERT)
    act = _moe_ffn1(xs, w_gu, tile_expert, n_used, TM_EXPERT, TN_EXPERT_UP)
    ys = _moe_ffn2(act, w_down, tile_expert, n_used, TM_EXPERT, TN_EXPERT_DOWN)
    return _moe_combine(ys, pos, route, h_all, ln_g, ln_b, n_head)


def kernel(x_prompt, x_sample, state_gdn_S, state_gdn_conv, state_sconv, ln_g, ln_b, gdn_w_in, gdn_conv_w, gdn_a_log, gdn_dt_bias, gdn_norm_w, gdn_w_out, sc_w_in, sc_conv_w, sc_w_out, ffn_w_gu, ffn_w_down, moe_w_router, moe_w_gu, moe_w_down):
    bsz, t, d = x_prompt.shape
    dec = x_sample.shape[0]
    heads = gdn_a_log.shape[1]
    dh = GDN_HEAD_DIM
    vw = heads * dh
    conv_ch = gdn_conv_w.shape[2]
    assert x_sample.shape[1] == 1 and conv_ch == 3 * vw and gdn_w_in.shape[2] == conv_ch + vw + 2 * heads
    assert t % GDN_TT == 0 and t % TM_PROMPT == 0 and (bsz * t + dec) % TM_TOKEN == 0
    assert dec % DMA_SUB == 0 and dec < TM_PROMPT
    n_p = bsz * t

    xp = x_prompt.reshape(n_p, d)
    xs = x_sample.reshape(dec, d)

    w_in = gdn_w_in[0]
    w_a = w_in[:, conv_ch + vw:conv_ch + vw + heads]
    w_b = w_in[:, conv_ch + vw + heads:]
    w_ab = jnp.pad(jnp.concatenate([w_a, w_b, w_a], axis=1), ((0, 0), (0, LANES - 3 * heads)))
    lane_pad = lambda p: jnp.pad(jnp.concatenate([p, p, p]), (0, LANES - 3 * heads)).reshape(1, LANES)
    alog_row = lane_pad(gdn_a_log[0])
    dtb_row = lane_pad(gdn_dt_bias[0])

    qk_p, qk_last = _gdn_in(xp, w_in, gdn_conv_w[0], 0, ("q",) * heads + ("k",) * heads, bsz, TM_PROMPT,
                            "gdn_in_qk")
    vz_p, v_last = _gdn_in(xp, w_in, gdn_conv_w[0], 2 * vw, ("v",) * heads + ("z",) * heads, bsz, TM_PROMPT,
                           "gdn_in_vz")
    proj_s = _matmul(xs, w_in, conv_ch + vw, dec, 1024, F32, "gdn_in_sample")
    gates_p = _gdn_gates(xp, w_ab, alog_row, dtb_row, heads, TM_PROMPT, True, "gdn_gates_prompt")
    gates_s = _gdn_gates(xs, w_ab, alog_row, dtb_row, heads, dec, False, "gdn_gates_sample")

    conv_buf_t = jnp.transpose(state_gdn_conv[0], (1, 0, 2))
    qkvn_s = _gdn_prep_sample(proj_s, conv_buf_t, gdn_conv_w[0], heads)

    grow = gates_p[:, :heads].reshape(bsz, t // GDN_TT, GDN_TT, heads).transpose(0, 1, 3, 2)
    og_p, s_prompt = _gdn_chunked(qk_p.reshape(bsz, t, 2 * vw), vz_p.reshape(bsz, t, 2 * vw),
                                  gates_p.reshape(bsz, t, LANES), grow, gdn_norm_w[0], heads)
    og_s, s_sample = _gdn_step(qkvn_s, proj_s, gates_s, state_gdn_S[0], gdn_norm_w[0], heads)

    conv_prompt = jnp.concatenate([qk_last, v_last], axis=-1)
    conv_sample = jnp.concatenate([state_gdn_conv[0][:, 1:], proj_s[:, None, :conv_ch]], axis=1)

    h_p, hb_p = _matmul_ln(og_p.reshape(n_p, vw), gdn_w_out[0], xp, ln_g[0, 0], ln_b[0, 0], TM_PROMPT,
                           "gdn_out_prompt", also_bf16=True)
    h_s, hb_s = _matmul_ln(og_s, gdn_w_out[0], xs, ln_g[0, 0], ln_b[0, 0], dec, "gdn_out_sample", also_bf16=True)

    d_ff = ffn_w_down.shape[1]
    act_p = _matmul_swiglu(hb_p, ffn_w_gu[0], TM_DOWN, d_ff, "ffn_up_prompt")
    act_s = _matmul_swiglu(hb_s, ffn_w_gu[0], dec, d_ff // 2, "ffn_up_sample")
    h_p, hb_p = _matmul_ln(act_p, ffn_w_down[0], h_p, ln_g[0, 1], ln_b[0, 1], TM_DOWN, "ffn_down_prompt",
                           also_bf16=True)
    h_s, hb_s = _matmul_ln(act_s, ffn_w_down[0], h_s, ln_g[0, 1], ln_b[0, 1], dec, "ffn_down_sample",
                           also_bf16=True)

    sg_p, sconv_prompt = _sconv_prompt(hb_p, sc_w_in[0], sc_conv_w[0], bsz, TM_PROMPT, TN_SCONV)
    sproj_s = _matmul(hb_s, sc_w_in[0], 3 * d, dec, 1024, F32, "sc_in_sample")
    sbuf_t = jnp.transpose(state_sconv[0], (1, 0, 2))
    sg_s, ch_s = _sconv_sample(sproj_s, sbuf_t, sc_conv_w[0])
    sconv_sample = jnp.concatenate([state_sconv[0][:, 1:], ch_s[:, None, :]], axis=1)

    h_all = _matmul_ln(sg_p, sc_w_out[0], h_p, ln_g[1, 0], ln_b[1, 0], TM_PROMPT,
                       "sc_out", tail=(sg_s, h_s))

    y_p, y_s = _moe_layer(h_all, n_p, moe_w_router[0], moe_w_gu[0], moe_w_down[0], ln_g[1, 1], ln_b[1, 1])

    return (y_p.reshape(bsz, t, d), y_s.reshape(dec, 1, d), s_prompt[None], s_sample[None],
            conv_prompt[None], conv_sample[None], sconv_prompt[None], sconv_sample[None])
```

```python
import functools

import jax
import jax.numpy as jnp
from jax import lax
from jax.experimental import pallas as pl
from jax.experimental.pallas import tpu as pltpu

F32 = jnp.float32
BF16 = jnp.bfloat16

DEPTH = 2
ALPHA = (2.0 * DEPTH) ** 0.25
LN_EPS = 1e-5
NORM_EPS = 1e-6
GDN_HEAD_DIM = 128
GDN_CHUNK = 64
GDN_SUB = 16
TOP_K = 2

LANES = 128
SUBLANES = 8
VMEM_LIMIT = 56 * 1024 * 1024
NEG_BIG = -1e30

TM_PROMPT = 1024
TM_DOWN = 512
GDN_TT = 256
GDN_GROUP = 8
TM_EXPERT = 512
TN_EXPERT_UP = 1792
TN_EXPERT_DOWN = 1024
TM_TOKEN = 384
TM_COMBINE = 1024
DMA_SUB = 128
DEC_BB = 8
MM_SLAB = 256
TN_SCONV = 512


def _cparams(n_axes, vmem=VMEM_LIMIT):
    return pltpu.CompilerParams(
        dimension_semantics=("arbitrary",) * n_axes, vmem_limit_bytes=vmem)


def _bdot(a, b):
    return jnp.dot(a.astype(BF16), b.astype(BF16), preferred_element_type=F32)


def _bdot_nt(a, b):
    return lax.dot_general(a.astype(BF16), b.astype(BF16),
                           (((1,), (1,)), ((), ())), preferred_element_type=F32)


def _bdot_tn(a, b):
    return lax.dot_general(a.astype(BF16), b.astype(BF16),
                           (((0,), (0,)), ((), ())), preferred_element_type=F32)


def _silu(x):
    return x * (0.5 * jnp.tanh(0.5 * x) + 0.5)


def _swiglu_slabs(x_ref, wg_bf, wu_bf, o_ref):
    slab = min(MM_SLAB, x_ref.shape[0])
    for r0 in range(0, x_ref.shape[0], slab):
        xb = x_ref[r0:r0 + slab, :].astype(BF16)
        g = jnp.dot(xb, wg_bf[...], preferred_element_type=F32)
        u = jnp.dot(xb, wu_bf[...], preferred_element_type=F32)
        o_ref[r0:r0 + slab, :] = (_silu(g) * u).astype(o_ref.dtype)


def _layer_norm(r, g, b):
    mu = jnp.mean(r, axis=-1, keepdims=True)
    d = r - mu
    var = jnp.mean(d * d, axis=-1, keepdims=True)
    return d * lax.rsqrt(var + LN_EPS) * g + b


def _mm_kernel(x_ref, w_ref, o_ref, wbf_ref):
    @pl.when(pl.program_id(1) == 0)
    def _():
        wbf_ref[...] = w_ref[...].astype(BF16)

    o_ref[...] = jnp.dot(x_ref[...].astype(BF16), wbf_ref[...],
                         preferred_element_type=F32).astype(o_ref.dtype)


def _matmul(x, w, n_cols, tm, tn, out_dtype, name):
    m, k = x.shape
    return pl.pallas_call(
        _mm_kernel,
        grid=(n_cols // tn, m // tm),
        in_specs=[pl.BlockSpec((tm, k), lambda j, i: (i, 0)),
                  pl.BlockSpec((k, tn), lambda j, i: (0, j))],
        out_specs=pl.BlockSpec((tm, tn), lambda j, i: (i, j)),
        out_shape=jax.ShapeDtypeStruct((m, n_cols), out_dtype),
        scratch_shapes=[pltpu.VMEM((k, tn), BF16)],
        compiler_params=_cparams(2),
        name=name,
    )(x, w)


def _mm_swiglu_kernel(x_ref, wg_ref, wu_ref, o_ref, wg_bf, wu_bf):
    @pl.when(pl.program_id(1) == 0)
    def _():
        wg_bf[...] = wg_ref[...].astype(BF16)
        wu_bf[...] = wu_ref[...].astype(BF16)

    _swiglu_slabs(x_ref, wg_bf, wu_bf, o_ref)


def _matmul_swiglu(x, w_gu, tm, tn, name):
    m, k = x.shape
    f = w_gu.shape[1] // 2
    nj = f // tn
    w_mode = dict(pipeline_mode=pl.Buffered(1)) if nj == 1 else {}
    return pl.pallas_call(
        _mm_swiglu_kernel,
        grid=(nj, m // tm),
        in_specs=[pl.BlockSpec((tm, k), lambda j, i: (i, 0)),
                  pl.BlockSpec((k, tn), lambda j, i: (0, j), **w_mode),
                  pl.BlockSpec((k, tn), lambda j, i: (0, j + nj), **w_mode)],
        out_specs=pl.BlockSpec((tm, tn), lambda j, i: (i, j)),
        out_shape=jax.ShapeDtypeStruct((m, f), BF16),
        scratch_shapes=[pltpu.VMEM((k, tn), BF16), pltpu.VMEM((k, tn), BF16)],
        compiler_params=_cparams(2),
        name=name,
    )(x, w_gu, w_gu)


def _mm_ln_kernel(x_ref, w_ref, res_ref, g_ref, b_ref, *rest, n_out, has_tail):
    tail_refs = rest[:2] if has_tail else ()
    out_refs = rest[len(tail_refs):len(tail_refs) + n_out]
    wbf_ref = rest[-1]
    i = pl.program_id(0)

    @pl.when(i == 0)
    def _():
        wbf_ref[...] = w_ref[...].astype(BF16)

    def emit(xr, rr):
        slab = min(MM_SLAB, xr.shape[0])
        for r0 in range(0, xr.shape[0], slab):
            rows = slice(r0, r0 + slab)
            y = jnp.dot(xr[rows, :].astype(BF16), wbf_ref[...], preferred_element_type=F32)
            h = _layer_norm(ALPHA * rr[rows, :] + y, g_ref[...], b_ref[...])
            for o_ref in out_refs:
                o_ref[rows, :] = h.astype(o_ref.dtype)

    if has_tail:
        n_main = pl.num_programs(0) - 1

        @pl.when(i < n_main)
        def _():
            emit(x_ref, res_ref)

        @pl.when(i == n_main)
        def _():
            emit(*tail_refs)
    else:
        emit(x_ref, res_ref)


def _matmul_ln(x, w, res, ln_g, ln_b, tm, name, also_bf16=False, tail=None):
    m, k = x.shape
    d = w.shape[1]
    n_main = m // tm
    main_idx = lambda i: (jnp.minimum(i, n_main - 1), 0)
    operands = [x, w, res, ln_g.reshape(1, d), ln_b.reshape(1, d)]
    in_specs = [pl.BlockSpec((tm, k), main_idx),
                pl.BlockSpec((k, d), lambda i: (0, 0)),
                pl.BlockSpec((tm, d), main_idx),
                pl.BlockSpec((1, d), lambda i: (0, 0)),
                pl.BlockSpec((1, d), lambda i: (0, 0))]
    m_out = m
    if tail is not None:
        m_tail = tail[0].shape[0]
        assert m_tail < tm and m_tail % (2 * SUBLANES) == 0
        operands += list(tail)
        in_specs += [pl.BlockSpec((m_tail, k), lambda i: (0, 0)), pl.BlockSpec((m_tail, d), lambda i: (0, 0))]
        m_out = m + m_tail
    out_shape = [jax.ShapeDtypeStruct((m_out, d), F32)]
    if also_bf16:
        out_shape.append(jax.ShapeDtypeStruct((m_out, d), BF16))
    out = pl.pallas_call(
        functools.partial(_mm_ln_kernel, n_out=len(out_shape), has_tail=tail is not None),
        grid=(n_main + int(tail is not None),),
        in_specs=in_specs,
        out_specs=[pl.BlockSpec((tm, d), lambda i: (i, 0)) for _ in out_shape],
        out_shape=out_shape,
        scratch_shapes=[pltpu.VMEM((k, d), BF16)],
        compiler_params=_cparams(1),
        name=name,
    )(*operands)
    return out if also_bf16 else out[0]


def _gates_kernel(x_ref, w_ref, alog_ref, dtb_ref, o_ref, *, heads, chunk_cumsum):
    a = _bdot(x_ref[...], w_ref[...])
    z = a + dtb_ref[...]
    softplus = jnp.maximum(z, 0.0) + jnp.log1p(jnp.exp(-jnp.abs(z)))
    g = -jnp.exp(alog_ref[...]) * softplus
    total = g
    if chunk_cumsum:
        n_rows = g.shape[0]
        row = jnp.bitwise_and(lax.broadcasted_iota(jnp.int32, g.shape, 0), GDN_CHUNK - 1)
        step = 1
        while step < GDN_CHUNK:
            g = g + jnp.where(row >= step, pltpu.roll(g, step, axis=0), 0.0)
            step *= 2
        total = g
        step = 1
        while step < GDN_CHUNK:
            total = jnp.where(row + step < GDN_CHUNK, pltpu.roll(total, n_rows - step, axis=0), total)
            step *= 2
    lane = lax.broadcasted_iota(jnp.int32, g.shape, 1)
    o_ref[...] = jnp.where(lane < heads, g, jnp.where(lane < 2 * heads, jax.nn.sigmoid(a), total))


def _gdn_gates(x, w_ab, alog_row, dtb_row, heads, tm, chunk_cumsum, name):
    m, k = x.shape
    return pl.pallas_call(
        functools.partial(_gates_kernel, heads=heads, chunk_cumsum=chunk_cumsum),
        grid=(m // tm,),
        in_specs=[pl.BlockSpec((tm, k), lambda i: (i, 0)),
                  pl.BlockSpec((k, LANES), lambda i: (0, 0)),
                  pl.BlockSpec((1, LANES), lambda i: (0, 0)),
                  pl.BlockSpec((1, LANES), lambda i: (0, 0))],
        out_specs=pl.BlockSpec((tm, LANES), lambda i: (i, 0)),
        out_shape=jax.ShapeDtypeStruct((m, LANES), F32),
        compiler_params=_cparams(1),
        name=name,
    )(x, w_ab, alog_row, dtb_row)


def _qkv_finish(conv, j, heads):
    y = _silu(conv)
    inv = lax.rsqrt(jnp.sum(y * y, axis=-1, keepdims=True) + NORM_EPS)
    scale = jnp.where(j < heads, inv * (GDN_HEAD_DIM ** -0.5),
                      jnp.where(j < 2 * heads, inv, 1.0))
    return y * scale


def _gdn_prep_sample_kernel(x_ref, buf_ref, w_ref, o_ref, *, heads):
    w = w_ref[...]
    width = w.shape[0]
    conv = w[width - 1:width, :] * x_ref[...]
    for s in range(width - 1):
        conv = conv + w[s:s + 1, :] * buf_ref[s]
    o_ref[...] = _qkv_finish(conv, pl.program_id(0), heads)


def _gdn_prep_sample(proj, buf_t, conv_w, heads):
    bsz = proj.shape[0]
    width, ch = conv_w.shape
    return pl.pallas_call(
        functools.partial(_gdn_prep_sample_kernel, heads=heads),
        grid=(ch // GDN_HEAD_DIM,),
        in_specs=[pl.BlockSpec((bsz, GDN_HEAD_DIM), lambda j: (0, j)),
                  pl.BlockSpec((width - 1, bsz, GDN_HEAD_DIM), lambda j: (0, 0, j)),
                  pl.BlockSpec((width, GDN_HEAD_DIM), lambda j: (0, j))],
        out_specs=pl.BlockSpec((bsz, GDN_HEAD_DIM), lambda j: (0, j)),
        out_shape=jax.ShapeDtypeStruct((bsz, ch), F32),
        compiler_params=_cparams(1),
        name="gdn_prep_sample",
    )(proj, buf_t, conv_w)


def _gated_rmsnorm(o, z, norm_w):
    on = o * lax.rsqrt(jnp.mean(o * o, axis=-1, keepdims=True) + NORM_EPS) * norm_w
    return on * _silu(z)


def _gdn_in_kernel(x_ref, w_ref, cw_ref, o_ref, last_ref, wbf_ref, halo_ref, *, kinds, tiles_per_seq):
    i = pl.program_id(0)
    dh = GDN_HEAD_DIM
    width = cw_ref.shape[0]
    tm = x_ref.shape[0]
    slab = min(MM_SLAB, tm)

    @pl.when(i == 0)
    def _():
        wbf_ref[...] = w_ref[...].astype(BF16)

    @pl.when(lax.rem(i, tiles_per_seq) == 0)
    def _():
        halo_ref[...] = jnp.zeros_like(halo_ref)

    halo_row = lax.broadcasted_iota(jnp.int32, (SUBLANES, dh), 0)
    for r0 in range(0, tm, slab):
        rows = slice(r0, r0 + slab)
        proj = jnp.dot(x_ref[rows, :].astype(BF16), wbf_ref[...], preferred_element_type=F32)
        for b, kind in enumerate(kinds):
            lanes = slice(b * dh, (b + 1) * dh)
            x = proj[:, lanes]
            if kind == "z":
                o_ref[rows, lanes] = x
                continue
            halo = halo_ref[:, lanes]
            w = cw_ref[:, lanes]
            acc = w[width - 1:width, :] * x
            for s in range(1, width):
                xr = pltpu.roll(x, s, axis=0)
                top = jnp.where(halo_row < s, pltpu.roll(halo, s, axis=0), xr[:SUBLANES, :])
                acc = acc + w[width - 1 - s:width - s, :] * jnp.concatenate([top, xr[SUBLANES:, :]], axis=0)
            halo_ref[:, lanes] = x[slab - SUBLANES:, :]
            y = _silu(acc)
            if kind in "qk":
                scale = dh ** -0.5 if kind == "q" else 1.0
                y = y * (lax.rsqrt(jnp.sum(y * y, axis=-1, keepdims=True) + NORM_EPS) * scale)
            o_ref[rows, lanes] = y
            if r0 + slab == tm:
                last_ref[0, :, lanes] = x[slab - (width - 1):, :]


def _gdn_in(x, w_in, conv_w, col0, kinds, bsz, tm, name):
    m, k = x.shape
    width = conv_w.shape[0]
    tn = GDN_HEAD_DIM * len(kinds)
    n_conv = GDN_HEAD_DIM * sum(kind != "z" for kind in kinds)
    assert col0 % tn == 0 and all(kind != "z" for kind in kinds[:n_conv // GDN_HEAD_DIM])
    tiles_per_seq = (m // bsz) // tm
    return pl.pallas_call(
        functools.partial(_gdn_in_kernel, kinds=kinds, tiles_per_seq=tiles_per_seq),
        grid=(m // tm,),
        in_specs=[pl.BlockSpec((tm, k), lambda i: (i, 0)),
                  pl.BlockSpec((k, tn), lambda i: (0, col0 // tn), pipeline_mode=pl.Buffered(1)),
                  pl.BlockSpec((width, n_conv), lambda i: (0, col0 // n_conv))],
        out_specs=[pl.BlockSpec((tm, tn), lambda i: (i, 0)),
                   pl.BlockSpec((1, width - 1, n_conv), lambda i: (i // tiles_per_seq, 0, 0))],
        out_shape=[jax.ShapeDtypeStruct((m, tn), F32),
                   jax.ShapeDtypeStruct((bsz, width - 1, n_conv), F32)],
        scratch_shapes=[pltpu.VMEM((k, tn), BF16), pltpu.VMEM((SUBLANES, n_conv), F32)],
        compiler_params=_cparams(1),
        name=name,
    )(x, w_in, conv_w)


def _gdn_tile_kernel(q_ref, k_ref, v_ref, z_ref, gates_ref, grow_ref, nw_ref,
                     og_ref, sout_ref, s_ref, *, heads):
    c_len = GDN_CHUNK
    dh = GDN_HEAD_DIM
    tt = q_ref.shape[1]
    n_c = tt // c_len
    tile = pl.program_id(1)

    @pl.when(tile == 0)
    def _():
        s_ref[...] = jnp.zeros_like(s_ref)

    row = lax.broadcasted_iota(jnp.int32, (tt, tt), 0)
    col = lax.broadcasted_iota(jnp.int32, (tt, tt), 1)
    same_chunk = jnp.bitwise_and(row, -c_len) == jnp.bitwise_and(col, -c_len)
    tril = jnp.logical_and(same_chunk, row >= col)
    strict = jnp.logical_and(same_chunk, row > col)
    same_sub = jnp.bitwise_and(row, -GDN_SUB) == jnp.bitwise_and(col, -GDN_SUB)
    gate_tile = gates_ref[0]
    grow_tile = grow_ref[0, 0]
    norm_w = nw_ref[...]

    for g0 in range(0, heads, GDN_GROUP):
        hs = range(g0, g0 + GDN_GROUP)
        q = [q_ref[0, :, h * dh:(h + 1) * dh] for h in hs]
        k = [k_ref[0, :, h * dh:(h + 1) * dh] for h in hs]
        v = [v_ref[0, :, h * dh:(h + 1) * dh] for h in hs]
        gcol = [gate_tile[:, h:h + 1] for h in hs]
        beta = [gate_tile[:, heads + h:heads + h + 1] for h in hs]
        glast = [gate_tile[:, 2 * heads + h:2 * heads + h + 1] for h in hs]
        decay = [jnp.exp(jnp.where(tril, gc - grow_tile[h:h + 1, :], NEG_BIG)) for h, gc in zip(hs, gcol)]
        kb = [ki * bi for ki, bi in zip(k, beta)]
        a_mat = [jnp.where(strict, _bdot_nt(kbi, ki) * di, 0.0) for kbi, ki, di in zip(kb, k, decay)]
        qk = [jnp.where(tril, _bdot_nt(qi, ki) * di, 0.0) for qi, ki, di in zip(q, k, decay)]

        p = [jnp.where(same_sub, ai, 0.0) for ai in a_mat]
        e_mat = [ai - pi for ai, pi in zip(a_mat, p)]
        x = [-pi for pi in p]
        span = 1
        while 2 * span < GDN_SUB:
            p = [_bdot(pi, pi) for pi in p]
            xp = [_bdot(xi, pi) for xi, pi in zip(x, p)]
            x = [xi + pi + xpi for xi, pi, xpi in zip(x, p, xp)]
            span *= 2
        rhs = [jnp.concatenate([vi * bi, kbi * jnp.exp(gc)], axis=-1)
               for vi, bi, kbi, gc in zip(v, beta, kb, gcol)]
        n_mat = [ei + _bdot(xi, ei) for ei, xi in zip(e_mat, x)]
        r = [ri + _bdot(xi, ri) for ri, xi in zip(rhs, x)]
        r = [ri - _bdot(ni, ri) for ri, ni in zip(r, n_mat)]
        span = 2
        while span < c_len // GDN_SUB:
            n_mat = [_bdot(ni, ni) for ni in n_mat]
            r = [ri + _bdot(ni, ri) for ri, ni in zip(r, n_mat)]
            span *= 2

        o_uw = [_bdot(qki, ri) for qki, ri in zip(qk, r)]
        q_eff = [qi * jnp.exp(gc) - oi[:, dh:] for qi, gc, oi in zip(q, gcol, o_uw)]
        k_dec = [ki * jnp.exp(gl - gc) for ki, gl, gc in zip(k, glast, gcol)]
        s = [s_ref[h] for h in hs]
        for c in range(n_c):
            rows = slice(c * c_len, (c + 1) * c_len)
            kr = [_bdot_tn(kd[rows, :], ri[rows, :]) for kd, ri in zip(k_dec, r)]
            o = [oi[rows, :dh] + _bdot(qe[rows, :], si) for oi, qe, si in zip(o_uw, q_eff, s)]
            for h, oi in zip(hs, o):
                z = z_ref[0, rows, h * dh:(h + 1) * dh]
                og_ref[0, rows, h * dh:(h + 1) * dh] = _gated_rmsnorm(oi, z, norm_w).astype(og_ref.dtype)
            a_c = [jnp.exp(gl[(c + 1) * c_len - 1:(c + 1) * c_len, :]) for gl in glast]
            s = [si * ai + kri[:, :dh] - _bdot(kri[:, dh:], si) for si, ai, kri in zip(s, a_c, kr)]
        for h, si in zip(hs, s):
            s_ref[h] = si

    @pl.when(tile == pl.num_programs(1) - 1)
    def _():
        sout_ref[0] = s_ref[...]


def _gdn_chunked(qk, vz, gates, grow, norm_w, heads):
    bsz, t, _ = qk.shape
    dh = GDN_HEAD_DIM
    vw = heads * dh
    tt = GDN_TT
    blk = lambda part: pl.BlockSpec((1, tt, vw), lambda b, i: (b, i, part))
    return pl.pallas_call(
        functools.partial(_gdn_tile_kernel, heads=heads),
        grid=(bsz, t // tt),
        in_specs=[blk(0), blk(1), blk(0), blk(1),
                  pl.BlockSpec((1, tt, LANES), lambda b, i: (b, i, 0)),
                  pl.BlockSpec((1, 1, heads, tt), lambda b, i: (b, i, 0, 0)),
                  pl.BlockSpec((1, dh), lambda b, i: (0, 0))],
        out_specs=[pl.BlockSpec((1, tt, vw), lambda b, i: (b, i, 0)),
                   pl.BlockSpec((1, heads, dh, dh), lambda b, i: (b, 0, 0, 0))],
        out_shape=[jax.ShapeDtypeStruct((bsz, t, vw), BF16),
                   jax.ShapeDtypeStruct((bsz, heads, dh, dh), F32)],
        scratch_shapes=[pltpu.VMEM((heads, dh, dh), F32)],
        compiler_params=_cparams(2),
        name="gdn_chunked",
    )(qk, qk, vz, vz, gates, grow, norm_w.reshape(1, dh))


def _gdn_step_kernel(qkv_ref, z_ref, gates_ref, s0_ref, nw_ref, og_ref, sout_ref, *, heads):
    dh = GDN_HEAD_DIM
    pad = 2 * SUBLANES
    norm_w = nw_ref[...]
    prow = lax.broadcasted_iota(jnp.int32, (pad, dh), 0)

    pairs = [(bi, h) for bi in range(qkv_ref.shape[0]) for h in range(heads)]
    rows = {bi: slice(bi, bi + 1) for bi, _ in pairs}
    q = [qkv_ref[rows[bi], h * dh:(h + 1) * dh] for bi, h in pairs]
    k = [qkv_ref[rows[bi], (heads + h) * dh:(heads + h + 1) * dh] for bi, h in pairs]
    a = [jnp.exp(gates_ref[rows[bi], h:h + 1]) for bi, h in pairs]
    s_kq = [_bdot(jnp.where(prow == 0, ki, jnp.where(prow == 1, qi, 0.0)), s0_ref[bi, h])
            for (bi, h), ki, qi in zip(pairs, k, q)]
    k_parts, d_parts = [], []
    for (bi, h), qi, ki, ai, si in zip(pairs, q, k, a, s_kq):
        v = qkv_ref[rows[bi], (2 * heads + h) * dh:(2 * heads + h + 1) * dh]
        beta = gates_ref[rows[bi], heads + h:heads + h + 1]
        delta = (v - ai * si[0:1, :]) * beta
        o = ai * si[1:2, :] + jnp.sum(ki * qi, axis=-1, keepdims=True) * delta
        og_ref[rows[bi], h * dh:(h + 1) * dh] = _gated_rmsnorm(o, z_ref[rows[bi], h * dh:(h + 1) * dh], norm_w)
        k_hi = ki.astype(BF16).astype(F32)
        d_hi = delta.astype(BF16).astype(F32)
        k_parts.append(jnp.where(prow < 2, k_hi, jnp.where(prow == 2, ki - k_hi, 0.0)))
        d_parts.append(jnp.where(prow == 1, delta - d_hi, jnp.where(prow < 3, d_hi, 0.0)))
    for (bi, h), ai, kp, dp in zip(pairs, a, k_parts, d_parts):
        sout_ref[bi, h] = s0_ref[bi, h] * ai + _bdot_tn(kp, dp)


def _gdn_step(qkvn, proj, gates, s0, norm_w, heads):
    bsz = qkvn.shape[0]
    dh = GDN_HEAD_DIM
    vw = heads * dh
    bb = DEC_BB
    return pl.pallas_call(
        functools.partial(_gdn_step_kernel, heads=heads),
        grid=(bsz // bb,),
        in_specs=[pl.BlockSpec((bb, 3 * vw), lambda i: (i, 0)),
                  pl.BlockSpec((bb, vw), lambda i: (i, 3)),
                  pl.BlockSpec((bb, LANES), lambda i: (i, 0)),
                  pl.BlockSpec((bb, heads, dh, dh), lambda i: (i, 0, 0, 0)),
                  pl.BlockSpec((1, dh), lambda i: (0, 0))],
        out_specs=[pl.BlockSpec((bb, vw), lambda i: (i, 0)),
                   pl.BlockSpec((bb, heads, dh, dh), lambda i: (i, 0, 0, 0))],
        out_shape=[jax.ShapeDtypeStruct((bsz, vw), F32),
                   jax.ShapeDtypeStruct((bsz, heads, dh, dh), F32)],
        compiler_params=_cparams(1),
        name="gdn_step",
    )(qkvn, proj, gates, s0, norm_w.reshape(1, dh))


def _sconv_prompt_kernel(x_ref, wb_ref, wc_ref, wh_ref, cw_ref, o_ref, last_ref,
                         wb_bf, wc_bf, wh_bf, halo_ref, *, tiles_per_seq):
    i = pl.program_id(1)

    @pl.when(i == 0)
    def _():
        wb_bf[...] = wb_ref[...].astype(BF16)
        wc_bf[...] = wc_ref[...].astype(BF16)
        wh_bf[...] = wh_ref[...].astype(BF16)

    @pl.when(lax.rem(i, tiles_per_seq) == 0)
    def _():
        halo_ref[...] = jnp.zeros_like(halo_ref)

    xb = x_ref[...].astype(BF16)
    gate = jnp.dot(xb, wb_bf[...], preferred_element_type=F32)
    ch = (jnp.dot(xb, wc_bf[...], preferred_element_type=F32)
          * jnp.dot(xb, wh_bf[...], preferred_element_type=F32))
    w = cw_ref[...]
    width = w.shape[0]
    tm = ch.shape[0]
    halo = halo_ref[...]
    halo_row = lax.broadcasted_iota(jnp.int32, halo.shape, 0)
    conv = w[width - 1:width, :] * ch
    for s in range(1, width):
        xr = pltpu.roll(ch, s, axis=0)
        top = jnp.where(halo_row < s, pltpu.roll(halo, s, axis=0), xr[:SUBLANES, :])
        conv = conv + w[width - 1 - s:width - s, :] * jnp.concatenate([top, xr[SUBLANES:, :]], axis=0)
    halo_ref[...] = ch[tm - SUBLANES:, :]
    o_ref[...] = (gate * conv).astype(o_ref.dtype)
    last_ref[0] = ch[tm - (width - 1):, :]


def _sconv_prompt(x, w_in, conv_w, bsz, tm, tn):
    m, k = x.shape
    width, d = conv_w.shape
    t = m // bsz
    tiles_per_seq = t // tm
    nb = d // tn
    wblk = lambda off: pl.BlockSpec((k, tn), lambda j, i: (0, j + off))
    return pl.pallas_call(
        functools.partial(_sconv_prompt_kernel, tiles_per_seq=tiles_per_seq),
        grid=(nb, m // tm),
        in_specs=[pl.BlockSpec((tm, k), lambda j, i: (i, 0)),
                  wblk(0), wblk(nb), wblk(2 * nb),
                  pl.BlockSpec((width, tn), lambda j, i: (0, j))],
        out_specs=[pl.BlockSpec((tm, tn), lambda j, i: (i, j)),
                   pl.BlockSpec((1, width - 1, tn), lambda j, i: (i // tiles_per_seq, 0, j))],
        out_shape=[jax.ShapeDtypeStruct((m, d), BF16),
                   jax.ShapeDtypeStruct((bsz, width - 1, d), F32)],
        scratch_shapes=[pltpu.VMEM((k, tn), BF16), pltpu.VMEM((k, tn), BF16), pltpu.VMEM((k, tn), BF16),
                        pltpu.VMEM((SUBLANES, tn), F32)],
        compiler_params=_cparams(2),
        name="sconv_prompt",
    )(x, w_in, w_in, w_in, conv_w)


def _sconv_sample_kernel(b_ref, c_ref, h_ref, buf_ref, w_ref, o_ref, ch_ref):
    ch = c_ref[...] * h_ref[...]
    w = w_ref[...]
    width = w.shape[0]
    conv = w[width - 1:width, :] * ch
    for s in range(width - 1):
        conv = conv + w[s:s + 1, :] * buf_ref[s]
    o_ref[...] = b_ref[...] * conv
    ch_ref[...] = ch


def _sconv_sample(proj, buf_t, conv_w):
    bsz = proj.shape[0]
    width, d = conv_w.shape
    nb = d // LANES
    blk = lambda off: pl.BlockSpec((bsz, LANES), lambda j: (0, j + off))
    return pl.pallas_call(
        _sconv_sample_kernel,
        grid=(nb,),
        in_specs=[blk(0), blk(nb), blk(2 * nb),
                  pl.BlockSpec((width - 1, bsz, LANES), lambda j: (0, 0, j)),
                  pl.BlockSpec((width, LANES), lambda j: (0, j))],
        out_specs=[pl.BlockSpec((bsz, LANES), lambda j: (0, j)),
                   pl.BlockSpec((bsz, LANES), lambda j: (0, j))],
        out_shape=[jax.ShapeDtypeStruct((bsz, d), F32),
                   jax.ShapeDtypeStruct((bsz, d), F32)],
        compiler_params=_cparams(1),
        name="sconv_sample",
    )(proj, proj, proj, buf_t, conv_w)


def _router_kernel(x_ref, w_ref, o_ref, *, n_experts):
    x = x_ref[...]
    w = w_ref[...]
    x_hi = x.astype(BF16)
    w_hi = w.astype(BF16)
    x_lo = x - x_hi.astype(F32)
    w_lo = w - w_hi.astype(F32)
    logits = (jnp.dot(x_hi, w_hi, preferred_element_type=F32) + _bdot(x_hi, w_lo) + _bdot(x_lo, w_hi))
    lane = lax.broadcasted_iota(jnp.int32, logits.shape, 1).astype(F32)
    lg = jnp.where(lane < n_experts, logits, NEG_BIG)
    m1 = jnp.max(lg, axis=-1, keepdims=True)
    i1 = jnp.min(jnp.where(lg == m1, lane, float(LANES)), axis=-1, keepdims=True)
    lg2 = jnp.where(lane == i1, NEG_BIG, lg)
    m2 = jnp.max(lg2, axis=-1, keepdims=True)
    i2 = jnp.min(jnp.where(lg2 == m2, lane, float(LANES)), axis=-1, keepdims=True)
    e = jnp.exp(m2 - m1)
    g1 = 1.0 / (1.0 + e)
    g2 = e / (1.0 + e)
    o_ref[...] = jnp.where(lane == 0, i1, jnp.where(lane == 1, i2,
                           jnp.where(lane == 2, g1, jnp.where(lane == 3, g2, 0.0))))


def _router(x, w_pad, n_experts):
    m, k = x.shape
    tm = TM_TOKEN
    return pl.pallas_call(
        functools.partial(_router_kernel, n_experts=n_experts),
        grid=(m // tm,),
        in_specs=[pl.BlockSpec((tm, k), lambda i: (i, 0)),
                  pl.BlockSpec((k, LANES), lambda i: (0, 0))],
        out_specs=pl.BlockSpec((tm, LANES), lambda i: (i, 0)),
        out_shape=jax.ShapeDtypeStruct((m, LANES), F32),
        compiler_params=_cparams(1),
        name="moe_router",
    )(x, w_pad)


def _row_copy(src_hbm, src_row, dst_ref, dst_row, sem):
    return pltpu.make_async_copy(src_hbm.at[pl.ds(src_row, 1)], dst_ref.at[pl.ds(dst_row, 1)], sem)


def _wait_rows(src_hbm, dst_ref, sem):
    pltpu.make_async_copy(src_hbm.at[pl.ds(0, dst_ref.shape[0])], dst_ref, sem).wait()


def _dispatch_kernel(pos_ref, pad0_ref, padn_ref, nu_ref, src_hbm, xs_hbm, zeros_ref, sems, zsem,
                     *, n_experts, n_tiles, tile_rows):
    i = pl.program_id(0)
    last = pl.num_programs(0) - 1
    slot = jnp.bitwise_and(i, 1)
    t0 = i * DMA_SUB

    def zero_fill(action):
        for e in range(n_experts):
            start = pad0_ref[e]
            left = padn_ref[e]
            odd = jnp.bitwise_and(left, SUBLANES - 1)
            for r in range(SUBLANES - 1):
                @pl.when(r < odd)
                def _(r=r):
                    action(_row_copy(zeros_ref, 0, xs_hbm, start + r, zsem))
            cur = start + odd
            size = SUBLANES
            while size < tile_rows:
                on = jnp.bitwise_and(left, size) != 0

                @pl.when(on)
                def _(cur=cur, size=size):
                    rows = pl.ds(pl.multiple_of(cur, SUBLANES), size)
                    action(pltpu.make_async_copy(zeros_ref.at[pl.ds(0, size)], xs_hbm.at[rows], zsem))
                cur = cur + jnp.where(on, size, 0)
                size *= 2

        def whole_tile(tile, c):
            action(pltpu.make_async_copy(zeros_ref, xs_hbm.at[pl.ds(tile * tile_rows, tile_rows)], zsem))
            return c
        lax.fori_loop(nu_ref[0], n_tiles, whole_tile, 0)

    def wait_step(s):
        n = TOP_K * DMA_SUB
        pltpu.make_async_copy(src_hbm.at[pl.ds(0, n)], xs_hbm.at[pl.ds(0, n)], sems.at[s]).wait()

    @pl.when(i == 0)
    def _():
        zeros_ref[...] = jnp.zeros_like(zeros_ref)
        zero_fill(lambda cp: cp.start())

    for r in range(DMA_SUB):
        for kk in range(TOP_K):
            _row_copy(src_hbm, t0 + r, xs_hbm, pos_ref[TOP_K * (t0 + r) + kk], sems.at[slot]).start()

    @pl.when(i > 0)
    def _():
        wait_step(1 - slot)

    @pl.when(i == last)
    def _():
        wait_step(slot)
        zero_fill(lambda cp: cp.wait())


def _dispatch_rows(src, pos, pad_start, pad_len, n_used, n_tiles, tile_rows):
    n_tok, d = src.shape
    return pl.pallas_call(
        functools.partial(_dispatch_kernel, n_experts=pad_start.shape[0], n_tiles=n_tiles, tile_rows=tile_rows),
        grid_spec=pltpu.PrefetchScalarGridSpec(
            num_scalar_prefetch=4,
            grid=(n_tok // DMA_SUB,),
            in_specs=[pl.BlockSpec(memory_space=pl.ANY)],
            out_specs=pl.BlockSpec(memory_space=pl.ANY),
            scratch_shapes=[pltpu.VMEM((tile_rows, d), src.dtype), pltpu.SemaphoreType.DMA((2,)),
                            pltpu.SemaphoreType.DMA(())]),
        out_shape=jax.ShapeDtypeStruct((n_tiles * tile_rows, d), src.dtype),
        compiler_params=_cparams(1),
        name="moe_dispatch",
    )(pos, pad_start, pad_len, n_used, src)


def _moe_ffn1_kernel(te_ref, nu_ref, x_ref, wg_ref, wu_ref, o_ref, wg_bf, wu_bf):
    i = pl.program_id(1)
    new_expert = jnp.logical_or(i == 0, te_ref[i] != te_ref[jnp.maximum(i - 1, 0)])

    @pl.when(new_expert)
    def _():
        wg_bf[...] = wg_ref[0].astype(BF16)
        wu_bf[...] = wu_ref[0].astype(BF16)

    @pl.when(i < nu_ref[0])
    def _():
        _swiglu_slabs(x_ref, wg_bf, wu_bf, o_ref)

    @pl.when(i >= nu_ref[0])
    def _():
        o_ref[...] = jnp.zeros_like(o_ref)


def _moe_ffn1(xs, w_gu, tile_expert, n_used, tm, tn):
    rows, k = xs.shape
    f = w_gu.shape[2] // 2
    nj = f // tn
    return pl.pallas_call(
        _moe_ffn1_kernel,
        grid_spec=pltpu.PrefetchScalarGridSpec(
            num_scalar_prefetch=2,
            grid=(nj, rows // tm),
            in_specs=[pl.BlockSpec((tm, k), lambda j, i, te, nu: (i, 0)),
                      pl.BlockSpec((1, k, tn), lambda j, i, te, nu: (te[i], 0, j)),
                      pl.BlockSpec((1, k, tn), lambda j, i, te, nu: (te[i], 0, j + nj))],
            out_specs=pl.BlockSpec((tm, tn), lambda j, i, te, nu: (i, j)),
            scratch_shapes=[pltpu.VMEM((k, tn), BF16), pltpu.VMEM((k, tn), BF16)]),
        out_shape=jax.ShapeDtypeStruct((rows, f), BF16),
        compiler_params=_cparams(2),
        name="moe_ffn1",
    )(tile_expert, n_used, xs, w_gu, w_gu)


def _moe_ffn2_kernel(te_ref, nu_ref, x_ref, w_ref, o_ref, w_bf):
    i = pl.program_id(1)
    new_expert = jnp.logical_or(i == 0, te_ref[i] != te_ref[jnp.maximum(i - 1, 0)])

    @pl.when(new_expert)
    def _():
        w_bf[...] = w_ref[0].astype(BF16)

    @pl.when(i < nu_ref[0])
    def _():
        o_ref[...] = jnp.dot(x_ref[...], w_bf[...], preferred_element_type=F32)

    @pl.when(i >= nu_ref[0])
    def _():
        o_ref[...] = jnp.zeros_like(o_ref)


def _moe_ffn2(act, w_down, tile_expert, n_used, tm, tn):
    rows, k = act.shape
    d = w_down.shape[2]
    return pl.pallas_call(
        _moe_ffn2_kernel,
        grid_spec=pltpu.PrefetchScalarGridSpec(
            num_scalar_prefetch=2,
            grid=(d // tn, rows // tm),
            in_specs=[pl.BlockSpec((tm, k), lambda j, i, te, nu: (i, 0)),
                      pl.BlockSpec((1, k, tn), lambda j, i, te, nu: (te[i], 0, j))],
            out_specs=pl.BlockSpec((tm, tn), lambda j, i, te, nu: (i, j)),
            scratch_shapes=[pltpu.VMEM((k, tn), BF16)]),
        out_shape=jax.ShapeDtypeStruct((rows, d), F32),
        compiler_params=_cparams(2),
        name="moe_ffn2",
    )(tile_expert, n_used, act, w_down)


def _combine_kernel(pos_ref, ys_hbm, route_ref, res_ref, g_ref, b_ref, o_head, o_tail, buf_a, buf_b, sems,
                    *, n_head_tiles):
    i = pl.program_id(0)
    tm = res_ref.shape[0]
    n_sub = jnp.where(i < n_head_tiles, tm // DMA_SUB, o_tail.shape[0] // DMA_SUB)
    base2 = i * (TOP_K * tm)

    def issue(j, slot):
        p0 = base2 + j * (TOP_K * DMA_SUB)
        for r in range(DMA_SUB):
            _row_copy(ys_hbm, pos_ref[p0 + TOP_K * r], buf_a.at[slot], r, sems.at[0, slot]).start()
            _row_copy(ys_hbm, pos_ref[p0 + TOP_K * r + 1], buf_b.at[slot], r, sems.at[1, slot]).start()

    issue(0, 0)

    def body(j, c):
        slot = jnp.bitwise_and(j, 1)

        @pl.when(j + 1 < n_sub)
        def _():
            issue(j + 1, 1 - slot)

        _wait_rows(ys_hbm, buf_a.at[slot], sems.at[0, slot])
        _wait_rows(ys_hbm, buf_b.at[slot], sems.at[1, slot])
        rows = pl.ds(pl.multiple_of(j * DMA_SUB, DMA_SUB), DMA_SUB)
        route = route_ref[rows, :]
        y = route[:, 2:3] * buf_a[slot] + route[:, 3:4] * buf_b[slot]
        out = _layer_norm(ALPHA * res_ref[rows, :] + y, g_ref[...], b_ref[...])

        @pl.when(i < n_head_tiles)
        def _():
            o_head[rows, :] = out

        @pl.when(i >= n_head_tiles)
        def _():
            o_tail[rows, :] = out
        return c
    lax.fori_loop(0, n_sub, body, 0)


def _moe_combine(ys, pos, route, res, ln_g, ln_b, n_head):
    m, d = res.shape
    tm = TM_COMBINE
    head_tiles = n_head // tm
    n_tail = m - n_head
    assert n_head % tm == 0 and 0 < n_tail <= tm and n_tail % DMA_SUB == 0
    return pl.pallas_call(
        functools.partial(_combine_kernel, n_head_tiles=head_tiles),
        grid_spec=pltpu.PrefetchScalarGridSpec(
            num_scalar_prefetch=1,
            grid=(head_tiles + 1,),
            in_specs=[pl.BlockSpec(memory_space=pl.ANY),
                      pl.BlockSpec((tm, LANES), lambda i, p: (i, 0)),
                      pl.BlockSpec((tm, d), lambda i, p: (i, 0)),
                      pl.BlockSpec((1, d), lambda i, p: (0, 0)),
                      pl.BlockSpec((1, d), lambda i, p: (0, 0))],
            out_specs=[pl.BlockSpec((tm, d), lambda i, p: (jnp.minimum(i, head_tiles - 1), 0)),
                       pl.BlockSpec((n_tail, d), lambda i, p: (0, 0))],
            scratch_shapes=[pltpu.VMEM((2, DMA_SUB, d), F32), pltpu.VMEM((2, DMA_SUB, d), F32),
                            pltpu.SemaphoreType.DMA((2, 2))]),
        out_shape=[jax.ShapeDtypeStruct((n_head, d), F32),
                   jax.ShapeDtypeStruct((n_tail, d), F32)],
        compiler_params=_cparams(1),
        name="moe_combine",
    )(pos, ys, route, res, ln_g.reshape(1, d), ln_b.reshape(1, d))


def _dispatch_plan(route, n_experts, tm):
    ids = route[:, :TOP_K].astype(jnp.int32).reshape(-1)
    n_pairs = ids.shape[0]
    n_tiles = n_pairs // tm + n_experts
    onehot = (ids[:, None] == jnp.arange(n_experts, dtype=jnp.int32)[None, :]).astype(jnp.int32)
    rank = jnp.sum((jnp.cumsum(onehot, axis=0) - onehot) * onehot, axis=1)
    counts = jnp.sum(onehot, axis=0)
    tiles_e = (counts + tm - 1) // tm
    tile_end = jnp.cumsum(tiles_e)
    group_off = (tile_end - tiles_e) * tm
    pos = group_off[ids] + rank
    n_used = tile_end[n_experts - 1:]
    tile_ids = jnp.arange(n_tiles, dtype=jnp.int32)
    tile_expert = jnp.sum((tile_ids[:, None] >= tile_end[None, :]).astype(jnp.int32), axis=1)
    last_expert = jnp.sum((n_used - 1 >= tile_end).astype(jnp.int32))
    tile_expert = jnp.minimum(tile_expert, last_expert).astype(jnp.int32)
    pad_start = (group_off + counts).astype(jnp.int32)
    pad_len = (tiles_e * tm - counts).astype(jnp.int32)
    return pos.astype(jnp.int32), tile_expert, n_used.astype(jnp.int32), pad_start, pad_len, n_tiles


def _moe_layer(h_all, n_head, w_router, w_gu, w_down, ln_g, ln_b):
    n_experts = w_router.shape[1]
    w_pad = jnp.pad(w_router, ((0, 0), (0, LANES - n_experts)))
    route = _router(h_all, w_pad, n_experts)
    pos, tile_expert, n_used, pad_start, pad_len, n_tiles = _dispatch_plan(route, n_experts, TM_EXPERT)
    xs = _dispatch_rows(h_all, pos, pad_start, pad_len, n_used, n_tiles, TM_EXPERT)
    act = _moe_ffn1(xs, w_gu, tile_expert, n_used, TM_EXPERT, TN_EXPERT_UP)
    ys = _moe_ffn2(act, w_down, tile_expert, n_used, TM_EXPERT, TN_EXPERT_DOWN)
    return _moe_combine(ys, pos, route, h_all, ln_g, ln_b, n_head)


def kernel(x_prompt, x_sample, state_gdn_S, state_gdn_conv, state_sconv, ln_g, ln_b, gdn_w_in, gdn_conv_w, gdn_a_log, gdn_dt_bias, gdn_norm_w, gdn_w_out, sc_w_in, sc_conv_w, sc_w_out, ffn_w_gu, ffn_w_down, moe_w_router, moe_w_gu, moe_w_down):
    bsz, t, d = x_prompt.shape
    dec = x_sample.shape[0]
    heads = gdn_a_log.shape[1]
    dh = GDN_HEAD_DIM
    vw = heads * dh
    conv_ch = gdn_conv_w.shape[2]
    assert x_sample.shape[1] == 1 and conv_ch == 3 * vw and gdn_w_in.shape[2] == conv_ch + vw + 2 * heads
    assert t % GDN_TT == 0 and t % TM_PROMPT == 0 and (bsz * t + dec) % TM_TOKEN == 0
    assert dec % DMA_SUB == 0 and dec < TM_PROMPT
    n_p = bsz * t

    xp = x_prompt.reshape(n_p, d)
    xs = x_sample.reshape(dec, d)

    w_in = gdn_w_in[0]
    w_a = w_in[:, conv_ch + vw:conv_ch + vw + heads]
    w_b = w_in[:, conv_ch + vw + heads:]
    w_ab = jnp.pad(jnp.concatenate([w_a, w_b, w_a], axis=1), ((0, 0), (0, LANES - 3 * heads)))
    lane_pad = lambda p: jnp.pad(jnp.concatenate([p, p, p]), (0, LANES - 3 * heads)).reshape(1, LANES)
    alog_row = lane_pad(gdn_a_log[0])
    dtb_row = lane_pad(gdn_dt_bias[0])

    qk_p, qk_last = _gdn_in(xp, w_in, gdn_conv_w[0], 0, ("q",) * heads + ("k",) * heads, bsz, TM_PROMPT,
                            "gdn_in_qk")
    vz_p, v_last = _gdn_in(xp, w_in, gdn_conv_w[0], 2 * vw, ("v",) * heads + ("z",) * heads, bsz, TM_PROMPT,
                           "gdn_in_vz")
    proj_s = _matmul(xs, w_in, conv_ch + vw, dec, 1024, F32, "gdn_in_sample")
    gates_p = _gdn_gates(xp, w_ab, alog_row, dtb_row, heads, TM_PROMPT, True, "gdn_gates_prompt")
    gates_s = _gdn_gates(xs, w_ab, alog_row, dtb_row, heads, dec, False, "gdn_gates_sample")

    conv_buf_t = jnp.transpose(state_gdn_conv[0], (1, 0, 2))
    qkvn_s = _gdn_prep_sample(proj_s, conv_buf_t, gdn_conv_w[0], heads)

    grow = gates_p[:, :heads].reshape(bsz, t // GDN_TT, GDN_TT, heads).transpose(0, 1, 3, 2)
    og_p, s_prompt = _gdn_chunked(qk_p.reshape(bsz, t, 2 * vw), vz_p.reshape(bsz, t, 2 * vw),
                                  gates_p.reshape(bsz, t, LANES), grow, gdn_norm_w[0], heads)
    og_s, s_sample = _gdn_step(qkvn_s, proj_s, gates_s, state_gdn_S[0], gdn_norm_w[0], heads)

    conv_prompt = jnp.concatenate([qk_last, v_last], axis=-1)
    conv_sample = jnp.concatenate([state_gdn_conv[0][:, 1:], proj_s[:, None, :conv_ch]], axis=1)

    h_p, hb_p = _matmul_ln(og_p.reshape(n_p, vw), gdn_w_out[0], xp, ln_g[0, 0], ln_b[0, 0], TM_PROMPT,
                           "gdn_out_prompt", also_bf16=True)
    h_s, hb_s = _matmul_ln(og_s, gdn_w_out[0], xs, ln_g[0, 0], ln_b[0, 0], dec, "gdn_out_sample", also_bf16=True)

    d_ff = ffn_w_down.shape[1]
    act_p = _matmul_swiglu(hb_p, ffn_w_gu[0], TM_DOWN, d_ff, "ffn_up_prompt")
    act_s = _matmul_swiglu(hb_s, ffn_w_gu[0], dec, d_ff // 2, "ffn_up_sample")
    h_p, hb_p = _matmul_ln(act_p, ffn_w_down[0], h_p, ln_g[0, 1], ln_b[0, 1], TM_DOWN, "ffn_down_prompt",
                           also_bf16=True)
    h_s, hb_s = _matmul_ln(act_s, ffn_w_down[0], h_s, ln_g[0, 1], ln_b[0, 1], dec, "ffn_down_sample",
                           also_bf16=True)

    sg_p, sconv_prompt = _sconv_prompt(hb_p, sc_w_in[0], sc_conv_w[0], bsz, TM_PROMPT, TN_SCONV)
    sproj_s = _matmul(hb_s, sc_w_in[0], 3 * d, dec, 1024, F32, "sc_in_sample")
    sbuf_t = jnp.transpose(state_sconv[0], (1, 0, 2))
    sg_s, ch_s = _sconv_sample(sproj_s, sbuf_t, sc_conv_w[0])
    sconv_sample = jnp.concatenate([state_sconv[0][:, 1:], ch_s[:, None, :]], axis=1)

    h_all = _matmul_ln(sg_p, sc_w_out[0], h_p, ln_g[1, 0], ln_b[1, 0], TM_PROMPT,
                       "sc_out", tail=(sg_s, h_s))

    y_p, y_s = _moe_layer(h_all, n_p, moe_w_router[0], moe_w_gu[0], moe_w_down[0], ln_g[1, 1], ln_b[1, 1])

    return (y_p.reshape(bsz, t, d), y_s.reshape(dec, 1, d), s_prompt[None], s_sample[None],
            conv_prompt[None], conv_sample[None], sconv_prompt[None], sconv_sample[None])
```

```python
import functools

import jax
import jax.numpy as jnp
from jax import lax
from jax.experimental import pallas as pl
from jax.experimental.pallas import tpu as pltpu

F32 = jnp.float32
BF16 = jnp.bfloat16

DEPTH = 2
ALPHA = (2.0 * DEPTH) ** 0.25
LN_EPS = 1e-5
NORM_EPS = 1e-6
GDN_HEAD_DIM = 128
GDN_CHUNK = 64
GDN_SUB = 16
TOP_K = 2

LANES = 128
SUBLANES = 8
VMEM_LIMIT = 56 * 1024 * 1024
NEG_BIG = -1e30

TM_PROMPT = 1024
TM_DOWN = 512
GDN_TT = 256
GDN_GROUP = 8
TM_EXPERT = 512
TN_EXPERT_UP = 1792
TN_EXPERT_DOWN = 1024
TM_COMBINE = 1024
DMA_SUB = 128
DEC_BB = 8
MM_SLAB = 256
TN_SCONV = 512


def _cparams(n_axes, vmem=VMEM_LIMIT):
    return pltpu.CompilerParams(
        dimension_semantics=("arbitrary",) * n_axes, vmem_limit_bytes=vmem)


def _bdot(a, b):
    return jnp.dot(a.astype(BF16), b.astype(BF16), preferred_element_type=F32)


def _bdot_nt(a, b):
    return lax.dot_general(a.astype(BF16), b.astype(BF16),
                           (((1,), (1,)), ((), ())), preferred_element_type=F32)


def _bdot_tn(a, b):
    return lax.dot_general(a.astype(BF16), b.astype(BF16),
                           (((0,), (0,)), ((), ())), preferred_element_type=F32)


def _silu(x):
    return x * (0.5 * jnp.tanh(0.5 * x) + 0.5)


def _swiglu_slabs(x_ref, wg_bf, wu_bf, o_ref):
    slab = min(MM_SLAB, x_ref.shape[0])
    for r0 in range(0, x_ref.shape[0], slab):
        xb = x_ref[r0:r0 + slab, :].astype(BF16)
        g = jnp.dot(xb, wg_bf[...], preferred_element_type=F32)
        u = jnp.dot(xb, wu_bf[...], preferred_element_type=F32)
        o_ref[r0:r0 + slab, :] = (_silu(g) * u).astype(o_ref.dtype)


def _layer_norm(r, g, b):
    mu = jnp.mean(r, axis=-1, keepdims=True)
    d = r - mu
    var = jnp.mean(d * d, axis=-1, keepdims=True)
    return d * lax.rsqrt(var + LN_EPS) * g + b


def _mm_kernel(x_ref, w_ref, o_ref, wbf_ref):
    @pl.when(pl.program_id(1) == 0)
    def _():
        wbf_ref[...] = w_ref[...].astype(BF16)

    o_ref[...] = jnp.dot(x_ref[...].astype(BF16), wbf_ref[...],
                         preferred_element_type=F32).astype(o_ref.dtype)


def _matmul(x, w, n_cols, tm, tn, out_dtype, name):
    m, k = x.shape
    return pl.pallas_call(
        _mm_kernel,
        grid=(n_cols // tn, m // tm),
        in_specs=[pl.BlockSpec((tm, k), lambda j, i: (i, 0)),
                  pl.BlockSpec((k, tn), lambda j, i: (0, j))],
        out_specs=pl.BlockSpec((tm, tn), lambda j, i: (i, j)),
        out_shape=jax.ShapeDtypeStruct((m, n_cols), out_dtype),
        scratch_shapes=[pltpu.VMEM((k, tn), BF16)],
        compiler_params=_cparams(2),
        name=name,
    )(x, w)


def _mm_swiglu_kernel(x_ref, wg_ref, wu_ref, o_ref, wg_bf, wu_bf):
    @pl.when(pl.program_id(1) == 0)
    def _():
        wg_bf[...] = wg_ref[...].astype(BF16)
        wu_bf[...] = wu_ref[...].astype(BF16)

    _swiglu_slabs(x_ref, wg_bf, wu_bf, o_ref)


def _matmul_swiglu(x, w_gu, tm, tn, name):
    m, k = x.shape
    f = w_gu.shape[1] // 2
    nj = f // tn
    w_mode = dict(pipeline_mode=pl.Buffered(1)) if nj == 1 else {}
    return pl.pallas_call(
        _mm_swiglu_kernel,
        grid=(nj, m // tm),
        in_specs=[pl.BlockSpec((tm, k), lambda j, i: (i, 0)),
                  pl.BlockSpec((k, tn), lambda j, i: (0, j), **w_mode),
                  pl.BlockSpec((k, tn), lambda j, i: (0, j + nj), **w_mode)],
        out_specs=pl.BlockSpec((tm, tn), lambda j, i: (i, j)),
        out_shape=jax.ShapeDtypeStruct((m, f), BF16),
        scratch_shapes=[pltpu.VMEM((k, tn), BF16), pltpu.VMEM((k, tn), BF16)],
        compiler_params=_cparams(2),
        name=name,
    )(x, w_gu, w_gu)


def _top2_route(x, w, n_experts):
    x_hi = x.astype(BF16)
    w_hi = w.astype(BF16)
    x_lo = x - x_hi.astype(F32)
    w_lo = w - w_hi.astype(F32)
    logits = (jnp.dot(x_hi, w_hi, preferred_element_type=F32) + _bdot(x_hi, w_lo) + _bdot(x_lo, w_hi))
    lane = lax.broadcasted_iota(jnp.int32, logits.shape, 1).astype(F32)
    lg = jnp.where(lane < n_experts, logits, NEG_BIG)
    m1 = jnp.max(lg, axis=-1, keepdims=True)
    i1 = jnp.min(jnp.where(lg == m1, lane, float(LANES)), axis=-1, keepdims=True)
    lg2 = jnp.where(lane == i1, NEG_BIG, lg)
    m2 = jnp.max(lg2, axis=-1, keepdims=True)
    i2 = jnp.min(jnp.where(lg2 == m2, lane, float(LANES)), axis=-1, keepdims=True)
    e = jnp.exp(m2 - m1)
    g1 = 1.0 / (1.0 + e)
    g2 = e / (1.0 + e)
    return jnp.where(lane == 0, i1, jnp.where(lane == 1, i2,
                     jnp.where(lane == 2, g1, jnp.where(lane == 3, g2, 0.0))))


def _mm_ln_kernel(x_ref, w_ref, res_ref, g_ref, b_ref, *rest, n_out, has_tail, n_experts):
    tail_refs = rest[:2] if has_tail else ()
    rest = rest[len(tail_refs):]
    router_ref = rest[0] if n_experts else None
    rest = rest[int(bool(n_experts)):]
    out_refs = rest[:n_out]
    route_ref = rest[n_out] if n_experts else None
    wbf_ref = rest[-1]
    i = pl.program_id(0)

    @pl.when(i == 0)
    def _():
        wbf_ref[...] = w_ref[...].astype(BF16)

    def emit(xr, rr):
        slab = min(MM_SLAB, xr.shape[0])
        for r0 in range(0, xr.shape[0], slab):
            rows = slice(r0, r0 + slab)
            y = jnp.dot(xr[rows, :].astype(BF16), wbf_ref[...], preferred_element_type=F32)
            h = _layer_norm(ALPHA * rr[rows, :] + y, g_ref[...], b_ref[...])
            for o_ref in out_refs:
                o_ref[rows, :] = h.astype(o_ref.dtype)
            if n_experts:
                route_ref[rows, :] = _top2_route(h, router_ref[...], n_experts)

    if has_tail:
        n_main = pl.num_programs(0) - 1

        @pl.when(i < n_main)
        def _():
            emit(x_ref, res_ref)

        @pl.when(i == n_main)
        def _():
            emit(*tail_refs)
    else:
        emit(x_ref, res_ref)


def _matmul_ln(x, w, res, ln_g, ln_b, tm, name, also_bf16=False, tail=None, router=None):
    m, k = x.shape
    d = w.shape[1]
    n_main = m // tm
    main_idx = lambda i: (jnp.minimum(i, n_main - 1), 0)
    operands = [x, w, res, ln_g.reshape(1, d), ln_b.reshape(1, d)]
    in_specs = [pl.BlockSpec((tm, k), main_idx),
                pl.BlockSpec((k, d), lambda i: (0, 0)),
                pl.BlockSpec((tm, d), main_idx),
                pl.BlockSpec((1, d), lambda i: (0, 0)),
                pl.BlockSpec((1, d), lambda i: (0, 0))]
    m_out = m
    if tail is not None:
        m_tail = tail[0].shape[0]
        assert m_tail < tm and m_tail % (2 * SUBLANES) == 0
        operands += list(tail)
        in_specs += [pl.BlockSpec((m_tail, k), lambda i: (0, 0)), pl.BlockSpec((m_tail, d), lambda i: (0, 0))]
        m_out = m + m_tail
    out_shape = [jax.ShapeDtypeStruct((m_out, d), F32)]
    if also_bf16:
        out_shape.append(jax.ShapeDtypeStruct((m_out, d), BF16))
    n_out = len(out_shape)
    out_specs = [pl.BlockSpec((tm, d), lambda i: (i, 0)) for _ in out_shape]
    n_experts = 0
    if router is not None:
        w_router, n_experts = router
        operands.append(w_router)
        in_specs.append(pl.BlockSpec((d, LANES), lambda i: (0, 0)))
        out_shape.append(jax.ShapeDtypeStruct((m_out, LANES), F32))
        out_specs.append(pl.BlockSpec((tm, LANES), lambda i: (i, 0)))
    out = pl.pallas_call(
        functools.partial(_mm_ln_kernel, n_out=n_out, has_tail=tail is not None, n_experts=n_experts),
        grid=(n_main + int(tail is not None),),
        in_specs=in_specs,
        out_specs=out_specs,
        out_shape=out_shape,
        scratch_shapes=[pltpu.VMEM((k, d), BF16)],
        compiler_params=_cparams(1),
        name=name,
    )(*operands)
    return out if len(out) > 1 else out[0]


def _gates_kernel(x_ref, w_ref, alog_ref, dtb_ref, o_ref, *, heads, chunk_cumsum):
    a = _bdot(x_ref[...], w_ref[...])
    z = a + dtb_ref[...]
    softplus = jnp.maximum(z, 0.0) + jnp.log1p(jnp.exp(-jnp.abs(z)))
    g = -jnp.exp(alog_ref[...]) * softplus
    total = g
    if chunk_cumsum:
        n_rows = g.shape[0]
        row = jnp.bitwise_and(lax.broadcasted_iota(jnp.int32, g.shape, 0), GDN_CHUNK - 1)
        step = 1
        while step < GDN_CHUNK:
            g = g + jnp.where(row >= step, pltpu.roll(g, step, axis=0), 0.0)
            step *= 2
        total = g
        step = 1
        while step < GDN_CHUNK:
            total = jnp.where(row + step < GDN_CHUNK, pltpu.roll(total, n_rows - step, axis=0), total)
            step *= 2
    lane = lax.broadcasted_iota(jnp.int32, g.shape, 1)
    o_ref[...] = jnp.where(lane < heads, g, jnp.where(lane < 2 * heads, jax.nn.sigmoid(a), total))


def _gdn_gates(x, w_ab, alog_row, dtb_row, heads, tm, chunk_cumsum, name):
    m, k = x.shape
    return pl.pallas_call(
        functools.partial(_gates_kernel, heads=heads, chunk_cumsum=chunk_cumsum),
        grid=(m // tm,),
        in_specs=[pl.BlockSpec((tm, k), lambda i: (i, 0)),
                  pl.BlockSpec((k, LANES), lambda i: (0, 0)),
                  pl.BlockSpec((1, LANES), lambda i: (0, 0)),
                  pl.BlockSpec((1, LANES), lambda i: (0, 0))],
        out_specs=pl.BlockSpec((tm, LANES), lambda i: (i, 0)),
        out_shape=jax.ShapeDtypeStruct((m, LANES), F32),
        compiler_params=_cparams(1),
        name=name,
    )(x, w_ab, alog_row, dtb_row)


def _qkv_finish(conv, j, heads):
    y = _silu(conv)
    inv = lax.rsqrt(jnp.sum(y * y, axis=-1, keepdims=True) + NORM_EPS)
    scale = jnp.where(j < heads, inv * (GDN_HEAD_DIM ** -0.5),
                      jnp.where(j < 2 * heads, inv, 1.0))
    return y * scale


def _gdn_prep_sample_kernel(x_ref, buf_ref, w_ref, o_ref, *, heads):
    w = w_ref[...]
    width = w.shape[0]
    conv = w[width - 1:width, :] * x_ref[...]
    for s in range(width - 1):
        conv = conv + w[s:s + 1, :] * buf_ref[s]
    o_ref[...] = _qkv_finish(conv, pl.program_id(0), heads)


def _gdn_prep_sample(proj, buf_t, conv_w, heads):
    bsz = proj.shape[0]
    width, ch = conv_w.shape
    return pl.pallas_call(
        functools.partial(_gdn_prep_sample_kernel, heads=heads),
        grid=(ch // GDN_HEAD_DIM,),
        in_specs=[pl.BlockSpec((bsz, GDN_HEAD_DIM), lambda j: (0, j)),
                  pl.BlockSpec((width - 1, bsz, GDN_HEAD_DIM), lambda j: (0, 0, j)),
                  pl.BlockSpec((width, GDN_HEAD_DIM), lambda j: (0, j))],
        out_specs=pl.BlockSpec((bsz, GDN_HEAD_DIM), lambda j: (0, j)),
        out_shape=jax.ShapeDtypeStruct((bsz, ch), F32),
        compiler_params=_cparams(1),
        name="gdn_prep_sample",
    )(proj, buf_t, conv_w)


def _gated_rmsnorm(o, z, norm_w):
    on = o * lax.rsqrt(jnp.mean(o * o, axis=-1, keepdims=True) + NORM_EPS) * norm_w
    return on * _silu(z)


def _gdn_in_kernel(x_ref, w_ref, cw_ref, o_ref, last_ref, wbf_ref, halo_ref, *, kinds, tiles_per_seq):
    i = pl.program_id(0)
    dh = GDN_HEAD_DIM
    width = cw_ref.shape[0]
    tm = x_ref.shape[0]
    slab = min(MM_SLAB, tm)

    @pl.when(i == 0)
    def _():
        wbf_ref[...] = w_ref[...].astype(BF16)

    @pl.when(lax.rem(i, tiles_per_seq) == 0)
    def _():
        halo_ref[...] = jnp.zeros_like(halo_ref)

    halo_row = lax.broadcasted_iota(jnp.int32, (SUBLANES, dh), 0)
    for r0 in range(0, tm, slab):
        rows = slice(r0, r0 + slab)
        proj = jnp.dot(x_ref[rows, :].astype(BF16), wbf_ref[...], preferred_element_type=F32)
        for b, kind in enumerate(kinds):
            lanes = slice(b * dh, (b + 1) * dh)
            x = proj[:, lanes]
            if kind == "z":
                o_ref[rows, lanes] = x
                continue
            halo = halo_ref[:, lanes]
            w = cw_ref[:, lanes]
            acc = w[width - 1:width, :] * x
            for s in range(1, width):
                xr = pltpu.roll(x, s, axis=0)
                top = jnp.where(halo_row < s, pltpu.roll(halo, s, axis=0), xr[:SUBLANES, :])
                acc = acc + w[width - 1 - s:width - s, :] * jnp.concatenate([top, xr[SUBLANES:, :]], axis=0)
            halo_ref[:, lanes] = x[slab - SUBLANES:, :]
            y = _silu(acc)
            if kind in "qk":
                scale = dh ** -0.5 if kind == "q" else 1.0
                y = y * (lax.rsqrt(jnp.sum(y * y, axis=-1, keepdims=True) + NORM_EPS) * scale)
            o_ref[rows, lanes] = y
            if r0 + slab == tm:
                last_ref[0, :, lanes] = x[slab - (width - 1):, :]


def _gdn_in(x, w_in, conv_w, col0, kinds, bsz, tm, name):
    m, k = x.shape
    width = conv_w.shape[0]
    tn = GDN_HEAD_DIM * len(kinds)
    n_conv = GDN_HEAD_DIM * sum(kind != "z" for kind in kinds)
    assert col0 % tn == 0 and all(kind != "z" for kind in kinds[:n_conv // GDN_HEAD_DIM])
    tiles_per_seq = (m // bsz) // tm
    return pl.pallas_call(
        functools.partial(_gdn_in_kernel, kinds=kinds, tiles_per_seq=tiles_per_seq),
        grid=(m // tm,),
        in_specs=[pl.BlockSpec((tm, k), lambda i: (i, 0)),
                  pl.BlockSpec((k, tn), lambda i: (0, col0 // tn), pipeline_mode=pl.Buffered(1)),
                  pl.BlockSpec((width, n_conv), lambda i: (0, col0 // n_conv))],
        out_specs=[pl.BlockSpec((tm, tn), lambda i: (i, 0)),
                   pl.BlockSpec((1, width - 1, n_conv), lambda i: (i // tiles_per_seq, 0, 0))],
        out_shape=[jax.ShapeDtypeStruct((m, tn), F32),
                   jax.ShapeDtypeStruct((bsz, width - 1, n_conv), F32)],
        scratch_shapes=[pltpu.VMEM((k, tn), BF16), pltpu.VMEM((SUBLANES, n_conv), F32)],
        compiler_params=_cparams(1),
        name=name,
    )(x, w_in, conv_w)


def _gdn_tile_kernel(q_ref, k_ref, v_ref, z_ref, gates_ref, grow_ref, nw_ref,
                     og_ref, sout_ref, s_ref, *, heads):
    c_len = GDN_CHUNK
    dh = GDN_HEAD_DIM
    tt = q_ref.shape[1]
    n_c = tt // c_len
    tile = pl.program_id(1)

    @pl.when(tile == 0)
    def _():
        s_ref[...] = jnp.zeros_like(s_ref)

    row = lax.broadcasted_iota(jnp.int32, (tt, tt), 0)
    col = lax.broadcasted_iota(jnp.int32, (tt, tt), 1)
    same_chunk = jnp.bitwise_and(row, -c_len) == jnp.bitwise_and(col, -c_len)
    tril = jnp.logical_and(same_chunk, row >= col)
    strict = jnp.logical_and(same_chunk, row > col)
    same_sub = jnp.bitwise_and(row, -GDN_SUB) == jnp.bitwise_and(col, -GDN_SUB)
    gate_tile = gates_ref[0]
    grow_tile = grow_ref[0, 0]
    norm_w = nw_ref[...]

    for g0 in range(0, heads, GDN_GROUP):
        hs = range(g0, g0 + GDN_GROUP)
        q = [q_ref[0, :, h * dh:(h + 1) * dh] for h in hs]
        k = [k_ref[0, :, h * dh:(h + 1) * dh] for h in hs]
        v = [v_ref[0, :, h * dh:(h + 1) * dh] for h in hs]
        gcol = [gate_tile[:, h:h + 1] for h in hs]
        beta = [gate_tile[:, heads + h:heads + h + 1] for h in hs]
        glast = [gate_tile[:, 2 * heads + h:2 * heads + h + 1] for h in hs]
        decay = [jnp.exp(jnp.where(tril, gc - grow_tile[h:h + 1, :], NEG_BIG)) for h, gc in zip(hs, gcol)]
        kb = [ki * bi for ki, bi in zip(k, beta)]
        a_mat = [jnp.where(strict, _bdot_nt(kbi, ki) * di, 0.0) for kbi, ki, di in zip(kb, k, decay)]
        qk = [jnp.where(tril, _bdot_nt(qi, ki) * di, 0.0) for qi, ki, di in zip(q, k, decay)]

        p = [jnp.where(same_sub, ai, 0.0) for ai in a_mat]
        e_mat = [ai - pi for ai, pi in zip(a_mat, p)]
        x = [-pi for pi in p]
        span = 1
        while 2 * span < GDN_SUB:
            p = [_bdot(pi, pi) for pi in p]
            xp = [_bdot(xi, pi) for xi, pi in zip(x, p)]
            x = [xi + pi + xpi for xi, pi, xpi in zip(x, p, xp)]
            span *= 2
        rhs = [jnp.concatenate([vi * bi, kbi * jnp.exp(gc)], axis=-1)
               for vi, bi, kbi, gc in zip(v, beta, kb, gcol)]
        n_mat = [ei + _bdot(xi, ei) for ei, xi in zip(e_mat, x)]
        r = [ri + _bdot(xi, ri) for ri, xi in zip(rhs, x)]
        r = [ri - _bdot(ni, ri) for ri, ni in zip(r, n_mat)]
        span = 2
        while span < c_len // GDN_SUB:
            n_mat = [_bdot(ni, ni) for ni in n_mat]
            r = [ri + _bdot(ni, ri) for ri, ni in zip(r, n_mat)]
            span *= 2

        o_uw = [_bdot(qki, ri) for qki, ri in zip(qk, r)]
        q_eff = [qi * jnp.exp(gc) - oi[:, dh:] for qi, gc, oi in zip(q, gcol, o_uw)]
        k_dec = [ki * jnp.exp(gl - gc) for ki, gl, gc in zip(k, glast, gcol)]
        s = [s_ref[h] for h in hs]
        for c in range(n_c):
            rows = slice(c * c_len, (c + 1) * c_len)
            kr = [_bdot_tn(kd[rows, :], ri[rows, :]) for kd, ri in zip(k_dec, r)]
            o = [oi[rows, :dh] + _bdot(qe[rows, :], si) for oi, qe, si in zip(o_uw, q_eff, s)]
            for h, oi in zip(hs, o):
                z = z_ref[0, rows, h * dh:(h + 1) * dh]
                og_ref[0, rows, h * dh:(h + 1) * dh] = _gated_rmsnorm(oi, z, norm_w).astype(og_ref.dtype)
            a_c = [jnp.exp(gl[(c + 1) * c_len - 1:(c + 1) * c_len, :]) for gl in glast]
            s = [si * ai + kri[:, :dh] - _bdot(kri[:, dh:], si) for si, ai, kri in zip(s, a_c, kr)]
        for h, si in zip(hs, s):
            s_ref[h] = si

    @pl.when(tile == pl.num_programs(1) - 1)
    def _():
        sout_ref[0] = s_ref[...]


def _gdn_chunked(qk, vz, gates, grow, norm_w, heads):
    bsz, t, _ = qk.shape
    dh = GDN_HEAD_DIM
    vw = heads * dh
    tt = GDN_TT
    blk = lambda part: pl.BlockSpec((1, tt, vw), lambda b, i: (b, i, part))
    return pl.pallas_call(
        functools.partial(_gdn_tile_kernel, heads=heads),
        grid=(bsz, t // tt),
        in_specs=[blk(0), blk(1), blk(0), blk(1),
                  pl.BlockSpec((1, tt, LANES), lambda b, i: (b, i, 0)),
                  pl.BlockSpec((1, 1, heads, tt), lambda b, i: (b, i, 0, 0)),
                  pl.BlockSpec((1, dh), lambda b, i: (0, 0))],
        out_specs=[pl.BlockSpec((1, tt, vw), lambda b, i: (b, i, 0)),
                   pl.BlockSpec((1, heads, dh, dh), lambda b, i: (b, 0, 0, 0))],
        out_shape=[jax.ShapeDtypeStruct((bsz, t, vw), BF16),
                   jax.ShapeDtypeStruct((bsz, heads, dh, dh), F32)],
        scratch_shapes=[pltpu.VMEM((heads, dh, dh), F32)],
        compiler_params=_cparams(2),
        name="gdn_chunked",
    )(qk, qk, vz, vz, gates, grow, norm_w.reshape(1, dh))


def _gdn_step_kernel(qkv_ref, z_ref, gates_ref, s0_ref, nw_ref, og_ref, sout_ref, *, heads):
    dh = GDN_HEAD_DIM
    pad = 2 * SUBLANES
    norm_w = nw_ref[...]
    prow = lax.broadcasted_iota(jnp.int32, (pad, dh), 0)

    pairs = [(bi, h) for bi in range(qkv_ref.shape[0]) for h in range(heads)]
    rows = {bi: slice(bi, bi + 1) for bi, _ in pairs}
    q = [qkv_ref[rows[bi], h * dh:(h + 1) * dh] for bi, h in pairs]
    k = [qkv_ref[rows[bi], (heads + h) * dh:(heads + h + 1) * dh] for bi, h in pairs]
    a = [jnp.exp(gates_ref[rows[bi], h:h + 1]) for bi, h in pairs]
    s_kq = [_bdot(jnp.where(prow == 0, ki, jnp.where(prow == 1, qi, 0.0)), s0_ref[bi, h])
            for (bi, h), ki, qi in zip(pairs, k, q)]
    k_parts, d_parts = [], []
    for (bi, h), qi, ki, ai, si in zip(pairs, q, k, a, s_kq):
        v = qkv_ref[rows[bi], (2 * heads + h) * dh:(2 * heads + h + 1) * dh]
        beta = gates_ref[rows[bi], heads + h:heads + h + 1]
        delta = (v - ai * si[0:1, :]) * beta
        o = ai * si[1:2, :] + jnp.sum(ki * qi, axis=-1, keepdims=True) * delta
        og_ref[rows[bi], h * dh:(h + 1) * dh] = _gated_rmsnorm(o, z_ref[rows[bi], h * dh:(h + 1) * dh], norm_w)
        k_hi = ki.astype(BF16).astype(F32)
        d_hi = delta.astype(BF16).astype(F32)
        k_parts.append(jnp.where(prow < 2, k_hi, jnp.where(prow == 2, ki - k_hi, 0.0)))
        d_parts.append(jnp.where(prow == 1, delta - d_hi, jnp.where(prow < 3, d_hi, 0.0)))
    for (bi, h), ai, kp, dp in zip(pairs, a, k_parts, d_parts):
        sout_ref[bi, h] = s0_ref[bi, h] * ai + _bdot_tn(kp, dp)


def _gdn_step(qkvn, proj, gates, s0, norm_w, heads):
    bsz = qkvn.shape[0]
    dh = GDN_HEAD_DIM
    vw = heads * dh
    bb = DEC_BB
    return pl.pallas_call(
        functools.partial(_gdn_step_kernel, heads=heads),
        grid=(bsz // bb,),
        in_specs=[pl.BlockSpec((bb, 3 * vw), lambda i: (i, 0)),
                  pl.BlockSpec((bb, vw), lambda i: (i, 3)),
                  pl.BlockSpec((bb, LANES), lambda i: (i, 0)),
                  pl.BlockSpec((bb, heads, dh, dh), lambda i: (i, 0, 0, 0)),
                  pl.BlockSpec((1, dh), lambda i: (0, 0))],
        out_specs=[pl.BlockSpec((bb, vw), lambda i: (i, 0)),
                   pl.BlockSpec((bb, heads, dh, dh), lambda i: (i, 0, 0, 0))],
        out_shape=[jax.ShapeDtypeStruct((bsz, vw), F32),
                   jax.ShapeDtypeStruct((bsz, heads, dh, dh), F32)],
        compiler_params=_cparams(1),
        name="gdn_step",
    )(qkvn, proj, gates, s0, norm_w.reshape(1, dh))


def _sconv_prompt_kernel(x_ref, wb_ref, wc_ref, wh_ref, cw_ref, o_ref, last_ref,
                         wb_bf, wc_bf, wh_bf, halo_ref, *, tiles_per_seq):
    i = pl.program_id(1)

    @pl.when(i == 0)
    def _():
        wb_bf[...] = wb_ref[...].astype(BF16)
        wc_bf[...] = wc_ref[...].astype(BF16)
        wh_bf[...] = wh_ref[...].astype(BF16)

    @pl.when(lax.rem(i, tiles_per_seq) == 0)
    def _():
        halo_ref[...] = jnp.zeros_like(halo_ref)

    xb = x_ref[...].astype(BF16)
    gate = jnp.dot(xb, wb_bf[...], preferred_element_type=F32)
    ch = (jnp.dot(xb, wc_bf[...], preferred_element_type=F32)
          * jnp.dot(xb, wh_bf[...], preferred_element_type=F32))
    w = cw_ref[...]
    width = w.shape[0]
    tm = ch.shape[0]
    halo = halo_ref[...]
    halo_row = lax.broadcasted_iota(jnp.int32, halo.shape, 0)
    conv = w[width - 1:width, :] * ch
    for s in range(1, width):
        xr = pltpu.roll(ch, s, axis=0)
        top = jnp.where(halo_row < s, pltpu.roll(halo, s, axis=0), xr[:SUBLANES, :])
        conv = conv + w[width - 1 - s:width - s, :] * jnp.concatenate([top, xr[SUBLANES:, :]], axis=0)
    halo_ref[...] = ch[tm - SUBLANES:, :]
    o_ref[...] = (gate * conv).astype(o_ref.dtype)
    last_ref[0] = ch[tm - (width - 1):, :]


def _sconv_prompt(x, w_in, conv_w, bsz, tm, tn):
    m, k = x.shape
    width, d = conv_w.shape
    t = m // bsz
    tiles_per_seq = t // tm
    nb = d // tn
    wblk = lambda off: pl.BlockSpec((k, tn), lambda j, i: (0, j + off))
    return pl.pallas_call(
        functools.partial(_sconv_prompt_kernel, tiles_per_seq=tiles_per_seq),
        grid=(nb, m // tm),
        in_specs=[pl.BlockSpec((tm, k), lambda j, i: (i, 0)),
                  wblk(0), wblk(nb), wblk(2 * nb),
                  pl.BlockSpec((width, tn), lambda j, i: (0, j))],
        out_specs=[pl.BlockSpec((tm, tn), lambda j, i: (i, j)),
                   pl.BlockSpec((1, width - 1, tn), lambda j, i: (i // tiles_per_seq, 0, j))],
        out_shape=[jax.ShapeDtypeStruct((m, d), BF16),
                   jax.ShapeDtypeStruct((bsz, width - 1, d), F32)],
        scratch_shapes=[pltpu.VMEM((k, tn), BF16), pltpu.VMEM((k, tn), BF16), pltpu.VMEM((k, tn), BF16),
                        pltpu.VMEM((SUBLANES, tn), F32)],
        compiler_params=_cparams(2),
        name="sconv_prompt",
    )(x, w_in, w_in, w_in, conv_w)


def _sconv_sample_kernel(b_ref, c_ref, h_ref, buf_ref, w_ref, o_ref, ch_ref):
    ch = c_ref[...] * h_ref[...]
    w = w_ref[...]
    width = w.shape[0]
    conv = w[width - 1:width, :] * ch
    for s in range(width - 1):
        conv = conv + w[s:s + 1, :] * buf_ref[s]
    o_ref[...] = b_ref[...] * conv
    ch_ref[...] = ch


def _sconv_sample(proj, buf_t, conv_w):
    bsz = proj.shape[0]
    width, d = conv_w.shape
    nb = d // LANES
    blk = lambda off: pl.BlockSpec((bsz, LANES), lambda j: (0, j + off))
    return pl.pallas_call(
        _sconv_sample_kernel,
        grid=(nb,),
        in_specs=[blk(0), blk(nb), blk(2 * nb),
                  pl.BlockSpec((width - 1, bsz, LANES), lambda j: (0, 0, j)),
                  pl.BlockSpec((width, LANES), lambda j: (0, j))],
        out_specs=[pl.BlockSpec((bsz, LANES), lambda j: (0, j)),
                   pl.BlockSpec((bsz, LANES), lambda j: (0, j))],
        out_shape=[jax.ShapeDtypeStruct((bsz, d), F32),
                   jax.ShapeDtypeStruct((bsz, d), F32)],
        compiler_params=_cparams(1),
        name="sconv_sample",
    )(proj, proj, proj, buf_t, conv_w)


def _row_copy(src_hbm, src_row, dst_ref, dst_row, sem):
    return pltpu.make_async_copy(src_hbm.at[pl.ds(src_row, 1)], dst_ref.at[pl.ds(dst_row, 1)], sem)


def _wait_rows(src_hbm, dst_ref, sem):
    pltpu.make_async_copy(src_hbm.at[pl.ds(0, dst_ref.shape[0])], dst_ref, sem).wait()


def _gather_kernel(idx_ref, nu_ref, src_hbm, o_ref, buf, sems):
    i = pl.program_id(0)
    tg = o_ref.shape[0]
    n_sub = tg // DMA_SUB
    base = i * tg

    def issue(j, slot):
        for r in range(DMA_SUB):
            _row_copy(src_hbm, idx_ref[base + j * DMA_SUB + r], buf.at[slot], r, sems.at[slot]).start()

    @pl.when(i < nu_ref[0])
    def _():
        issue(0, 0)

        def body(j, c):
            slot = jnp.bitwise_and(j, 1)

            @pl.when(j + 1 < n_sub)
            def _():
                issue(j + 1, 1 - slot)

            _wait_rows(src_hbm, buf.at[slot], sems.at[slot])
            o_ref[pl.ds(pl.multiple_of(j * DMA_SUB, DMA_SUB), DMA_SUB), :] = buf[slot].astype(o_ref.dtype)
            return c
        lax.fori_loop(0, n_sub, body, 0)

    @pl.when(i >= nu_ref[0])
    def _():
        o_ref[...] = jnp.zeros_like(o_ref)


def _gather_rows(src, row_token, n_used, tg):
    rows = row_token.shape[0]
    d = src.shape[1]
    return pl.pallas_call(
        _gather_kernel,
        grid_spec=pltpu.PrefetchScalarGridSpec(
            num_scalar_prefetch=2,
            grid=(rows // tg,),
            in_specs=[pl.BlockSpec(memory_space=pl.ANY)],
            out_specs=pl.BlockSpec((tg, d), lambda i, idx, nu: (i, 0)),
            scratch_shapes=[pltpu.VMEM((2, DMA_SUB, d), src.dtype), pltpu.SemaphoreType.DMA((2,))]),
        out_shape=jax.ShapeDtypeStruct((rows, d), BF16),
        compiler_params=_cparams(1),
        name="moe_gather",
    )(row_token, n_used, src)


def _moe_ffn1_kernel(te_ref, nu_ref, x_ref, wg_ref, wu_ref, o_ref, wg_bf, wu_bf):
    i = pl.program_id(1)
    new_expert = jnp.logical_or(i == 0, te_ref[i] != te_ref[jnp.maximum(i - 1, 0)])

    @pl.when(new_expert)
    def _():
        wg_bf[...] = wg_ref[0].astype(BF16)
        wu_bf[...] = wu_ref[0].astype(BF16)

    @pl.when(i < nu_ref[0])
    def _():
        _swiglu_slabs(x_ref, wg_bf, wu_bf, o_ref)

    @pl.when(i >= nu_ref[0])
    def _():
        o_ref[...] = jnp.zeros_like(o_ref)


def _moe_ffn1(xs, w_gu, tile_expert, n_used, tm, tn):
    rows, k = xs.shape
    f = w_gu.shape[2] // 2
    nj = f // tn
    return pl.pallas_call(
        _moe_ffn1_kernel,
        grid_spec=pltpu.PrefetchScalarGridSpec(
            num_scalar_prefetch=2,
            grid=(nj, rows // tm),
            in_specs=[pl.BlockSpec((tm, k), lambda j, i, te, nu: (i, 0)),
                      pl.BlockSpec((1, k, tn), lambda j, i, te, nu: (te[i], 0, j)),
                      pl.BlockSpec((1, k, tn), lambda j, i, te, nu: (te[i], 0, j + nj))],
            out_specs=pl.BlockSpec((tm, tn), lambda j, i, te, nu: (i, j)),
            scratch_shapes=[pltpu.VMEM((k, tn), BF16), pltpu.VMEM((k, tn), BF16)]),
        out_shape=jax.ShapeDtypeStruct((rows, f), BF16),
        compiler_params=_cparams(2),
        name="moe_ffn1",
    )(tile_expert, n_used, xs, w_gu, w_gu)


def _moe_ffn2_kernel(te_ref, nu_ref, x_ref, w_ref, o_ref, w_bf):
    i = pl.program_id(1)
    new_expert = jnp.logical_or(i == 0, te_ref[i] != te_ref[jnp.maximum(i - 1, 0)])

    @pl.when(new_expert)
    def _():
        w_bf[...] = w_ref[0].astype(BF16)

    @pl.when(i < nu_ref[0])
    def _():
        o_ref[...] = jnp.dot(x_ref[...], w_bf[...], preferred_element_type=F32)

    @pl.when(i >= nu_ref[0])
    def _():
        o_ref[...] = jnp.zeros_like(o_ref)


def _moe_ffn2(act, w_down, tile_expert, n_used, tm, tn):
    rows, k = act.shape
    d = w_down.shape[2]
    return pl.pallas_call(
        _moe_ffn2_kernel,
        grid_spec=pltpu.PrefetchScalarGridSpec(
            num_scalar_prefetch=2,
            grid=(d // tn, rows // tm),
            in_specs=[pl.BlockSpec((tm, k), lambda j, i, te, nu: (i, 0)),
                      pl.BlockSpec((1, k, tn), lambda j, i, te, nu: (te[i], 0, j))],
            out_specs=pl.BlockSpec((tm, tn), lambda j, i, te, nu: (i, j)),
            scratch_shapes=[pltpu.VMEM((k, tn), BF16)]),
        out_shape=jax.ShapeDtypeStruct((rows, d), F32),
        compiler_params=_cparams(2),
        name="moe_ffn2",
    )(tile_expert, n_used, act, w_down)


def _combine_kernel(pos_ref, ys_hbm, route_ref, res_ref, g_ref, b_ref, o_head, o_tail, buf_a, buf_b, sems,
                    *, n_head_tiles):
    i = pl.program_id(0)
    tm = res_ref.shape[0]
    n_sub = jnp.where(i < n_head_tiles, tm // DMA_SUB, o_tail.shape[0] // DMA_SUB)
    base2 = i * (TOP_K * tm)

    def issue(j, slot):
        p0 = base2 + j * (TOP_K * DMA_SUB)
        for r in range(DMA_SUB):
            _row_copy(ys_hbm, pos_ref[p0 + TOP_K * r], buf_a.at[slot], r, sems.at[0, slot]).start()
            _row_copy(ys_hbm, pos_ref[p0 + TOP_K * r + 1], buf_b.at[slot], r, sems.at[1, slot]).start()

    issue(0, 0)

    def body(j, c):
        slot = jnp.bitwise_and(j, 1)

        @pl.when(j + 1 < n_sub)
        def _():
            issue(j + 1, 1 - slot)

        _wait_rows(ys_hbm, buf_a.at[slot], sems.at[0, slot])
        _wait_rows(ys_hbm, buf_b.at[slot], sems.at[1, slot])
        rows = pl.ds(pl.multiple_of(j * DMA_SUB, DMA_SUB), DMA_SUB)
        route = route_ref[rows, :]
        y = route[:, 2:3] * buf_a[slot] + route[:, 3:4] * buf_b[slot]
        out = _layer_norm(ALPHA * res_ref[rows, :] + y, g_ref[...], b_ref[...])

        @pl.when(i < n_head_tiles)
        def _():
            o_head[rows, :] = out

        @pl.when(i >= n_head_tiles)
        def _():
            o_tail[rows, :] = out
        return c
    lax.fori_loop(0, n_sub, body, 0)


def _moe_combine(ys, pos, route, res, ln_g, ln_b, n_head):
    m, d = res.shape
    tm = TM_COMBINE
    head_tiles = n_head // tm
    n_tail = m - n_head
    assert n_head % tm == 0 and 0 < n_tail <= tm and n_tail % DMA_SUB == 0
    return pl.pallas_call(
        functools.partial(_combine_kernel, n_head_tiles=head_tiles),
        grid_spec=pltpu.PrefetchScalarGridSpec(
            num_scalar_prefetch=1,
            grid=(head_tiles + 1,),
            in_specs=[pl.BlockSpec(memory_space=pl.ANY),
                      pl.BlockSpec((tm, LANES), lambda i, p: (i, 0)),
                      pl.BlockSpec((tm, d), lambda i, p: (i, 0)),
                      pl.BlockSpec((1, d), lambda i, p: (0, 0)),
                      pl.BlockSpec((1, d), lambda i, p: (0, 0))],
            out_specs=[pl.BlockSpec((tm, d), lambda i, p: (jnp.minimum(i, head_tiles - 1), 0)),
                       pl.BlockSpec((n_tail, d), lambda i, p: (0, 0))],
            scratch_shapes=[pltpu.VMEM((2, DMA_SUB, d), F32), pltpu.VMEM((2, DMA_SUB, d), F32),
                            pltpu.SemaphoreType.DMA((2, 2))]),
        out_shape=[jax.ShapeDtypeStruct((n_head, d), F32),
                   jax.ShapeDtypeStruct((n_tail, d), F32)],
        compiler_params=_cparams(1),
        name="moe_combine",
    )(pos, ys, route, res, ln_g.reshape(1, d), ln_b.reshape(1, d))


def _dispatch_plan(route, n_experts, tm):
    n_tok = route.shape[0]
    ids = route[:, :TOP_K].astype(jnp.int32).reshape(-1)
    n_pairs = ids.shape[0]
    n_tiles = n_pairs // tm + n_experts
    onehot = (ids[:, None] == jnp.arange(n_experts, dtype=jnp.int32)[None, :]).astype(jnp.int32)
    rank = jnp.sum((jnp.cumsum(onehot, axis=0) - onehot) * onehot, axis=1)
    counts = jnp.sum(onehot, axis=0)
    tiles_e = (counts + tm - 1) // tm
    tile_end = jnp.cumsum(tiles_e)
    group_off = (tile_end - tiles_e) * tm
    pos = group_off[ids] + rank
    row_token = jnp.zeros((n_tiles * tm,), jnp.int32).at[pos].set(
        jnp.arange(n_pairs, dtype=jnp.int32) // TOP_K)
    n_used = tile_end[n_experts - 1:]
    tile_ids = jnp.arange(n_tiles, dtype=jnp.int32)
    tile_expert = jnp.sum((tile_ids[:, None] >= tile_end[None, :]).astype(jnp.int32), axis=1)
    last_expert = jnp.sum((n_used - 1 >= tile_end).astype(jnp.int32))
    tile_expert = jnp.minimum(tile_expert, last_expert).astype(jnp.int32)
    return row_token, pos.astype(jnp.int32), tile_expert, n_used.astype(jnp.int32)


def _moe_layer(h_all, route, n_head, n_experts, w_gu, w_down, ln_g, ln_b):
    row_token, pos, tile_expert, n_used = _dispatch_plan(route, n_experts, TM_EXPERT)
    xs = _gather_rows(h_all, row_token, n_used, TM_EXPERT)
    act = _moe_ffn1(xs, w_gu, tile_expert, n_used, TM_EXPERT, TN_EXPERT_UP)
    ys = _moe_ffn2(act, w_down, tile_expert, n_used, TM_EXPERT, TN_EXPERT_DOWN)
    return _moe_combine(ys, pos, route, h_all, ln_g, ln_b, n_head)


def kernel(x_prompt, x_sample, state_gdn_S, state_gdn_conv, state_sconv, ln_g, ln_b, gdn_w_in, gdn_conv_w, gdn_a_log, gdn_dt_bias, gdn_norm_w, gdn_w_out, sc_w_in, sc_conv_w, sc_w_out, ffn_w_gu, ffn_w_down, moe_w_router, moe_w_gu, moe_w_down):
    bsz, t, d = x_prompt.shape
    dec = x_sample.shape[0]
    heads = gdn_a_log.shape[1]
    dh = GDN_HEAD_DIM
    vw = heads * dh
    conv_ch = gdn_conv_w.shape[2]
    assert x_sample.shape[1] == 1 and conv_ch == 3 * vw and gdn_w_in.shape[2] == conv_ch + vw + 2 * heads
    assert t % GDN_TT == 0 and t % TM_PROMPT == 0
    assert dec % DMA_SUB == 0 and dec < TM_PROMPT
    n_p = bsz * t

    xp = x_prompt.reshape(n_p, d)
    xs = x_sample.reshape(dec, d)

    w_in = gdn_w_in[0]
    w_a = w_in[:, conv_ch + vw:conv_ch + vw + heads]
    w_b = w_in[:, conv_ch + vw + heads:]
    w_ab = jnp.pad(jnp.concatenate([w_a, w_b, w_a], axis=1), ((0, 0), (0, LANES - 3 * heads)))
    lane_pad = lambda p: jnp.pad(jnp.concatenate([p, p, p]), (0, LANES - 3 * heads)).reshape(1, LANES)
    alog_row = lane_pad(gdn_a_log[0])
    dtb_row = lane_pad(gdn_dt_bias[0])

    qk_p, qk_last = _gdn_in(xp, w_in, gdn_conv_w[0], 0, ("q",) * heads + ("k",) * heads, bsz, TM_PROMPT,
                            "gdn_in_qk")
    vz_p, v_last = _gdn_in(xp, w_in, gdn_conv_w[0], 2 * vw, ("v",) * heads + ("z",) * heads, bsz, TM_PROMPT,
                           "gdn_in_vz")
    proj_s = _matmul(xs, w_in, conv_ch + vw, dec, 1024, F32, "gdn_in_sample")
    gates_p = _gdn_gates(xp, w_ab, alog_row, dtb_row, heads, TM_PROMPT, True, "gdn_gates_prompt")
    gates_s = _gdn_gates(xs, w_ab, alog_row, dtb_row, heads, dec, False, "gdn_gates_sample")

    conv_buf_t = jnp.transpose(state_gdn_conv[0], (1, 0, 2))
    qkvn_s = _gdn_prep_sample(proj_s, conv_buf_t, gdn_conv_w[0], heads)

    grow = gates_p[:, :heads].reshape(bsz, t // GDN_TT, GDN_TT, heads).transpose(0, 1, 3, 2)
    og_p, s_prompt = _gdn_chunked(qk_p.reshape(bsz, t, 2 * vw), vz_p.reshape(bsz, t, 2 * vw),
                                  gates_p.reshape(bsz, t, LANES), grow, gdn_norm_w[0], heads)
    og_s, s_sample = _gdn_step(qkvn_s, proj_s, gates_s, state_gdn_S[0], gdn_norm_w[0], heads)

    conv_prompt = jnp.concatenate([qk_last, v_last], axis=-1)
    conv_sample = jnp.concatenate([state_gdn_conv[0][:, 1:], proj_s[:, None, :conv_ch]], axis=1)

    h_p, hb_p = _matmul_ln(og_p.reshape(n_p, vw), gdn_w_out[0], xp, ln_g[0, 0], ln_b[0, 0], TM_PROMPT,
                           "gdn_out_prompt", also_bf16=True)
    h_s, hb_s = _matmul_ln(og_s, gdn_w_out[0], xs, ln_g[0, 0], ln_b[0, 0], dec, "gdn_out_sample", also_bf16=True)

    d_ff = ffn_w_down.shape[1]
    act_p = _matmul_swiglu(hb_p, ffn_w_gu[0], TM_DOWN, d_ff, "ffn_up_prompt")
    act_s = _matmul_swiglu(hb_s, ffn_w_gu[0], dec, d_ff // 2, "ffn_up_sample")
    h_p, hb_p = _matmul_ln(act_p, ffn_w_down[0], h_p, ln_g[0, 1], ln_b[0, 1], TM_DOWN, "ffn_down_prompt",
                           also_bf16=True)
    h_s, hb_s = _matmul_ln(act_s, ffn_w_down[0], h_s, ln_g[0, 1], ln_b[0, 1], dec, "ffn_down_sample",
                           also_bf16=True)

    sg_p, sconv_prompt = _sconv_prompt(hb_p, sc_w_in[0], sc_conv_w[0], bsz, TM_PROMPT, TN_SCONV)
    sproj_s = _matmul(hb_s, sc_w_in[0], 3 * d, dec, 1024, F32, "sc_in_sample")
    sbuf_t = jnp.transpose(state_sconv[0], (1, 0, 2))
    sg_s, ch_s = _sconv_sample(sproj_s, sbuf_t, sc_conv_w[0])
    sconv_sample = jnp.concatenate([state_sconv[0][:, 1:], ch_s[:, None, :]], axis=1)

    n_experts = moe_w_router.shape[2]
    w_router = jnp.pad(moe_w_router[0], ((0, 0), (0, LANES - n_experts)))
    h_all, route = _matmul_ln(sg_p, sc_w_out[0], h_p, ln_g[1, 0], ln_b[1, 0], TM_PROMPT,
                              "sc_out", tail=(sg_s, h_s), router=(w_router, n_experts))

    y_p, y_s = _moe_layer(h_all, route, n_p, n_experts, moe_w_gu[0], moe_w_down[0], ln_g[1, 1], ln_b[1, 1])

    return (y_p.reshape(bsz, t, d), y_s.reshape(dec, 1, d), s_prompt[None], s_sample[None],
            conv_prompt[None], conv_sample[None], sconv_prompt[None], sconv_sample[None])
```

```python
import functools

import jax
import jax.numpy as jnp
from jax import lax
from jax.experimental import pallas as pl
from jax.experimental.pallas import tpu as pltpu

F32 = jnp.float32
BF16 = jnp.bfloat16

DEPTH = 2
ALPHA = (2.0 * DEPTH) ** 0.25
LN_EPS = 1e-5
NORM_EPS = 1e-6
GDN_HEAD_DIM = 128
GDN_CHUNK = 64
GDN_SUB = 16
TOP_K = 2

LANES = 128
SUBLANES = 8
VMEM_LIMIT = 56 * 1024 * 1024
NEG_BIG = -1e30

TM_PROMPT = 1024
TM_DOWN = 512
GDN_TT = 256
GDN_GROUP = 8
TM_EXPERT = 512
TN_EXPERT_UP = 1792
TN_EXPERT_DOWN = 1024
TM_COMBINE = 1024
DMA_SUB = 128
DMA_BUFS = 4
DEC_BB = 8
MM_SLAB = 256
TN_SCONV = 512


def _cparams(n_axes, vmem=VMEM_LIMIT):
    return pltpu.CompilerParams(
        dimension_semantics=("arbitrary",) * n_axes, vmem_limit_bytes=vmem)


def _bdot(a, b):
    return jnp.dot(a.astype(BF16), b.astype(BF16), preferred_element_type=F32)


def _bdot_nt(a, b):
    return lax.dot_general(a.astype(BF16), b.astype(BF16),
                           (((1,), (1,)), ((), ())), preferred_element_type=F32)


def _bdot_tn(a, b):
    return lax.dot_general(a.astype(BF16), b.astype(BF16),
                           (((0,), (0,)), ((), ())), preferred_element_type=F32)


def _silu(x):
    return x * (0.5 * jnp.tanh(0.5 * x) + 0.5)


def _swiglu_slabs(x_ref, wg_bf, wu_bf, o_ref):
    slab = min(MM_SLAB, x_ref.shape[0])
    for r0 in range(0, x_ref.shape[0], slab):
        xb = x_ref[r0:r0 + slab, :].astype(BF16)
        g = jnp.dot(xb, wg_bf[...], preferred_element_type=F32)
        u = jnp.dot(xb, wu_bf[...], preferred_element_type=F32)
        o_ref[r0:r0 + slab, :] = (_silu(g) * u).astype(o_ref.dtype)


def _layer_norm(r, g, b):
    mu = jnp.mean(r, axis=-1, keepdims=True)
    d = r - mu
    var = jnp.mean(d * d, axis=-1, keepdims=True)
    return d * lax.rsqrt(var + LN_EPS) * g + b


def _mm_kernel(x_ref, w_ref, o_ref, wbf_ref):
    @pl.when(pl.program_id(1) == 0)
    def _():
        wbf_ref[...] = w_ref[...].astype(BF16)

    o_ref[...] = jnp.dot(x_ref[...].astype(BF16), wbf_ref[...],
                         preferred_element_type=F32).astype(o_ref.dtype)


def _matmul(x, w, n_cols, tm, tn, out_dtype, name):
    m, k = x.shape
    return pl.pallas_call(
        _mm_kernel,
        grid=(n_cols // tn, m // tm),
        in_specs=[pl.BlockSpec((tm, k), lambda j, i: (i, 0)),
                  pl.BlockSpec((k, tn), lambda j, i: (0, j))],
        out_specs=pl.BlockSpec((tm, tn), lambda j, i: (i, j)),
        out_shape=jax.ShapeDtypeStruct((m, n_cols), out_dtype),
        scratch_shapes=[pltpu.VMEM((k, tn), BF16)],
        compiler_params=_cparams(2),
        name=name,
    )(x, w)


def _mm_swiglu_kernel(x_ref, wg_ref, wu_ref, o_ref, wg_bf, wu_bf):
    @pl.when(pl.program_id(1) == 0)
    def _():
        wg_bf[...] = wg_ref[...].astype(BF16)
        wu_bf[...] = wu_ref[...].astype(BF16)

    _swiglu_slabs(x_ref, wg_bf, wu_bf, o_ref)


def _matmul_swiglu(x, w_gu, tm, tn, name):
    m, k = x.shape
    f = w_gu.shape[1] // 2
    nj = f // tn
    w_mode = dict(pipeline_mode=pl.Buffered(1)) if nj == 1 else {}
    return pl.pallas_call(
        _mm_swiglu_kernel,
        grid=(nj, m // tm),
        in_specs=[pl.BlockSpec((tm, k), lambda j, i: (i, 0)),
                  pl.BlockSpec((k, tn), lambda j, i: (0, j), **w_mode),
                  pl.BlockSpec((k, tn), lambda j, i: (0, j + nj), **w_mode)],
        out_specs=pl.BlockSpec((tm, tn), lambda j, i: (i, j)),
        out_shape=jax.ShapeDtypeStruct((m, f), BF16),
        scratch_shapes=[pltpu.VMEM((k, tn), BF16), pltpu.VMEM((k, tn), BF16)],
        compiler_params=_cparams(2),
        name=name,
    )(x, w_gu, w_gu)


def _top2_route(x, w, n_experts):
    x_hi = x.astype(BF16)
    w_hi = w.astype(BF16)
    x_lo = x - x_hi.astype(F32)
    w_lo = w - w_hi.astype(F32)
    logits = (jnp.dot(x_hi, w_hi, preferred_element_type=F32) + _bdot(x_hi, w_lo) + _bdot(x_lo, w_hi))
    lane = lax.broadcasted_iota(jnp.int32, logits.shape, 1).astype(F32)
    lg = jnp.where(lane < n_experts, logits, NEG_BIG)
    m1 = jnp.max(lg, axis=-1, keepdims=True)
    i1 = jnp.min(jnp.where(lg == m1, lane, float(LANES)), axis=-1, keepdims=True)
    lg2 = jnp.where(lane == i1, NEG_BIG, lg)
    m2 = jnp.max(lg2, axis=-1, keepdims=True)
    i2 = jnp.min(jnp.where(lg2 == m2, lane, float(LANES)), axis=-1, keepdims=True)
    e = jnp.exp(m2 - m1)
    g1 = 1.0 / (1.0 + e)
    g2 = e / (1.0 + e)
    return jnp.where(lane == 0, i1, jnp.where(lane == 1, i2,
                     jnp.where(lane == 2, g1, jnp.where(lane == 3, g2, 0.0))))


def _mm_ln_kernel(x_ref, w_ref, res_ref, g_ref, b_ref, *rest, n_out, has_tail, n_experts):
    tail_refs = rest[:2] if has_tail else ()
    rest = rest[len(tail_refs):]
    router_ref = rest[0] if n_experts else None
    rest = rest[int(bool(n_experts)):]
    out_refs = rest[:n_out]
    route_ref = rest[n_out] if n_experts else None
    wbf_ref = rest[-1]
    i = pl.program_id(0)

    @pl.when(i == 0)
    def _():
        wbf_ref[...] = w_ref[...].astype(BF16)

    def emit(xr, rr):
        slab = min(MM_SLAB, xr.shape[0])
        for r0 in range(0, xr.shape[0], slab):
            rows = slice(r0, r0 + slab)
            y = jnp.dot(xr[rows, :].astype(BF16), wbf_ref[...], preferred_element_type=F32)
            h = _layer_norm(ALPHA * rr[rows, :] + y, g_ref[...], b_ref[...])
            for o_ref in out_refs:
                o_ref[rows, :] = h.astype(o_ref.dtype)
            if n_experts:
                route_ref[rows, :] = _top2_route(h, router_ref[...], n_experts)

    if has_tail:
        n_main = pl.num_programs(0) - 1

        @pl.when(i < n_main)
        def _():
            emit(x_ref, res_ref)

        @pl.when(i == n_main)
        def _():
            emit(*tail_refs)
    else:
        emit(x_ref, res_ref)


def _matmul_ln(x, w, res, ln_g, ln_b, tm, name, also_bf16=False, tail=None, router=None):
    m, k = x.shape
    d = w.shape[1]
    n_main = m // tm
    main_idx = lambda i: (jnp.minimum(i, n_main - 1), 0)
    operands = [x, w, res, ln_g.reshape(1, d), ln_b.reshape(1, d)]
    in_specs = [pl.BlockSpec((tm, k), main_idx),
                pl.BlockSpec((k, d), lambda i: (0, 0)),
                pl.BlockSpec((tm, d), main_idx),
                pl.BlockSpec((1, d), lambda i: (0, 0)),
                pl.BlockSpec((1, d), lambda i: (0, 0))]
    m_out = m
    if tail is not None:
        m_tail = tail[0].shape[0]
        assert m_tail < tm and m_tail % (2 * SUBLANES) == 0
        operands += list(tail)
        in_specs += [pl.BlockSpec((m_tail, k), lambda i: (0, 0)), pl.BlockSpec((m_tail, d), lambda i: (0, 0))]
        m_out = m + m_tail
    out_shape = [jax.ShapeDtypeStruct((m_out, d), F32)]
    if also_bf16:
        out_shape.append(jax.ShapeDtypeStruct((m_out, d), BF16))
    n_out = len(out_shape)
    out_specs = [pl.BlockSpec((tm, d), lambda i: (i, 0)) for _ in out_shape]
    n_experts = 0
    if router is not None:
        w_router, n_experts = router
        operands.append(w_router)
        in_specs.append(pl.BlockSpec((d, LANES), lambda i: (0, 0)))
        out_shape.append(jax.ShapeDtypeStruct((m_out, LANES), F32))
        out_specs.append(pl.BlockSpec((tm, LANES), lambda i: (i, 0)))
    out = pl.pallas_call(
        functools.partial(_mm_ln_kernel, n_out=n_out, has_tail=tail is not None, n_experts=n_experts),
        grid=(n_main + int(tail is not None),),
        in_specs=in_specs,
        out_specs=out_specs,
        out_shape=out_shape,
        scratch_shapes=[pltpu.VMEM((k, d), BF16)],
        compiler_params=_cparams(1),
        name=name,
    )(*operands)
    return out if len(out) > 1 else out[0]


def _gates_kernel(x_ref, w_ref, alog_ref, dtb_ref, o_ref, *, heads, chunk_cumsum):
    a = _bdot(x_ref[...], w_ref[...])
    z = a + dtb_ref[...]
    softplus = jnp.maximum(z, 0.0) + jnp.log1p(jnp.exp(-jnp.abs(z)))
    g = -jnp.exp(alog_ref[...]) * softplus
    total = g
    if chunk_cumsum:
        n_rows = g.shape[0]
        row = jnp.bitwise_and(lax.broadcasted_iota(jnp.int32, g.shape, 0), GDN_CHUNK - 1)
        step = 1
        while step < GDN_CHUNK:
            g = g + jnp.where(row >= step, pltpu.roll(g, step, axis=0), 0.0)
            step *= 2
        total = g
        step = 1
        while step < GDN_CHUNK:
            total = jnp.where(row + step < GDN_CHUNK, pltpu.roll(total, n_rows - step, axis=0), total)
            step *= 2
    lane = lax.broadcasted_iota(jnp.int32, g.shape, 1)
    o_ref[...] = jnp.where(lane < heads, g, jnp.where(lane < 2 * heads, jax.nn.sigmoid(a), total))


def _gdn_gates(x, w_ab, alog_row, dtb_row, heads, tm, chunk_cumsum, name):
    m, k = x.shape
    return pl.pallas_call(
        functools.partial(_gates_kernel, heads=heads, chunk_cumsum=chunk_cumsum),
        grid=(m // tm,),
        in_specs=[pl.BlockSpec((tm, k), lambda i: (i, 0)),
                  pl.BlockSpec((k, LANES), lambda i: (0, 0)),
                  pl.BlockSpec((1, LANES), lambda i: (0, 0)),
                  pl.BlockSpec((1, LANES), lambda i: (0, 0))],
        out_specs=pl.BlockSpec((tm, LANES), lambda i: (i, 0)),
        out_shape=jax.ShapeDtypeStruct((m, LANES), F32),
        compiler_params=_cparams(1),
        name=name,
    )(x, w_ab, alog_row, dtb_row)


def _qkv_finish(conv, j, heads):
    y = _silu(conv)
    inv = lax.rsqrt(jnp.sum(y * y, axis=-1, keepdims=True) + NORM_EPS)
    scale = jnp.where(j < heads, inv * (GDN_HEAD_DIM ** -0.5),
                      jnp.where(j < 2 * heads, inv, 1.0))
    return y * scale


def _gdn_prep_sample_kernel(x_ref, buf_ref, w_ref, o_ref, *, heads):
    w = w_ref[...]
    width = w.shape[0]
    conv = w[width - 1:width, :] * x_ref[...]
    for s in range(width - 1):
        conv = conv + w[s:s + 1, :] * buf_ref[s]
    o_ref[...] = _qkv_finish(conv, pl.program_id(0), heads)


def _gdn_prep_sample(proj, buf_t, conv_w, heads):
    bsz = proj.shape[0]
    width, ch = conv_w.shape
    return pl.pallas_call(
        functools.partial(_gdn_prep_sample_kernel, heads=heads),
        grid=(ch // GDN_HEAD_DIM,),
        in_specs=[pl.BlockSpec((bsz, GDN_HEAD_DIM), lambda j: (0, j)),
                  pl.BlockSpec((width - 1, bsz, GDN_HEAD_DIM), lambda j: (0, 0, j)),
                  pl.BlockSpec((width, GDN_HEAD_DIM), lambda j: (0, j))],
        out_specs=pl.BlockSpec((bsz, GDN_HEAD_DIM), lambda j: (0, j)),
        out_shape=jax.ShapeDtypeStruct((bsz, ch), F32),
        compiler_params=_cparams(1),
        name="gdn_prep_sample",
    )(proj, buf_t, conv_w)


def _gated_rmsnorm(o, z, norm_w):
    on = o * lax.rsqrt(jnp.mean(o * o, axis=-1, keepdims=True) + NORM_EPS) * norm_w
    return on * _silu(z)


def _gdn_in_kernel(x_ref, w_ref, cw_ref, o_ref, last_ref, wbf_ref, halo_ref, *, kinds, tiles_per_seq):
    i = pl.program_id(0)
    dh = GDN_HEAD_DIM
    width = cw_ref.shape[0]
    tm = x_ref.shape[0]
    slab = min(MM_SLAB, tm)

    @pl.when(i == 0)
    def _():
        wbf_ref[...] = w_ref[...].astype(BF16)

    @pl.when(lax.rem(i, tiles_per_seq) == 0)
    def _():
        halo_ref[...] = jnp.zeros_like(halo_ref)

    halo_row = lax.broadcasted_iota(jnp.int32, (SUBLANES, dh), 0)
    for r0 in range(0, tm, slab):
        rows = slice(r0, r0 + slab)
        proj = jnp.dot(x_ref[rows, :].astype(BF16), wbf_ref[...], preferred_element_type=F32)
        for b, kind in enumerate(kinds):
            lanes = slice(b * dh, (b + 1) * dh)
            x = proj[:, lanes]
            if kind == "z":
                o_ref[rows, lanes] = x
                continue
            halo = halo_ref[:, lanes]
            w = cw_ref[:, lanes]
            acc = w[width - 1:width, :] * x
            for s in range(1, width):
                xr = pltpu.roll(x, s, axis=0)
                top = jnp.where(halo_row < s, pltpu.roll(halo, s, axis=0), xr[:SUBLANES, :])
                acc = acc + w[width - 1 - s:width - s, :] * jnp.concatenate([top, xr[SUBLANES:, :]], axis=0)
            halo_ref[:, lanes] = x[slab - SUBLANES:, :]
            y = _silu(acc)
            if kind in "qk":
                scale = dh ** -0.5 if kind == "q" else 1.0
                y = y * (lax.rsqrt(jnp.sum(y * y, axis=-1, keepdims=True) + NORM_EPS) * scale)
            o_ref[rows, lanes] = y
            if r0 + slab == tm:
                last_ref[0, :, lanes] = x[slab - (width - 1):, :]


def _gdn_in(x, w_in, conv_w, col0, kinds, bsz, tm, name):
    m, k = x.shape
    width = conv_w.shape[0]
    tn = GDN_HEAD_DIM * len(kinds)
    n_conv = GDN_HEAD_DIM * sum(kind != "z" for kind in kinds)
    assert col0 % tn == 0 and all(kind != "z" for kind in kinds[:n_conv // GDN_HEAD_DIM])
    tiles_per_seq = (m // bsz) // tm
    return pl.pallas_call(
        functools.partial(_gdn_in_kernel, kinds=kinds, tiles_per_seq=tiles_per_seq),
        grid=(m // tm,),
        in_specs=[pl.BlockSpec((tm, k), lambda i: (i, 0)),
                  pl.BlockSpec((k, tn), lambda i: (0, col0 // tn), pipeline_mode=pl.Buffered(1)),
                  pl.BlockSpec((width, n_conv), lambda i: (0, col0 // n_conv))],
        out_specs=[pl.BlockSpec((tm, tn), lambda i: (i, 0)),
                   pl.BlockSpec((1, width - 1, n_conv), lambda i: (i // tiles_per_seq, 0, 0))],
        out_shape=[jax.ShapeDtypeStruct((m, tn), F32),
                   jax.ShapeDtypeStruct((bsz, width - 1, n_conv), F32)],
        scratch_shapes=[pltpu.VMEM((k, tn), BF16), pltpu.VMEM((SUBLANES, n_conv), F32)],
        compiler_params=_cparams(1),
        name=name,
    )(x, w_in, conv_w)


def _gdn_tile_kernel(q_ref, k_ref, v_ref, z_ref, gates_ref, grow_ref, nw_ref,
                     og_ref, sout_ref, s_ref, *, heads):
    c_len = GDN_CHUNK
    dh = GDN_HEAD_DIM
    tt = q_ref.shape[1]
    n_c = tt // c_len
    tile = pl.program_id(1)

    @pl.when(tile == 0)
    def _():
        s_ref[...] = jnp.zeros_like(s_ref)

    row = lax.broadcasted_iota(jnp.int32, (tt, tt), 0)
    col = lax.broadcasted_iota(jnp.int32, (tt, tt), 1)
    same_chunk = jnp.bitwise_and(row, -c_len) == jnp.bitwise_and(col, -c_len)
    tril = jnp.logical_and(same_chunk, row >= col)
    strict = jnp.logical_and(same_chunk, row > col)
    same_sub = jnp.bitwise_and(row, -GDN_SUB) == jnp.bitwise_and(col, -GDN_SUB)
    gate_tile = gates_ref[0]
    grow_tile = grow_ref[0, 0]
    norm_w = nw_ref[...]

    for g0 in range(0, heads, GDN_GROUP):
        hs = range(g0, g0 + GDN_GROUP)
        q = [q_ref[0, :, h * dh:(h + 1) * dh] for h in hs]
        k = [k_ref[0, :, h * dh:(h + 1) * dh] for h in hs]
        v = [v_ref[0, :, h * dh:(h + 1) * dh] for h in hs]
        gcol = [gate_tile[:, h:h + 1] for h in hs]
        beta = [gate_tile[:, heads + h:heads + h + 1] for h in hs]
        glast = [gate_tile[:, 2 * heads + h:2 * heads + h + 1] for h in hs]
        decay = [jnp.exp(jnp.where(tril, gc - grow_tile[h:h + 1, :], NEG_BIG)) for h, gc in zip(hs, gcol)]
        kb = [ki * bi for ki, bi in zip(k, beta)]
        a_mat = [jnp.where(strict, _bdot_nt(kbi, ki) * di, 0.0) for kbi, ki, di in zip(kb, k, decay)]
        qk = [jnp.where(tril, _bdot_nt(qi, ki) * di, 0.0) for qi, ki, di in zip(q, k, decay)]

        p = [jnp.where(same_sub, ai, 0.0) for ai in a_mat]
        e_mat = [ai - pi for ai, pi in zip(a_mat, p)]
        x = [-pi for pi in p]
        span = 1
        while 2 * span < GDN_SUB:
            p = [_bdot(pi, pi) for pi in p]
            xp = [_bdot(xi, pi) for xi, pi in zip(x, p)]
            x = [xi + pi + xpi for xi, pi, xpi in zip(x, p, xp)]
            span *= 2
        rhs = [jnp.concatenate([vi * bi, kbi * jnp.exp(gc)], axis=-1)
               for vi, bi, kbi, gc in zip(v, beta, kb, gcol)]
        n_mat = [ei + _bdot(xi, ei) for ei, xi in zip(e_mat, x)]
        r = [ri + _bdot(xi, ri) for ri, xi in zip(rhs, x)]
        r = [ri - _bdot(ni, ri) for ri, ni in zip(r, n_mat)]
        span = 2
        while span < c_len // GDN_SUB:
            n_mat = [_bdot(ni, ni) for ni in n_mat]
            r = [ri + _bdot(ni, ri) for ri, ni in zip(r, n_mat)]
            span *= 2

        o_uw = [_bdot(qki, ri) for qki, ri in zip(qk, r)]
        q_eff = [qi * jnp.exp(gc) - oi[:, dh:] for qi, gc, oi in zip(q, gcol, o_uw)]
        k_dec = [ki * jnp.exp(gl - gc) for ki, gl, gc in zip(k, glast, gcol)]
        s = [s_ref[h] for h in hs]
        for c in range(n_c):
            rows = slice(c * c_len, (c + 1) * c_len)
            kr = [_bdot_tn(kd[rows, :], ri[rows, :]) for kd, ri in zip(k_dec, r)]
            o = [oi[rows, :dh] + _bdot(qe[rows, :], si) for oi, qe, si in zip(o_uw, q_eff, s)]
            for h, oi in zip(hs, o):
                z = z_ref[0, rows, h * dh:(h + 1) * dh]
                og_ref[0, rows, h * dh:(h + 1) * dh] = _gated_rmsnorm(oi, z, norm_w).astype(og_ref.dtype)
            a_c = [jnp.exp(gl[(c + 1) * c_len - 1:(c + 1) * c_len, :]) for gl in glast]
            s = [si * ai + kri[:, :dh] - _bdot(kri[:, dh:], si) for si, ai, kri in zip(s, a_c, kr)]
        for h, si in zip(hs, s):
            s_ref[h] = si

    @pl.when(tile == pl.num_programs(1) - 1)
    def _():
        sout_ref[0] = s_ref[...]


def _gdn_chunked(qk, vz, gates, grow, norm_w, heads):
    bsz, t, _ = qk.shape
    dh = GDN_HEAD_DIM
    vw = heads * dh
    tt = GDN_TT
    blk = lambda part: pl.BlockSpec((1, tt, vw), lambda b, i: (b, i, part))
    return pl.pallas_call(
        functools.partial(_gdn_tile_kernel, heads=heads),
        grid=(bsz, t // tt),
        in_specs=[blk(0), blk(1), blk(0), blk(1),
                  pl.BlockSpec((1, tt, LANES), lambda b, i: (b, i, 0)),
                  pl.BlockSpec((1, 1, heads, tt), lambda b, i: (b, i, 0, 0)),
                  pl.BlockSpec((1, dh), lambda b, i: (0, 0))],
        out_specs=[pl.BlockSpec((1, tt, vw), lambda b, i: (b, i, 0)),
                   pl.BlockSpec((1, heads, dh, dh), lambda b, i: (b, 0, 0, 0))],
        out_shape=[jax.ShapeDtypeStruct((bsz, t, vw), BF16),
                   jax.ShapeDtypeStruct((bsz, heads, dh, dh), F32)],
        scratch_shapes=[pltpu.VMEM((heads, dh, dh), F32)],
        compiler_params=_cparams(2),
        name="gdn_chunked",
    )(qk, qk, vz, vz, gates, grow, norm_w.reshape(1, dh))


def _gdn_step_kernel(qkv_ref, z_ref, gates_ref, s0_ref, nw_ref, og_ref, sout_ref, *, heads):
    dh = GDN_HEAD_DIM
    pad = 2 * SUBLANES
    norm_w = nw_ref[...]
    prow = lax.broadcasted_iota(jnp.int32, (pad, dh), 0)

    pairs = [(bi, h) for bi in range(qkv_ref.shape[0]) for h in range(heads)]
    rows = {bi: slice(bi, bi + 1) for bi, _ in pairs}
    q = [qkv_ref[rows[bi], h * dh:(h + 1) * dh] for bi, h in pairs]
    k = [qkv_ref[rows[bi], (heads + h) * dh:(heads + h + 1) * dh] for bi, h in pairs]
    a = [jnp.exp(gates_ref[rows[bi], h:h + 1]) for bi, h in pairs]
    s_kq = [_bdot(jnp.where(prow == 0, ki, jnp.where(prow == 1, qi, 0.0)), s0_ref[bi, h])
            for (bi, h), ki, qi in zip(pairs, k, q)]
    k_parts, d_parts = [], []
    for (bi, h), qi, ki, ai, si in zip(pairs, q, k, a, s_kq):
        v = qkv_ref[rows[bi], (2 * heads + h) * dh:(2 * heads + h + 1) * dh]
        beta = gates_ref[rows[bi], heads + h:heads + h + 1]
        delta = (v - ai * si[0:1, :]) * beta
        o = ai * si[1:2, :] + jnp.sum(ki * qi, axis=-1, keepdims=True) * delta
        og_ref[rows[bi], h * dh:(h + 1) * dh] = _gated_rmsnorm(o, z_ref[rows[bi], h * dh:(h + 1) * dh], norm_w)
        k_hi = ki.astype(BF16).astype(F32)
        d_hi = delta.astype(BF16).astype(F32)
        k_parts.append(jnp.where(prow < 2, k_hi, jnp.where(prow == 2, ki - k_hi, 0.0)))
        d_parts.append(jnp.where(prow == 1, delta - d_hi, jnp.where(prow < 3, d_hi, 0.0)))
    for (bi, h), ai, kp, dp in zip(pairs, a, k_parts, d_parts):
        sout_ref[bi, h] = s0_ref[bi, h] * ai + _bdot_tn(kp, dp)


def _gdn_step(qkvn, proj, gates, s0, norm_w, heads):
    bsz = qkvn.shape[0]
    dh = GDN_HEAD_DIM
    vw = heads * dh
    bb = DEC_BB
    return pl.pallas_call(
        functools.partial(_gdn_step_kernel, heads=heads),
        grid=(bsz // bb,),
        in_specs=[pl.BlockSpec((bb, 3 * vw), lambda i: (i, 0)),
                  pl.BlockSpec((bb, vw), lambda i: (i, 3)),
                  pl.BlockSpec((bb, LANES), lambda i: (i, 0)),
                  pl.BlockSpec((bb, heads, dh, dh), lambda i: (i, 0, 0, 0)),
                  pl.BlockSpec((1, dh), lambda i: (0, 0))],
        out_specs=[pl.BlockSpec((bb, vw), lambda i: (i, 0)),
                   pl.BlockSpec((bb, heads, dh, dh), lambda i: (i, 0, 0, 0))],
        out_shape=[jax.ShapeDtypeStruct((bsz, vw), F32),
                   jax.ShapeDtypeStruct((bsz, heads, dh, dh), F32)],
        compiler_params=_cparams(1),
        name="gdn_step",
    )(qkvn, proj, gates, s0, norm_w.reshape(1, dh))


def _sconv_prompt_kernel(x_ref, wb_ref, wc_ref, wh_ref, cw_ref, o_ref, last_ref,
                         wb_bf, wc_bf, wh_bf, halo_ref, *, tiles_per_seq):
    i = pl.program_id(1)

    @pl.when(i == 0)
    def _():
        wb_bf[...] = wb_ref[...].astype(BF16)
        wc_bf[...] = wc_ref[...].astype(BF16)
        wh_bf[...] = wh_ref[...].astype(BF16)

    @pl.when(lax.rem(i, tiles_per_seq) == 0)
    def _():
        halo_ref[...] = jnp.zeros_like(halo_ref)

    xb = x_ref[...].astype(BF16)
    gate = jnp.dot(xb, wb_bf[...], preferred_element_type=F32)
    ch = (jnp.dot(xb, wc_bf[...], preferred_element_type=F32)
          * jnp.dot(xb, wh_bf[...], preferred_element_type=F32))
    w = cw_ref[...]
    width = w.shape[0]
    tm = ch.shape[0]
    halo = halo_ref[...]
    halo_row = lax.broadcasted_iota(jnp.int32, halo.shape, 0)
    conv = w[width - 1:width, :] * ch
    for s in range(1, width):
        xr = pltpu.roll(ch, s, axis=0)
        top = jnp.where(halo_row < s, pltpu.roll(halo, s, axis=0), xr[:SUBLANES, :])
        conv = conv + w[width - 1 - s:width - s, :] * jnp.concatenate([top, xr[SUBLANES:, :]], axis=0)
    halo_ref[...] = ch[tm - SUBLANES:, :]
    o_ref[...] = (gate * conv).astype(o_ref.dtype)
    last_ref[0] = ch[tm - (width - 1):, :]


def _sconv_prompt(x, w_in, conv_w, bsz, tm, tn):
    m, k = x.shape
    width, d = conv_w.shape
    t = m // bsz
    tiles_per_seq = t // tm
    nb = d // tn
    wblk = lambda off: pl.BlockSpec((k, tn), lambda j, i: (0, j + off))
    return pl.pallas_call(
        functools.partial(_sconv_prompt_kernel, tiles_per_seq=tiles_per_seq),
        grid=(nb, m // tm),
        in_specs=[pl.BlockSpec((tm, k), lambda j, i: (i, 0)),
                  wblk(0), wblk(nb), wblk(2 * nb),
                  pl.BlockSpec((width, tn), lambda j, i: (0, j))],
        out_specs=[pl.BlockSpec((tm, tn), lambda j, i: (i, j)),
                   pl.BlockSpec((1, width - 1, tn), lambda j, i: (i // tiles_per_seq, 0, j))],
        out_shape=[jax.ShapeDtypeStruct((m, d), BF16),
                   jax.ShapeDtypeStruct((bsz, width - 1, d), F32)],
        scratch_shapes=[pltpu.VMEM((k, tn), BF16), pltpu.VMEM((k, tn), BF16), pltpu.VMEM((k, tn), BF16),
                        pltpu.VMEM((SUBLANES, tn), F32)],
        compiler_params=_cparams(2),
        name="sconv_prompt",
    )(x, w_in, w_in, w_in, conv_w)


def _sconv_sample_kernel(b_ref, c_ref, h_ref, buf_ref, w_ref, o_ref, ch_ref):
    ch = c_ref[...] * h_ref[...]
    w = w_ref[...]
    width = w.shape[0]
    conv = w[width - 1:width, :] * ch
    for s in range(width - 1):
        conv = conv + w[s:s + 1, :] * buf_ref[s]
    o_ref[...] = b_ref[...] * conv
    ch_ref[...] = ch


def _sconv_sample(proj, buf_t, conv_w):
    bsz = proj.shape[0]
    width, d = conv_w.shape
    nb = d // LANES
    blk = lambda off: pl.BlockSpec((bsz, LANES), lambda j: (0, j + off))
    return pl.pallas_call(
        _sconv_sample_kernel,
        grid=(nb,),
        in_specs=[blk(0), blk(nb), blk(2 * nb),
                  pl.BlockSpec((width - 1, bsz, LANES), lambda j: (0, 0, j)),
                  pl.BlockSpec((width, LANES), lambda j: (0, j))],
        out_specs=[pl.BlockSpec((bsz, LANES), lambda j: (0, j)),
                   pl.BlockSpec((bsz, LANES), lambda j: (0, j))],
        out_shape=[jax.ShapeDtypeStruct((bsz, d), F32),
                   jax.ShapeDtypeStruct((bsz, d), F32)],
        compiler_params=_cparams(1),
        name="sconv_sample",
    )(proj, proj, proj, buf_t, conv_w)


def _row_copy(src_hbm, src_row, dst_ref, dst_row, sem):
    return pltpu.make_async_copy(src_hbm.at[pl.ds(src_row, 1)], dst_ref.at[pl.ds(dst_row, 1)], sem)


def _wait_rows(src_hbm, dst_ref, sem):
    pltpu.make_async_copy(src_hbm.at[pl.ds(0, dst_ref.shape[0])], dst_ref, sem).wait()


def _gather_kernel(idx_ref, nu_ref, src_hbm, o_ref, buf, sems):
    i = pl.program_id(0)
    tg = o_ref.shape[0]
    n_sub = tg // DMA_SUB
    ahead = DMA_BUFS - 1
    base = i * tg

    def issue(j, slot):
        for r in range(DMA_SUB):
            _row_copy(src_hbm, idx_ref[base + j * DMA_SUB + r], buf.at[slot], r, sems.at[slot]).start()

    @pl.when(i < nu_ref[0])
    def _():
        for j0 in range(min(ahead, n_sub)):
            issue(j0, j0)

        def body(j, c):
            slot = jnp.bitwise_and(j, DMA_BUFS - 1)

            @pl.when(j + ahead < n_sub)
            def _():
                issue(j + ahead, jnp.bitwise_and(j + ahead, DMA_BUFS - 1))

            _wait_rows(src_hbm, buf.at[slot], sems.at[slot])
            o_ref[pl.ds(pl.multiple_of(j * DMA_SUB, DMA_SUB), DMA_SUB), :] = buf[slot].astype(o_ref.dtype)
            return c
        lax.fori_loop(0, n_sub, body, 0)

    @pl.when(i >= nu_ref[0])
    def _():
        o_ref[...] = jnp.zeros_like(o_ref)


def _gather_rows(src, row_token, n_used, tg):
    rows = row_token.shape[0]
    d = src.shape[1]
    return pl.pallas_call(
        _gather_kernel,
        grid_spec=pltpu.PrefetchScalarGridSpec(
            num_scalar_prefetch=2,
            grid=(rows // tg,),
            in_specs=[pl.BlockSpec(memory_space=pl.ANY)],
            out_specs=pl.BlockSpec((tg, d), lambda i, idx, nu: (i, 0)),
            scratch_shapes=[pltpu.VMEM((DMA_BUFS, DMA_SUB, d), src.dtype), pltpu.SemaphoreType.DMA((DMA_BUFS,))]),
        out_shape=jax.ShapeDtypeStruct((rows, d), BF16),
        compiler_params=_cparams(1),
        name="moe_gather",
    )(row_token, n_used, src)


def _moe_ffn1_kernel(te_ref, nu_ref, x_ref, wg_ref, wu_ref, o_ref, wg_bf, wu_bf):
    i = pl.program_id(1)
    new_expert = jnp.logical_or(i == 0, te_ref[i] != te_ref[jnp.maximum(i - 1, 0)])

    @pl.when(new_expert)
    def _():
        wg_bf[...] = wg_ref[0].astype(BF16)
        wu_bf[...] = wu_ref[0].astype(BF16)

    @pl.when(i < nu_ref[0])
    def _():
        _swiglu_slabs(x_ref, wg_bf, wu_bf, o_ref)

    @pl.when(i >= nu_ref[0])
    def _():
        o_ref[...] = jnp.zeros_like(o_ref)


def _moe_ffn1(xs, w_gu, tile_expert, n_used, tm, tn):
    rows, k = xs.shape
    f = w_gu.shape[2] // 2
    nj = f // tn
    return pl.pallas_call(
        _moe_ffn1_kernel,
        grid_spec=pltpu.PrefetchScalarGridSpec(
            num_scalar_prefetch=2,
            grid=(nj, rows // tm),
            in_specs=[pl.BlockSpec((tm, k), lambda j, i, te, nu: (i, 0)),
                      pl.BlockSpec((1, k, tn), lambda j, i, te, nu: (te[i], 0, j)),
                      pl.BlockSpec((1, k, tn), lambda j, i, te, nu: (te[i], 0, j + nj))],
            out_specs=pl.BlockSpec((tm, tn), lambda j, i, te, nu: (i, j)),
            scratch_shapes=[pltpu.VMEM((k, tn), BF16), pltpu.VMEM((k, tn), BF16)]),
        out_shape=jax.ShapeDtypeStruct((rows, f), BF16),
        compiler_params=_cparams(2),
        name="moe_ffn1",
    )(tile_expert, n_used, xs, w_gu, w_gu)


def _moe_ffn2_kernel(te_ref, nu_ref, x_ref, w_ref, o_ref, w_bf):
    i = pl.program_id(1)
    new_expert = jnp.logical_or(i == 0, te_ref[i] != te_ref[jnp.maximum(i - 1, 0)])

    @pl.when(new_expert)
    def _():
        w_bf[...] = w_ref[0].astype(BF16)

    @pl.when(i < nu_ref[0])
    def _():
        o_ref[...] = jnp.dot(x_ref[...], w_bf[...], preferred_element_type=F32)

    @pl.when(i >= nu_ref[0])
    def _():
        o_ref[...] = jnp.zeros_like(o_ref)


def _moe_ffn2(act, w_down, tile_expert, n_used, tm, tn):
    rows, k = act.shape
    d = w_down.shape[2]
    return pl.pallas_call(
        _moe_ffn2_kernel,
        grid_spec=pltpu.PrefetchScalarGridSpec(
            num_scalar_prefetch=2,
            grid=(d // tn, rows // tm),
            in_specs=[pl.BlockSpec((tm, k), lambda j, i, te, nu: (i, 0)),
                      pl.BlockSpec((1, k, tn), lambda j, i, te, nu: (te[i], 0, j))],
            out_specs=pl.BlockSpec((tm, tn), lambda j, i, te, nu: (i, j)),
            scratch_shapes=[pltpu.VMEM((k, tn), BF16)]),
        out_shape=jax.ShapeDtypeStruct((rows, d), F32),
        compiler_params=_cparams(2),
        name="moe_ffn2",
    )(tile_expert, n_used, act, w_down)


def _combine_kernel(pos_ref, ys_hbm, route_ref, res_ref, g_ref, b_ref, o_head, o_tail, buf_a, buf_b, sems,
                    *, n_head_tiles):
    i = pl.program_id(0)
    tm = res_ref.shape[0]
    n_sub = jnp.where(i < n_head_tiles, tm // DMA_SUB, o_tail.shape[0] // DMA_SUB)
    base2 = i * (TOP_K * tm)

    def issue(j, slot):
        p0 = base2 + j * (TOP_K * DMA_SUB)
        for r in range(DMA_SUB):
            _row_copy(ys_hbm, pos_ref[p0 + TOP_K * r], buf_a.at[slot], r, sems.at[0, slot]).start()
            _row_copy(ys_hbm, pos_ref[p0 + TOP_K * r + 1], buf_b.at[slot], r, sems.at[1, slot]).start()

    ahead = DMA_BUFS - 1
    for j0 in range(ahead):
        @pl.when(j0 < n_sub)
        def _(j0=j0):
            issue(j0, j0)

    def body(j, c):
        slot = jnp.bitwise_and(j, DMA_BUFS - 1)

        @pl.when(j + ahead < n_sub)
        def _():
            issue(j + ahead, jnp.bitwise_and(j + ahead, DMA_BUFS - 1))

        _wait_rows(ys_hbm, buf_a.at[slot], sems.at[0, slot])
        _wait_rows(ys_hbm, buf_b.at[slot], sems.at[1, slot])
        rows = pl.ds(pl.multiple_of(j * DMA_SUB, DMA_SUB), DMA_SUB)
        route = route_ref[rows, :]
        y = route[:, 2:3] * buf_a[slot] + route[:, 3:4] * buf_b[slot]
        out = _layer_norm(ALPHA * res_ref[rows, :] + y, g_ref[...], b_ref[...])

        @pl.when(i < n_head_tiles)
        def _():
            o_head[rows, :] = out

        @pl.when(i >= n_head_tiles)
        def _():
            o_tail[rows, :] = out
        return c
    lax.fori_loop(0, n_sub, body, 0)


def _moe_combine(ys, pos, route, res, ln_g, ln_b, n_head):
    m, d = res.shape
    tm = TM_COMBINE
    head_tiles = n_head // tm
    n_tail = m - n_head
    assert n_head % tm == 0 and 0 < n_tail <= tm and n_tail % DMA_SUB == 0
    return pl.pallas_call(
        functools.partial(_combine_kernel, n_head_tiles=head_tiles),
        grid_spec=pltpu.PrefetchScalarGridSpec(
            num_scalar_prefetch=1,
            grid=(head_tiles + 1,),
            in_specs=[pl.BlockSpec(memory_space=pl.ANY),
                      pl.BlockSpec((tm, LANES), lambda i, p: (i, 0)),
                      pl.BlockSpec((tm, d), lambda i, p: (i, 0)),
                      pl.BlockSpec((1, d), lambda i, p: (0, 0)),
                      pl.BlockSpec((1, d), lambda i, p: (0, 0))],
            out_specs=[pl.BlockSpec((tm, d), lambda i, p: (jnp.minimum(i, head_tiles - 1), 0)),
                       pl.BlockSpec((n_tail, d), lambda i, p: (0, 0))],
            scratch_shapes=[pltpu.VMEM((DMA_BUFS, DMA_SUB, d), F32), pltpu.VMEM((DMA_BUFS, DMA_SUB, d), F32),
                            pltpu.SemaphoreType.DMA((2, DMA_BUFS))]),
        out_shape=[jax.ShapeDtypeStruct((n_head, d), F32),
                   jax.ShapeDtypeStruct((n_tail, d), F32)],
        compiler_params=_cparams(1),
        name="moe_combine",
    )(pos, ys, route, res, ln_g.reshape(1, d), ln_b.reshape(1, d))


def _dispatch_plan(route, n_experts, tm):
    n_tok = route.shape[0]
    ids = route[:, :TOP_K].astype(jnp.int32).reshape(-1)
    n_pairs = ids.shape[0]
    n_tiles = n_pairs // tm + n_experts
    onehot = (ids[:, None] == jnp.arange(n_experts, dtype=jnp.int32)[None, :]).astype(jnp.int32)
    rank = jnp.sum((jnp.cumsum(onehot, axis=0) - onehot) * onehot, axis=1)
    counts = jnp.sum(onehot, axis=0)
    tiles_e = (counts + tm - 1) // tm
    tile_end = jnp.cumsum(tiles_e)
    group_off = (tile_end - tiles_e) * tm
    pos = group_off[ids] + rank
    row_token = jnp.zeros((n_tiles * tm,), jnp.int32).at[pos].set(
        jnp.arange(n_pairs, dtype=jnp.int32) // TOP_K)
    n_used = tile_end[n_experts - 1:]
    tile_ids = jnp.arange(n_tiles, dtype=jnp.int32)
    tile_expert = jnp.sum((tile_ids[:, None] >= tile_end[None, :]).astype(jnp.int32), axis=1)
    last_expert = jnp.sum((n_used - 1 >= tile_end).astype(jnp.int32))
    tile_expert = jnp.minimum(tile_expert, last_expert).astype(jnp.int32)
    return row_token, pos.astype(jnp.int32), tile_expert, n_used.astype(jnp.int32)


def _moe_layer(h_all, route, n_head, n_experts, w_gu, w_down, ln_g, ln_b):
    row_token, pos, tile_expert, n_used = _dispatch_plan(route, n_experts, TM_EXPERT)
    xs = _gather_rows(h_all, row_token, n_used, TM_EXPERT)
    act = _moe_ffn1(xs, w_gu, tile_expert, n_used, TM_EXPERT, TN_EXPERT_UP)
    ys = _moe_ffn2(act, w_down, tile_expert, n_used, TM_EXPERT, TN_EXPERT_DOWN)
    return _moe_combine(ys, pos, route, h_all, ln_g, ln_b, n_head)


def kernel(x_prompt, x_sample, state_gdn_S, state_gdn_conv, state_sconv, ln_g, ln_b, gdn_w_in, gdn_conv_w, gdn_a_log, gdn_dt_bias, gdn_norm_w, gdn_w_out, sc_w_in, sc_conv_w, sc_w_out, ffn_w_gu, ffn_w_down, moe_w_router, moe_w_gu, moe_w_down):
    bsz, t, d = x_prompt.shape
    dec = x_sample.shape[0]
    heads = gdn_a_log.shape[1]
    dh = GDN_HEAD_DIM
    vw = heads * dh
    conv_ch = gdn_conv_w.shape[2]
    assert x_sample.shape[1] == 1 and conv_ch == 3 * vw and gdn_w_in.shape[2] == conv_ch + vw + 2 * heads
    assert t % GDN_TT == 0 and t % TM_PROMPT == 0
    assert dec % DMA_SUB == 0 and dec < TM_PROMPT
    n_p = bsz * t

    xp = x_prompt.reshape(n_p, d)
    xs = x_sample.reshape(dec, d)

    w_in = gdn_w_in[0]
    w_a = w_in[:, conv_ch + vw:conv_ch + vw + heads]
    w_b = w_in[:, conv_ch + vw + heads:]
    w_ab = jnp.pad(jnp.concatenate([w_a, w_b, w_a], axis=1), ((0, 0), (0, LANES - 3 * heads)))
    lane_pad = lambda p: jnp.pad(jnp.concatenate([p, p, p]), (0, LANES - 3 * heads)).reshape(1, LANES)
    alog_row = lane_pad(gdn_a_log[0])
    dtb_row = lane_pad(gdn_dt_bias[0])

    qk_p, qk_last = _gdn_in(xp, w_in, gdn_conv_w[0], 0, ("q",) * heads + ("k",) * heads, bsz, TM_PROMPT,
                            "gdn_in_qk")
    vz_p, v_last = _gdn_in(xp, w_in, gdn_conv_w[0], 2 * vw, ("v",) * heads + ("z",) * heads, bsz, TM_PROMPT,
                           "gdn_in_vz")
    proj_s = _matmul(xs, w_in, conv_ch + vw, dec, 1024, F32, "gdn_in_sample")
    gates_p = _gdn_gates(xp, w_ab, alog_row, dtb_row, heads, TM_PROMPT, True, "gdn_gates_prompt")
    gates_s = _gdn_gates(xs, w_ab, alog_row, dtb_row, heads, dec, False, "gdn_gates_sample")

    conv_buf_t = jnp.transpose(state_gdn_conv[0], (1, 0, 2))
    qkvn_s = _gdn_prep_sample(proj_s, conv_buf_t, gdn_conv_w[0], heads)

    grow = gates_p[:, :heads].reshape(bsz, t // GDN_TT, GDN_TT, heads).transpose(0, 1, 3, 2)
    og_p, s_prompt = _gdn_chunked(qk_p.reshape(bsz, t, 2 * vw), vz_p.reshape(bsz, t, 2 * vw),
                                  gates_p.reshape(bsz, t, LANES), grow, gdn_norm_w[0], heads)
    og_s, s_sample = _gdn_step(qkvn_s, proj_s, gates_s, state_gdn_S[0], gdn_norm_w[0], heads)

    conv_prompt = jnp.concatenate([qk_last, v_last], axis=-1)
    conv_sample = jnp.concatenate([state_gdn_conv[0][:, 1:], proj_s[:, None, :conv_ch]], axis=1)

    h_p, hb_p = _matmul_ln(og_p.reshape(n_p, vw), gdn_w_out[0], xp, ln_g[0, 0], ln_b[0, 0], TM_PROMPT,
                           "gdn_out_prompt", also_bf16=True)
    h_s, hb_s = _matmul_ln(og_s, gdn_w_out[0], xs, ln_g[0, 0], ln_b[0, 0], dec, "gdn_out_sample", also_bf16=True)

    d_ff = ffn_w_down.shape[1]
    act_p = _matmul_swiglu(hb_p, ffn_w_gu[0], TM_DOWN, d_ff, "ffn_up_prompt")
    act_s = _matmul_swiglu(hb_s, ffn_w_gu[0], dec, d_ff // 2, "ffn_up_sample")
    h_p, hb_p = _matmul_ln(act_p, ffn_w_down[0], h_p, ln_g[0, 1], ln_b[0, 1], TM_DOWN, "ffn_down_prompt",
                           also_bf16=True)
    h_s, hb_s = _matmul_ln(act_s, ffn_w_down[0], h_s, ln_g[0, 1], ln_b[0, 1], dec, "ffn_down_sample",
                           also_bf16=True)

    sg_p, sconv_prompt = _sconv_prompt(hb_p, sc_w_in[0], sc_conv_w[0], bsz, TM_PROMPT, TN_SCONV)
    sproj_s = _matmul(hb_s, sc_w_in[0], 3 * d, dec, 1024, F32, "sc_in_sample")
    sbuf_t = jnp.transpose(state_sconv[0], (1, 0, 2))
    sg_s, ch_s = _sconv_sample(sproj_s, sbuf_t, sc_conv_w[0])
    sconv_sample = jnp.concatenate([state_sconv[0][:, 1:], ch_s[:, None, :]], axis=1)

    n_experts = moe_w_router.shape[2]
    w_router = jnp.pad(moe_w_router[0], ((0, 0), (0, LANES - n_experts)))
    h_all, route = _matmul_ln(sg_p, sc_w_out[0], h_p, ln_g[1, 0], ln_b[1, 0], TM_PROMPT,
                              "sc_out", tail=(sg_s, h_s), router=(w_router, n_experts))

    y_p, y_s = _moe_layer(h_all, route, n_p, n_experts, moe_w_gu[0], moe_w_down[0], ln_g[1, 1], ln_b[1, 1])

    return (y_p.reshape(bsz, t, d), y_s.reshape(dec, 1, d), s_prompt[None], s_sample[None],
            conv_prompt[None], conv_sample[None], sconv_prompt[None], sconv_sample[None])
```

```python
import functools

import jax
import jax.numpy as jnp
from jax import lax
from jax.experimental import pallas as pl
from jax.experimental.pallas import tpu as pltpu

F32 = jnp.float32
BF16 = jnp.bfloat16

DEPTH = 2
ALPHA = (2.0 * DEPTH) ** 0.25
LN_EPS = 1e-5
NORM_EPS = 1e-6
GDN_HEAD_DIM = 128
GDN_CHUNK = 64
GDN_SUB = 16
TOP_K = 2

LANES = 128
SUBLANES = 8
VMEM_LIMIT = 56 * 1024 * 1024
NEG_BIG = -1e30

TM_PROMPT = 1024
TM_DOWN = 512
GDN_TT = 256
GDN_GROUP = 8
TM_EXPERT = 512
TN_EXPERT_UP = 1792
TN_EXPERT_DOWN = 1024
TG_GATHER = 2048
TM_COMBINE = 1024
DMA_SUB = 128
DMA_BUFS = 4
DEC_BB = 8
MM_SLAB = 256
TN_SCONV = 512


def _cparams(n_axes, vmem=VMEM_LIMIT):
    return pltpu.CompilerParams(
        dimension_semantics=("arbitrary",) * n_axes, vmem_limit_bytes=vmem)


def _bdot(a, b):
    return jnp.dot(a.astype(BF16), b.astype(BF16), preferred_element_type=F32)


def _bdot_nt(a, b):
    return lax.dot_general(a.astype(BF16), b.astype(BF16),
                           (((1,), (1,)), ((), ())), preferred_element_type=F32)


def _bdot_tn(a, b):
    return lax.dot_general(a.astype(BF16), b.astype(BF16),
                           (((0,), (0,)), ((), ())), preferred_element_type=F32)


def _silu(x):
    return x * (0.5 * jnp.tanh(0.5 * x) + 0.5)


def _swiglu_slabs(x_ref, wg_bf, wu_bf, o_ref):
    slab = min(MM_SLAB, x_ref.shape[0])
    for r0 in range(0, x_ref.shape[0], slab):
        xb = x_ref[r0:r0 + slab, :].astype(BF16)
        g = jnp.dot(xb, wg_bf[...], preferred_element_type=F32)
        u = jnp.dot(xb, wu_bf[...], preferred_element_type=F32)
        o_ref[r0:r0 + slab, :] = (_silu(g) * u).astype(o_ref.dtype)


def _layer_norm(r, g, b):
    mu = jnp.mean(r, axis=-1, keepdims=True)
    d = r - mu
    var = jnp.mean(d * d, axis=-1, keepdims=True)
    return d * lax.rsqrt(var + LN_EPS) * g + b


def _mm_kernel(x_ref, w_ref, o_ref, wbf_ref):
    @pl.when(pl.program_id(1) == 0)
    def _():
        wbf_ref[...] = w_ref[...].astype(BF16)

    o_ref[...] = jnp.dot(x_ref[...].astype(BF16), wbf_ref[...],
                         preferred_element_type=F32).astype(o_ref.dtype)


def _matmul(x, w, n_cols, tm, tn, out_dtype, name):
    m, k = x.shape
    return pl.pallas_call(
        _mm_kernel,
        grid=(n_cols // tn, m // tm),
        in_specs=[pl.BlockSpec((tm, k), lambda j, i: (i, 0)),
                  pl.BlockSpec((k, tn), lambda j, i: (0, j))],
        out_specs=pl.BlockSpec((tm, tn), lambda j, i: (i, j)),
        out_shape=jax.ShapeDtypeStruct((m, n_cols), out_dtype),
        scratch_shapes=[pltpu.VMEM((k, tn), BF16)],
        compiler_params=_cparams(2),
        name=name,
    )(x, w)


def _mm_swiglu_kernel(x_ref, wg_ref, wu_ref, o_ref, wg_bf, wu_bf):
    @pl.when(pl.program_id(1) == 0)
    def _():
        wg_bf[...] = wg_ref[...].astype(BF16)
        wu_bf[...] = wu_ref[...].astype(BF16)

    _swiglu_slabs(x_ref, wg_bf, wu_bf, o_ref)


def _matmul_swiglu(x, w_gu, tm, tn, name):
    m, k = x.shape
    f = w_gu.shape[1] // 2
    nj = f // tn
    w_mode = dict(pipeline_mode=pl.Buffered(1)) if nj == 1 else {}
    return pl.pallas_call(
        _mm_swiglu_kernel,
        grid=(nj, m // tm),
        in_specs=[pl.BlockSpec((tm, k), lambda j, i: (i, 0)),
                  pl.BlockSpec((k, tn), lambda j, i: (0, j), **w_mode),
                  pl.BlockSpec((k, tn), lambda j, i: (0, j + nj), **w_mode)],
        out_specs=pl.BlockSpec((tm, tn), lambda j, i: (i, j)),
        out_shape=jax.ShapeDtypeStruct((m, f), BF16),
        scratch_shapes=[pltpu.VMEM((k, tn), BF16), pltpu.VMEM((k, tn), BF16)],
        compiler_params=_cparams(2),
        name=name,
    )(x, w_gu, w_gu)


def _top2_route(x, w, n_experts):
    x_hi = x.astype(BF16)
    w_hi = w.astype(BF16)
    x_lo = x - x_hi.astype(F32)
    w_lo = w - w_hi.astype(F32)
    logits = (jnp.dot(x_hi, w_hi, preferred_element_type=F32) + _bdot(x_hi, w_lo) + _bdot(x_lo, w_hi))
    lane = lax.broadcasted_iota(jnp.int32, logits.shape, 1).astype(F32)
    lg = jnp.where(lane < n_experts, logits, NEG_BIG)
    m1 = jnp.max(lg, axis=-1, keepdims=True)
    i1 = jnp.min(jnp.where(lg == m1, lane, float(LANES)), axis=-1, keepdims=True)
    lg2 = jnp.where(lane == i1, NEG_BIG, lg)
    m2 = jnp.max(lg2, axis=-1, keepdims=True)
    i2 = jnp.min(jnp.where(lg2 == m2, lane, float(LANES)), axis=-1, keepdims=True)
    e = jnp.exp(m2 - m1)
    g1 = 1.0 / (1.0 + e)
    g2 = e / (1.0 + e)
    return jnp.where(lane == 0, i1, jnp.where(lane == 1, i2,
                     jnp.where(lane == 2, g1, jnp.where(lane == 3, g2, 0.0))))


def _mm_ln_kernel(x_ref, w_ref, res_ref, g_ref, b_ref, *rest, n_out, has_tail, n_experts):
    tail_refs = rest[:2] if has_tail else ()
    rest = rest[len(tail_refs):]
    router_ref = rest[0] if n_experts else None
    rest = rest[int(bool(n_experts)):]
    out_refs = rest[:n_out]
    route_ref = rest[n_out] if n_experts else None
    wbf_ref = rest[-1]
    i = pl.program_id(0)

    @pl.when(i == 0)
    def _():
        wbf_ref[...] = w_ref[...].astype(BF16)

    def emit(xr, rr):
        slab = min(MM_SLAB, xr.shape[0])
        for r0 in range(0, xr.shape[0], slab):
            rows = slice(r0, r0 + slab)
            y = jnp.dot(xr[rows, :].astype(BF16), wbf_ref[...], preferred_element_type=F32)
            h = _layer_norm(ALPHA * rr[rows, :] + y, g_ref[...], b_ref[...])
            for o_ref in out_refs:
                o_ref[rows, :] = h.astype(o_ref.dtype)
            if n_experts:
                route_ref[rows, :] = _top2_route(h, router_ref[...], n_experts)

    if has_tail:
        n_main = pl.num_programs(0) - 1

        @pl.when(i < n_main)
        def _():
            emit(x_ref, res_ref)

        @pl.when(i == n_main)
        def _():
            emit(*tail_refs)
    else:
        emit(x_ref, res_ref)


def _matmul_ln(x, w, res, ln_g, ln_b, tm, name, also_bf16=False, tail=None, router=None):
    m, k = x.shape
    d = w.shape[1]
    n_main = m // tm
    main_idx = lambda i: (jnp.minimum(i, n_main - 1), 0)
    operands = [x, w, res, ln_g.reshape(1, d), ln_b.reshape(1, d)]
    in_specs = [pl.BlockSpec((tm, k), main_idx),
                pl.BlockSpec((k, d), lambda i: (0, 0)),
                pl.BlockSpec((tm, d), main_idx),
                pl.BlockSpec((1, d), lambda i: (0, 0)),
                pl.BlockSpec((1, d), lambda i: (0, 0))]
    m_out = m
    if tail is not None:
        m_tail = tail[0].shape[0]
        assert m_tail < tm and m_tail % (2 * SUBLANES) == 0
        operands += list(tail)
        in_specs += [pl.BlockSpec((m_tail, k), lambda i: (0, 0)), pl.BlockSpec((m_tail, d), lambda i: (0, 0))]
        m_out = m + m_tail
    out_shape = [jax.ShapeDtypeStruct((m_out, d), F32)]
    if also_bf16:
        out_shape.append(jax.ShapeDtypeStruct((m_out, d), BF16))
    n_out = len(out_shape)
    out_specs = [pl.BlockSpec((tm, d), lambda i: (i, 0)) for _ in out_shape]
    n_experts = 0
    if router is not None:
        w_router, n_experts = router
        operands.append(w_router)
        in_specs.append(pl.BlockSpec((d, LANES), lambda i: (0, 0)))
        out_shape.append(jax.ShapeDtypeStruct((m_out, LANES), F32))
        out_specs.append(pl.BlockSpec((tm, LANES), lambda i: (i, 0)))
    out = pl.pallas_call(
        functools.partial(_mm_ln_kernel, n_out=n_out, has_tail=tail is not None, n_experts=n_experts),
        grid=(n_main + int(tail is not None),),
        in_specs=in_specs,
        out_specs=out_specs,
        out_shape=out_shape,
        scratch_shapes=[pltpu.VMEM((k, d), BF16)],
        compiler_params=_cparams(1),
        name=name,
    )(*operands)
    return out if len(out) > 1 else out[0]


def _gates_kernel(x_ref, w_ref, alog_ref, dtb_ref, o_ref, *, heads, chunk_cumsum):
    a = _bdot(x_ref[...], w_ref[...])
    z = a + dtb_ref[...]
    softplus = jnp.maximum(z, 0.0) + jnp.log1p(jnp.exp(-jnp.abs(z)))
    g = -jnp.exp(alog_ref[...]) * softplus
    total = g
    if chunk_cumsum:
        n_rows = g.shape[0]
        row = jnp.bitwise_and(lax.broadcasted_iota(jnp.int32, g.shape, 0), GDN_CHUNK - 1)
        step = 1
        while step < GDN_CHUNK:
            g = g + jnp.where(row >= step, pltpu.roll(g, step, axis=0), 0.0)
            step *= 2
        total = g
        step = 1
        while step < GDN_CHUNK:
            total = jnp.where(row + step < GDN_CHUNK, pltpu.roll(total, n_rows - step, axis=0), total)
            step *= 2
    lane = lax.broadcasted_iota(jnp.int32, g.shape, 1)
    o_ref[...] = jnp.where(lane < heads, g, jnp.where(lane < 2 * heads, jax.nn.sigmoid(a), total))


def _gdn_gates(x, w_ab, alog_row, dtb_row, heads, tm, chunk_cumsum, name):
    m, k = x.shape
    return pl.pallas_call(
        functools.partial(_gates_kernel, heads=heads, chunk_cumsum=chunk_cumsum),
        grid=(m // tm,),
        in_specs=[pl.BlockSpec((tm, k), lambda i: (i, 0)),
                  pl.BlockSpec((k, LANES), lambda i: (0, 0)),
                  pl.BlockSpec((1, LANES), lambda i: (0, 0)),
                  pl.BlockSpec((1, LANES), lambda i: (0, 0))],
        out_specs=pl.BlockSpec((tm, LANES), lambda i: (i, 0)),
        out_shape=jax.ShapeDtypeStruct((m, LANES), F32),
        compiler_params=_cparams(1),
        name=name,
    )(x, w_ab, alog_row, dtb_row)


def _qkv_finish(conv, j, heads):
    y = _silu(conv)
    inv = lax.rsqrt(jnp.sum(y * y, axis=-1, keepdims=True) + NORM_EPS)
    scale = jnp.where(j < heads, inv * (GDN_HEAD_DIM ** -0.5),
                      jnp.where(j < 2 * heads, inv, 1.0))
    return y * scale


def _gdn_prep_sample_kernel(x_ref, buf_ref, w_ref, o_ref, *, heads):
    w = w_ref[...]
    width = w.shape[0]
    conv = w[width - 1:width, :] * x_ref[...]
    for s in range(width - 1):
        conv = conv + w[s:s + 1, :] * buf_ref[s]
    o_ref[...] = _qkv_finish(conv, pl.program_id(0), heads)


def _gdn_prep_sample(proj, buf_t, conv_w, heads):
    bsz = proj.shape[0]
    width, ch = conv_w.shape
    return pl.pallas_call(
        functools.partial(_gdn_prep_sample_kernel, heads=heads),
        grid=(ch // GDN_HEAD_DIM,),
        in_specs=[pl.BlockSpec((bsz, GDN_HEAD_DIM), lambda j: (0, j)),
                  pl.BlockSpec((width - 1, bsz, GDN_HEAD_DIM), lambda j: (0, 0, j)),
                  pl.BlockSpec((width, GDN_HEAD_DIM), lambda j: (0, j))],
        out_specs=pl.BlockSpec((bsz, GDN_HEAD_DIM), lambda j: (0, j)),
        out_shape=jax.ShapeDtypeStruct((bsz, ch), F32),
        compiler_params=_cparams(1),
        name="gdn_prep_sample",
    )(proj, buf_t, conv_w)


def _gated_rmsnorm(o, z, norm_w):
    on = o * lax.rsqrt(jnp.mean(o * o, axis=-1, keepdims=True) + NORM_EPS) * norm_w
    return on * _silu(z)


def _gdn_in_kernel(x_ref, w_ref, cw_ref, o_ref, last_ref, wbf_ref, halo_ref, *, kinds, tiles_per_seq):
    i = pl.program_id(0)
    dh = GDN_HEAD_DIM
    width = cw_ref.shape[0]
    tm = x_ref.shape[0]
    slab = min(MM_SLAB, tm)

    @pl.when(i == 0)
    def _():
        wbf_ref[...] = w_ref[...].astype(BF16)

    @pl.when(lax.rem(i, tiles_per_seq) == 0)
    def _():
        halo_ref[...] = jnp.zeros_like(halo_ref)

    halo_row = lax.broadcasted_iota(jnp.int32, (SUBLANES, dh), 0)
    for r0 in range(0, tm, slab):
        rows = slice(r0, r0 + slab)
        proj = jnp.dot(x_ref[rows, :].astype(BF16), wbf_ref[...], preferred_element_type=F32)
        for b, kind in enumerate(kinds):
            lanes = slice(b * dh, (b + 1) * dh)
            x = proj[:, lanes]
            if kind == "z":
                o_ref[rows, lanes] = x
                continue
            halo = halo_ref[:, lanes]
            w = cw_ref[:, lanes]
            acc = w[width - 1:width, :] * x
            for s in range(1, width):
                xr = pltpu.roll(x, s, axis=0)
                top = jnp.where(halo_row < s, pltpu.roll(halo, s, axis=0), xr[:SUBLANES, :])
                acc = acc + w[width - 1 - s:width - s, :] * jnp.concatenate([top, xr[SUBLANES:, :]], axis=0)
            halo_ref[:, lanes] = x[slab - SUBLANES:, :]
            y = _silu(acc)
            if kind in "qk":
                scale = dh ** -0.5 if kind == "q" else 1.0
                y = y * (lax.rsqrt(jnp.sum(y * y, axis=-1, keepdims=True) + NORM_EPS) * scale)
            o_ref[rows, lanes] = y
            if r0 + slab == tm:
                last_ref[0, :, lanes] = x[slab - (width - 1):, :]


def _gdn_in(x, w_in, conv_w, col0, kinds, bsz, tm, name):
    m, k = x.shape
    width = conv_w.shape[0]
    tn = GDN_HEAD_DIM * len(kinds)
    n_conv = GDN_HEAD_DIM * sum(kind != "z" for kind in kinds)
    assert col0 % tn == 0 and all(kind != "z" for kind in kinds[:n_conv // GDN_HEAD_DIM])
    tiles_per_seq = (m // bsz) // tm
    return pl.pallas_call(
        functools.partial(_gdn_in_kernel, kinds=kinds, tiles_per_seq=tiles_per_seq),
        grid=(m // tm,),
        in_specs=[pl.BlockSpec((tm, k), lambda i: (i, 0)),
                  pl.BlockSpec((k, tn), lambda i: (0, col0 // tn), pipeline_mode=pl.Buffered(1)),
                  pl.BlockSpec((width, n_conv), lambda i: (0, col0 // n_conv))],
        out_specs=[pl.BlockSpec((tm, tn), lambda i: (i, 0)),
                   pl.BlockSpec((1, width - 1, n_conv), lambda i: (i // tiles_per_seq, 0, 0))],
        out_shape=[jax.ShapeDtypeStruct((m, tn), F32),
                   jax.ShapeDtypeStruct((bsz, width - 1, n_conv), F32)],
        scratch_shapes=[pltpu.VMEM((k, tn), BF16), pltpu.VMEM((SUBLANES, n_conv), F32)],
        compiler_params=_cparams(1),
        name=name,
    )(x, w_in, conv_w)


def _gdn_tile_kernel(q_ref, k_ref, v_ref, z_ref, gates_ref, grow_ref, nw_ref,
                     og_ref, sout_ref, s_ref, *, heads):
    c_len = GDN_CHUNK
    dh = GDN_HEAD_DIM
    tt = q_ref.shape[1]
    n_c = tt // c_len
    tile = pl.program_id(1)

    @pl.when(tile == 0)
    def _():
        s_ref[...] = jnp.zeros_like(s_ref)

    row = lax.broadcasted_iota(jnp.int32, (tt, tt), 0)
    col = lax.broadcasted_iota(jnp.int32, (tt, tt), 1)
    same_chunk = jnp.bitwise_and(row, -c_len) == jnp.bitwise_and(col, -c_len)
    tril = jnp.logical_and(same_chunk, row >= col)
    strict = jnp.logical_and(same_chunk, row > col)
    same_sub = jnp.bitwise_and(row, -GDN_SUB) == jnp.bitwise_and(col, -GDN_SUB)
    gate_tile = gates_ref[0]
    grow_tile = grow_ref[0, 0]
    norm_w = nw_ref[...]

    for g0 in range(0, heads, GDN_GROUP):
        hs = range(g0, g0 + GDN_GROUP)
        q = [q_ref[0, :, h * dh:(h + 1) * dh] for h in hs]
        k = [k_ref[0, :, h * dh:(h + 1) * dh] for h in hs]
        v = [v_ref[0, :, h * dh:(h + 1) * dh] for h in hs]
        gcol = [gate_tile[:, h:h + 1] for h in hs]
        beta = [gate_tile[:, heads + h:heads + h + 1] for h in hs]
        glast = [gate_tile[:, 2 * heads + h:2 * heads + h + 1] for h in hs]
        decay = [jnp.exp(jnp.where(tril, gc - grow_tile[h:h + 1, :], NEG_BIG)) for h, gc in zip(hs, gcol)]
        kb = [ki * bi for ki, bi in zip(k, beta)]
        a_mat = [jnp.where(strict, _bdot_nt(kbi, ki) * di, 0.0) for kbi, ki, di in zip(kb, k, decay)]
        qk = [jnp.where(tril, _bdot_nt(qi, ki) * di, 0.0) for qi, ki, di in zip(q, k, decay)]

        p = [jnp.where(same_sub, ai, 0.0) for ai in a_mat]
        e_mat = [ai - pi for ai, pi in zip(a_mat, p)]
        x = [-pi for pi in p]
        span = 1
        while 2 * span < GDN_SUB:
            p = [_bdot(pi, pi) for pi in p]
            xp = [_bdot(xi, pi) for xi, pi in zip(x, p)]
            x = [xi + pi + xpi for xi, pi, xpi in zip(x, p, xp)]
            span *= 2
        rhs = [jnp.concatenate([vi * bi, kbi * jnp.exp(gc)], axis=-1)
               for vi, bi, kbi, gc in zip(v, beta, kb, gcol)]
        n_mat = [ei + _bdot(xi, ei) for ei, xi in zip(e_mat, x)]
        r = [ri + _bdot(xi, ri) for ri, xi in zip(rhs, x)]
        r = [ri - _bdot(ni, ri) for ri, ni in zip(r, n_mat)]
        span = 2
        while span < c_len // GDN_SUB:
            n_mat = [_bdot(ni, ni) for ni in n_mat]
            r = [ri + _bdot(ni, ri) for ri, ni in zip(r, n_mat)]
            span *= 2

        o_uw = [_bdot(qki, ri) for qki, ri in zip(qk, r)]
        q_eff = [qi * jnp.exp(gc) - oi[:, dh:] for qi, gc, oi in zip(q, gcol, o_uw)]
        k_dec = [ki * jnp.exp(gl - gc) for ki, gl, gc in zip(k, glast, gcol)]
        s = [s_ref[h] for h in hs]
        for c in range(n_c):
            rows = slice(c * c_len, (c + 1) * c_len)
            kr = [_bdot_tn(kd[rows, :], ri[rows, :]) for kd, ri in zip(k_dec, r)]
            o = [oi[rows, :dh] + _bdot(qe[rows, :], si) for oi, qe, si in zip(o_uw, q_eff, s)]
            for h, oi in zip(hs, o):
                z = z_ref[0, rows, h * dh:(h + 1) * dh]
                og_ref[0, rows, h * dh:(h + 1) * dh] = _gated_rmsnorm(oi, z, norm_w).astype(og_ref.dtype)
            a_c = [jnp.exp(gl[(c + 1) * c_len - 1:(c + 1) * c_len, :]) for gl in glast]
            s = [si * ai + kri[:, :dh] - _bdot(kri[:, dh:], si) for si, ai, kri in zip(s, a_c, kr)]
        for h, si in zip(hs, s):
            s_ref[h] = si

    @pl.when(tile == pl.num_programs(1) - 1)
    def _():
        sout_ref[0] = s_ref[...]


def _gdn_chunked(qk, vz, gates, grow, norm_w, heads):
    bsz, t, _ = qk.shape
    dh = GDN_HEAD_DIM
    vw = heads * dh
    tt = GDN_TT
    blk = lambda part: pl.BlockSpec((1, tt, vw), lambda b, i: (b, i, part))
    return pl.pallas_call(
        functools.partial(_gdn_tile_kernel, heads=heads),
        grid=(bsz, t // tt),
        in_specs=[blk(0), blk(1), blk(0), blk(1),
                  pl.BlockSpec((1, tt, LANES), lambda b, i: (b, i, 0)),
                  pl.BlockSpec((1, 1, heads, tt), lambda b, i: (b, i, 0, 0)),
                  pl.BlockSpec((1, dh), lambda b, i: (0, 0))],
        out_specs=[pl.BlockSpec((1, tt, vw), lambda b, i: (b, i, 0)),
                   pl.BlockSpec((1, heads, dh, dh), lambda b, i: (b, 0, 0, 0))],
        out_shape=[jax.ShapeDtypeStruct((bsz, t, vw), BF16),
                   jax.ShapeDtypeStruct((bsz, heads, dh, dh), F32)],
        scratch_shapes=[pltpu.VMEM((heads, dh, dh), F32)],
        compiler_params=_cparams(2),
        name="gdn_chunked",
    )(qk, qk, vz, vz, gates, grow, norm_w.reshape(1, dh))


def _gdn_step_kernel(qkv_ref, z_ref, gates_ref, s0_ref, nw_ref, og_ref, sout_ref, *, heads):
    dh = GDN_HEAD_DIM
    pad = 2 * SUBLANES
    norm_w = nw_ref[...]
    prow = lax.broadcasted_iota(jnp.int32, (pad, dh), 0)

    pairs = [(bi, h) for bi in range(qkv_ref.shape[0]) for h in range(heads)]
    rows = {bi: slice(bi, bi + 1) for bi, _ in pairs}
    q = [qkv_ref[rows[bi], h * dh:(h + 1) * dh] for bi, h in pairs]
    k = [qkv_ref[rows[bi], (heads + h) * dh:(heads + h + 1) * dh] for bi, h in pairs]
    a = [jnp.exp(gates_ref[rows[bi], h:h + 1]) for bi, h in pairs]
    s_kq = [_bdot(jnp.where(prow == 0, ki, jnp.where(prow == 1, qi, 0.0)), s0_ref[bi, h])
            for (bi, h), ki, qi in zip(pairs, k, q)]
    k_parts, d_parts = [], []
    for (bi, h), qi, ki, ai, si in zip(pairs, q, k, a, s_kq):
        v = qkv_ref[rows[bi], (2 * heads + h) * dh:(2 * heads + h + 1) * dh]
        beta = gates_ref[rows[bi], heads + h:heads + h + 1]
        delta = (v - ai * si[0:1, :]) * beta
        o = ai * si[1:2, :] + jnp.sum(ki * qi, axis=-1, keepdims=True) * delta
        og_ref[rows[bi], h * dh:(h + 1) * dh] = _gated_rmsnorm(o, z_ref[rows[bi], h * dh:(h + 1) * dh], norm_w)
        k_hi = ki.astype(BF16).astype(F32)
        d_hi = delta.astype(BF16).astype(F32)
        k_parts.append(jnp.where(prow < 2, k_hi, jnp.where(prow == 2, ki - k_hi, 0.0)))
        d_parts.append(jnp.where(prow == 1, delta - d_hi, jnp.where(prow < 3, d_hi, 0.0)))
    for (bi, h), ai, kp, dp in zip(pairs, a, k_parts, d_parts):
        sout_ref[bi, h] = s0_ref[bi, h] * ai + _bdot_tn(kp, dp)


def _gdn_step(qkvn, proj, gates, s0, norm_w, heads):
    bsz = qkvn.shape[0]
    dh = GDN_HEAD_DIM
    vw = heads * dh
    bb = DEC_BB
    return pl.pallas_call(
        functools.partial(_gdn_step_kernel, heads=heads),
        grid=(bsz // bb,),
        in_specs=[pl.BlockSpec((bb, 3 * vw), lambda i: (i, 0)),
                  pl.BlockSpec((bb, vw), lambda i: (i, 3)),
                  pl.BlockSpec((bb, LANES), lambda i: (i, 0)),
                  pl.BlockSpec((bb, heads, dh, dh), lambda i: (i, 0, 0, 0)),
                  pl.BlockSpec((1, dh), lambda i: (0, 0))],
        out_specs=[pl.BlockSpec((bb, vw), lambda i: (i, 0)),
                   pl.BlockSpec((bb, heads, dh, dh), lambda i: (i, 0, 0, 0))],
        out_shape=[jax.ShapeDtypeStruct((bsz, vw), F32),
                   jax.ShapeDtypeStruct((bsz, heads, dh, dh), F32)],
        compiler_params=_cparams(1),
        name="gdn_step",
    )(qkvn, proj, gates, s0, norm_w.reshape(1, dh))


def _sconv_prompt_kernel(x_ref, wb_ref, wc_ref, wh_ref, cw_ref, o_ref, last_ref,
                         wb_bf, wc_bf, wh_bf, halo_ref, *, tiles_per_seq):
    i = pl.program_id(1)

    @pl.when(i == 0)
    def _():
        wb_bf[...] = wb_ref[...].astype(BF16)
        wc_bf[...] = wc_ref[...].astype(BF16)
        wh_bf[...] = wh_ref[...].astype(BF16)

    @pl.when(lax.rem(i, tiles_per_seq) == 0)
    def _():
        halo_ref[...] = jnp.zeros_like(halo_ref)

    xb = x_ref[...].astype(BF16)
    gate = jnp.dot(xb, wb_bf[...], preferred_element_type=F32)
    ch = (jnp.dot(xb, wc_bf[...], preferred_element_type=F32)
          * jnp.dot(xb, wh_bf[...], preferred_element_type=F32))
    w = cw_ref[...]
    width = w.shape[0]
    tm = ch.shape[0]
    halo = halo_ref[...]
    halo_row = lax.broadcasted_iota(jnp.int32, halo.shape, 0)
    conv = w[width - 1:width, :] * ch
    for s in range(1, width):
        xr = pltpu.roll(ch, s, axis=0)
        top = jnp.where(halo_row < s, pltpu.roll(halo, s, axis=0), xr[:SUBLANES, :])
        conv = conv + w[width - 1 - s:width - s, :] * jnp.concatenate([top, xr[SUBLANES:, :]], axis=0)
    halo_ref[...] = ch[tm - SUBLANES:, :]
    o_ref[...] = (gate * conv).astype(o_ref.dtype)
    last_ref[0] = ch[tm - (width - 1):, :]


def _sconv_prompt(x, w_in, conv_w, bsz, tm, tn):
    m, k = x.shape
    width, d = conv_w.shape
    t = m // bsz
    tiles_per_seq = t // tm
    nb = d // tn
    wblk = lambda off: pl.BlockSpec((k, tn), lambda j, i: (0, j + off))
    return pl.pallas_call(
        functools.partial(_sconv_prompt_kernel, tiles_per_seq=tiles_per_seq),
        grid=(nb, m // tm),
        in_specs=[pl.BlockSpec((tm, k), lambda j, i: (i, 0)),
                  wblk(0), wblk(nb), wblk(2 * nb),
                  pl.BlockSpec((width, tn), lambda j, i: (0, j))],
        out_specs=[pl.BlockSpec((tm, tn), lambda j, i: (i, j)),
                   pl.BlockSpec((1, width - 1, tn), lambda j, i: (i // tiles_per_seq, 0, j))],
        out_shape=[jax.ShapeDtypeStruct((m, d), BF16),
                   jax.ShapeDtypeStruct((bsz, width - 1, d), F32)],
        scratch_shapes=[pltpu.VMEM((k, tn), BF16), pltpu.VMEM((k, tn), BF16), pltpu.VMEM((k, tn), BF16),
                        pltpu.VMEM((SUBLANES, tn), F32)],
        compiler_params=_cparams(2),
        name="sconv_prompt",
    )(x, w_in, w_in, w_in, conv_w)


def _sconv_sample_kernel(b_ref, c_ref, h_ref, buf_ref, w_ref, o_ref, ch_ref):
    ch = c_ref[...] * h_ref[...]
    w = w_ref[...]
    width = w.shape[0]
    conv = w[width - 1:width, :] * ch
    for s in range(width - 1):
        conv = conv + w[s:s + 1, :] * buf_ref[s]
    o_ref[...] = b_ref[...] * conv
    ch_ref[...] = ch


def _sconv_sample(proj, buf_t, conv_w):
    bsz = proj.shape[0]
    width, d = conv_w.shape
    nb = d // LANES
    blk = lambda off: pl.BlockSpec((bsz, LANES), lambda j: (0, j + off))
    return pl.pallas_call(
        _sconv_sample_kernel,
        grid=(nb,),
        in_specs=[blk(0), blk(nb), blk(2 * nb),
                  pl.BlockSpec((width - 1, bsz, LANES), lambda j: (0, 0, j)),
                  pl.BlockSpec((width, LANES), lambda j: (0, j))],
        out_specs=[pl.BlockSpec((bsz, LANES), lambda j: (0, j)),
                   pl.BlockSpec((bsz, LANES), lambda j: (0, j))],
        out_shape=[jax.ShapeDtypeStruct((bsz, d), F32),
                   jax.ShapeDtypeStruct((bsz, d), F32)],
        compiler_params=_cparams(1),
        name="sconv_sample",
    )(proj, proj, proj, buf_t, conv_w)


def _row_copy(src_hbm, src_row, dst_ref, dst_row, sem):
    return pltpu.make_async_copy(src_hbm.at[pl.ds(src_row, 1)], dst_ref.at[pl.ds(dst_row, 1)], sem)


def _wait_rows(src_hbm, dst_ref, sem):
    pltpu.make_async_copy(src_hbm.at[pl.ds(0, dst_ref.shape[0])], dst_ref, sem).wait()


def _gather_kernel(idx_ref, nu_ref, src_hbm, o_ref, buf, sems, *, tile_rows):
    i = pl.program_id(0)
    tg = o_ref.shape[0]
    n_sub = tg // DMA_SUB
    ahead = DMA_BUFS - 1
    base = i * tg
    n_act = jnp.minimum(jnp.maximum(nu_ref[0] * (tile_rows // DMA_SUB) - i * n_sub, 0), n_sub)

    def issue(j, slot):
        for r in range(DMA_SUB):
            _row_copy(src_hbm, idx_ref[base + j * DMA_SUB + r], buf.at[slot], r, sems.at[slot]).start()

    def sub_rows(j):
        return pl.ds(pl.multiple_of(j * DMA_SUB, DMA_SUB), DMA_SUB)

    for j0 in range(ahead):
        @pl.when(j0 < n_act)
        def _(j0=j0):
            issue(j0, j0)

    def body(j, c):
        slot = jnp.bitwise_and(j, DMA_BUFS - 1)

        @pl.when(j + ahead < n_act)
        def _():
            issue(j + ahead, jnp.bitwise_and(j + ahead, DMA_BUFS - 1))

        _wait_rows(src_hbm, buf.at[slot], sems.at[slot])
        o_ref[sub_rows(j), :] = buf[slot].astype(o_ref.dtype)
        return c
    lax.fori_loop(0, n_act, body, 0)

    def zero(j, c):
        o_ref[sub_rows(j), :] = jnp.zeros((DMA_SUB, o_ref.shape[1]), o_ref.dtype)
        return c
    lax.fori_loop(n_act, n_sub, zero, 0)


def _gather_rows(src, row_token, n_used, tg, tile_rows):
    rows = row_token.shape[0]
    d = src.shape[1]
    assert rows % tg == 0 and tg % tile_rows == 0 and tg // DMA_SUB >= DMA_BUFS - 1
    return pl.pallas_call(
        functools.partial(_gather_kernel, tile_rows=tile_rows),
        grid_spec=pltpu.PrefetchScalarGridSpec(
            num_scalar_prefetch=2,
            grid=(rows // tg,),
            in_specs=[pl.BlockSpec(memory_space=pl.ANY)],
            out_specs=pl.BlockSpec((tg, d), lambda i, idx, nu: (i, 0)),
            scratch_shapes=[pltpu.VMEM((DMA_BUFS, DMA_SUB, d), src.dtype), pltpu.SemaphoreType.DMA((DMA_BUFS,))]),
        out_shape=jax.ShapeDtypeStruct((rows, d), BF16),
        compiler_params=_cparams(1),
        name="moe_gather",
    )(row_token, n_used, src)


def _moe_ffn1_kernel(te_ref, nu_ref, x_ref, wg_ref, wu_ref, o_ref, wg_bf, wu_bf):
    i = pl.program_id(1)
    new_expert = jnp.logical_or(i == 0, te_ref[i] != te_ref[jnp.maximum(i - 1, 0)])

    @pl.when(new_expert)
    def _():
        wg_bf[...] = wg_ref[0].astype(BF16)
        wu_bf[...] = wu_ref[0].astype(BF16)

    @pl.when(i < nu_ref[0])
    def _():
        _swiglu_slabs(x_ref, wg_bf, wu_bf, o_ref)

    @pl.when(i >= nu_ref[0])
    def _():
        o_ref[...] = jnp.zeros_like(o_ref)


def _moe_ffn1(xs, w_gu, tile_expert, n_used, tm, tn):
    rows, k = xs.shape
    f = w_gu.shape[2] // 2
    nj = f // tn
    return pl.pallas_call(
        _moe_ffn1_kernel,
        grid_spec=pltpu.PrefetchScalarGridSpec(
            num_scalar_prefetch=2,
            grid=(nj, rows // tm),
            in_specs=[pl.BlockSpec((tm, k), lambda j, i, te, nu: (i, 0)),
                      pl.BlockSpec((1, k, tn), lambda j, i, te, nu: (te[i], 0, j)),
                      pl.BlockSpec((1, k, tn), lambda j, i, te, nu: (te[i], 0, j + nj))],
            out_specs=pl.BlockSpec((tm, tn), lambda j, i, te, nu: (i, j)),
            scratch_shapes=[pltpu.VMEM((k, tn), BF16), pltpu.VMEM((k, tn), BF16)]),
        out_shape=jax.ShapeDtypeStruct((rows, f), BF16),
        compiler_params=_cparams(2),
        name="moe_ffn1",
    )(tile_expert, n_used, xs, w_gu, w_gu)


def _moe_ffn2_kernel(te_ref, nu_ref, x_ref, w_ref, o_ref, w_bf):
    i = pl.program_id(1)
    new_expert = jnp.logical_or(i == 0, te_ref[i] != te_ref[jnp.maximum(i - 1, 0)])

    @pl.when(new_expert)
    def _():
        w_bf[...] = w_ref[0].astype(BF16)

    @pl.when(i < nu_ref[0])
    def _():
        o_ref[...] = jnp.dot(x_ref[...], w_bf[...], preferred_element_type=F32)

    @pl.when(i >= nu_ref[0])
    def _():
        o_ref[...] = jnp.zeros_like(o_ref)


def _moe_ffn2(act, w_down, tile_expert, n_used, tm, tn):
    rows, k = act.shape
    d = w_down.shape[2]
    return pl.pallas_call(
        _moe_ffn2_kernel,
        grid_spec=pltpu.PrefetchScalarGridSpec(
            num_scalar_prefetch=2,
            grid=(d // tn, rows // tm),
            in_specs=[pl.BlockSpec((tm, k), lambda j, i, te, nu: (i, 0)),
                      pl.BlockSpec((1, k, tn), lambda j, i, te, nu: (te[i], 0, j))],
            out_specs=pl.BlockSpec((tm, tn), lambda j, i, te, nu: (i, j)),
            scratch_shapes=[pltpu.VMEM((k, tn), BF16)]),
        out_shape=jax.ShapeDtypeStruct((rows, d), F32),
        compiler_params=_cparams(2),
        name="moe_ffn2",
    )(tile_expert, n_used, act, w_down)


def _combine_kernel(pos_ref, ys_hbm, route_ref, res_ref, g_ref, b_ref, o_head, o_tail, buf_a, buf_b, sems,
                    *, n_head_tiles):
    i = pl.program_id(0)
    tm = res_ref.shape[0]
    n_sub = jnp.where(i < n_head_tiles, tm // DMA_SUB, o_tail.shape[0] // DMA_SUB)
    base2 = i * (TOP_K * tm)

    def issue(j, slot):
        p0 = base2 + j * (TOP_K * DMA_SUB)
        for r in range(DMA_SUB):
            _row_copy(ys_hbm, pos_ref[p0 + TOP_K * r], buf_a.at[slot], r, sems.at[0, slot]).start()
            _row_copy(ys_hbm, pos_ref[p0 + TOP_K * r + 1], buf_b.at[slot], r, sems.at[1, slot]).start()

    ahead = DMA_BUFS - 1
    for j0 in range(ahead):
        @pl.when(j0 < n_sub)
        def _(j0=j0):
            issue(j0, j0)

    def body(j, c):
        slot = jnp.bitwise_and(j, DMA_BUFS - 1)

        @pl.when(j + ahead < n_sub)
        def _():
            issue(j + ahead, jnp.bitwise_and(j + ahead, DMA_BUFS - 1))

        _wait_rows(ys_hbm, buf_a.at[slot], sems.at[0, slot])
        _wait_rows(ys_hbm, buf_b.at[slot], sems.at[1, slot])
        rows = pl.ds(pl.multiple_of(j * DMA_SUB, DMA_SUB), DMA_SUB)
        route = route_ref[rows, :]
        y = route[:, 2:3] * buf_a[slot] + route[:, 3:4] * buf_b[slot]
        out = _layer_norm(ALPHA * res_ref[rows, :] + y, g_ref[...], b_ref[...])

        @pl.when(i < n_head_tiles)
        def _():
            o_head[rows, :] = out

        @pl.when(i >= n_head_tiles)
        def _():
            o_tail[rows, :] = out
        return c
    lax.fori_loop(0, n_sub, body, 0)


def _moe_combine(ys, pos, route, res, ln_g, ln_b, n_head):
    m, d = res.shape
    tm = TM_COMBINE
    head_tiles = n_head // tm
    n_tail = m - n_head
    assert n_head % tm == 0 and 0 < n_tail <= tm and n_tail % DMA_SUB == 0
    return pl.pallas_call(
        functools.partial(_combine_kernel, n_head_tiles=head_tiles),
        grid_spec=pltpu.PrefetchScalarGridSpec(
            num_scalar_prefetch=1,
            grid=(head_tiles + 1,),
            in_specs=[pl.BlockSpec(memory_space=pl.ANY),
                      pl.BlockSpec((tm, LANES), lambda i, p: (i, 0)),
                      pl.BlockSpec((tm, d), lambda i, p: (i, 0)),
                      pl.BlockSpec((1, d), lambda i, p: (0, 0)),
                      pl.BlockSpec((1, d), lambda i, p: (0, 0))],
            out_specs=[pl.BlockSpec((tm, d), lambda i, p: (jnp.minimum(i, head_tiles - 1), 0)),
                       pl.BlockSpec((n_tail, d), lambda i, p: (0, 0))],
            scratch_shapes=[pltpu.VMEM((DMA_BUFS, DMA_SUB, d), F32), pltpu.VMEM((DMA_BUFS, DMA_SUB, d), F32),
                            pltpu.SemaphoreType.DMA((2, DMA_BUFS))]),
        out_shape=[jax.ShapeDtypeStruct((n_head, d), F32),
                   jax.ShapeDtypeStruct((n_tail, d), F32)],
        compiler_params=_cparams(1),
        name="moe_combine",
    )(pos, ys, route, res, ln_g.reshape(1, d), ln_b.reshape(1, d))


def _dispatch_plan(route, n_experts, tm):
    n_tok = route.shape[0]
    ids = route[:, :TOP_K].astype(jnp.int32).reshape(-1)
    n_pairs = ids.shape[0]
    n_tiles = n_pairs // tm + n_experts
    onehot = (ids[:, None] == jnp.arange(n_experts, dtype=jnp.int32)[None, :]).astype(jnp.int32)
    rank = jnp.sum((jnp.cumsum(onehot, axis=0) - onehot) * onehot, axis=1)
    counts = jnp.sum(onehot, axis=0)
    tiles_e = (counts + tm - 1) // tm
    tile_end = jnp.cumsum(tiles_e)
    group_off = (tile_end - tiles_e) * tm
    pos = group_off[ids] + rank
    row_token = jnp.zeros((n_tiles * tm,), jnp.int32).at[pos].set(
        jnp.arange(n_pairs, dtype=jnp.int32) // TOP_K)
    n_used = tile_end[n_experts - 1:]
    tile_ids = jnp.arange(n_tiles, dtype=jnp.int32)
    tile_expert = jnp.sum((tile_ids[:, None] >= tile_end[None, :]).astype(jnp.int32), axis=1)
    last_expert = jnp.sum((n_used - 1 >= tile_end).astype(jnp.int32))
    tile_expert = jnp.minimum(tile_expert, last_expert).astype(jnp.int32)
    return row_token, pos.astype(jnp.int32), tile_expert, n_used.astype(jnp.int32)


def _moe_layer(h_all, route, n_head, n_experts, w_gu, w_down, ln_g, ln_b):
    row_token, pos, tile_expert, n_used = _dispatch_plan(route, n_experts, TM_EXPERT)
    xs = _gather_rows(h_all, row_token, n_used, TG_GATHER, TM_EXPERT)
    act = _moe_ffn1(xs, w_gu, tile_expert, n_used, TM_EXPERT, TN_EXPERT_UP)
    ys = _moe_ffn2(act, w_down, tile_expert, n_used, TM_EXPERT, TN_EXPERT_DOWN)
    return _moe_combine(ys, pos, route, h_all, ln_g, ln_b, n_head)


def kernel(x_prompt, x_sample, state_gdn_S, state_gdn_conv, state_sconv, ln_g, ln_b, gdn_w_in, gdn_conv_w, gdn_a_log, gdn_dt_bias, gdn_norm_w, gdn_w_out, sc_w_in, sc_conv_w, sc_w_out, ffn_w_gu, ffn_w_down, moe_w_router, moe_w_gu, moe_w_down):
    bsz, t, d = x_prompt.shape
    dec = x_sample.shape[0]
    heads = gdn_a_log.shape[1]
    dh = GDN_HEAD_DIM
    vw = heads * dh
    conv_ch = gdn_conv_w.shape[2]
    assert x_sample.shape[1] == 1 and conv_ch == 3 * vw and gdn_w_in.shape[2] == conv_ch + vw + 2 * heads
    assert t % GDN_TT == 0 and t % TM_PROMPT == 0
    assert dec % DMA_SUB == 0 and dec < TM_PROMPT
    n_p = bsz * t

    xp = x_prompt.reshape(n_p, d)
    xs = x_sample.reshape(dec, d)

    w_in = gdn_w_in[0]
    w_a = w_in[:, conv_ch + vw:conv_ch + vw + heads]
    w_b = w_in[:, conv_ch + vw + heads:]
    w_ab = jnp.pad(jnp.concatenate([w_a, w_b, w_a], axis=1), ((0, 0), (0, LANES - 3 * heads)))
    lane_pad = lambda p: jnp.pad(jnp.concatenate([p, p, p]), (0, LANES - 3 * heads)).reshape(1, LANES)
    alog_row = lane_pad(gdn_a_log[0])
    dtb_row = lane_pad(gdn_dt_bias[0])

    qk_p, qk_last = _gdn_in(xp, w_in, gdn_conv_w[0], 0, ("q",) * heads + ("k",) * heads, bsz, TM_PROMPT,
                            "gdn_in_qk")
    vz_p, v_last = _gdn_in(xp, w_in, gdn_conv_w[0], 2 * vw, ("v",) * heads + ("z",) * heads, bsz, TM_PROMPT,
                           "gdn_in_vz")
    proj_s = _matmul(xs, w_in, conv_ch + vw, dec, 1024, F32, "gdn_in_sample")
    gates_p = _gdn_gates(xp, w_ab, alog_row, dtb_row, heads, TM_PROMPT, True, "gdn_gates_prompt")
    gates_s = _gdn_gates(xs, w_ab, alog_row, dtb_row, heads, dec, False, "gdn_gates_sample")

    conv_buf_t = jnp.transpose(state_gdn_conv[0], (1, 0, 2))
    qkvn_s = _gdn_prep_sample(proj_s, conv_buf_t, gdn_conv_w[0], heads)

    grow = gates_p[:, :heads].reshape(bsz, t // GDN_TT, GDN_TT, heads).transpose(0, 1, 3, 2)
    og_p, s_prompt = _gdn_chunked(qk_p.reshape(bsz, t, 2 * vw), vz_p.reshape(bsz, t, 2 * vw),
                                  gates_p.reshape(bsz, t, LANES), grow, gdn_norm_w[0], heads)
    og_s, s_sample = _gdn_step(qkvn_s, proj_s, gates_s, state_gdn_S[0], gdn_norm_w[0], heads)

    conv_prompt = jnp.concatenate([qk_last, v_last], axis=-1)
    conv_sample = jnp.concatenate([state_gdn_conv[0][:, 1:], proj_s[:, None, :conv_ch]], axis=1)

    h_p, hb_p = _matmul_ln(og_p.reshape(n_p, vw), gdn_w_out[0], xp, ln_g[0, 0], ln_b[0, 0], TM_PROMPT,
                           "gdn_out_prompt", also_bf16=True)
    h_s, hb_s = _matmul_ln(og_s, gdn_w_out[0], xs, ln_g[0, 0], ln_b[0, 0], dec, "gdn_out_sample", also_bf16=True)

    d_ff = ffn_w_down.shape[1]
    act_p = _matmul_swiglu(hb_p, ffn_w_gu[0], TM_DOWN, d_ff, "ffn_up_prompt")
    act_s = _matmul_swiglu(hb_s, ffn_w_gu[0], dec, d_ff // 2, "ffn_up_sample")
    h_p, hb_p = _matmul_ln(act_p, ffn_w_down[0], h_p, ln_g[0, 1], ln_b[0, 1], TM_DOWN, "ffn_down_prompt",
                           also_bf16=True)
    h_s, hb_s = _matmul_ln(act_s, ffn_w_down[0], h_s, ln_g[0, 1], ln_b[0, 1], dec, "ffn_down_sample",
                           also_bf16=True)

    sg_p, sconv_prompt = _sconv_prompt(hb_p, sc_w_in[0], sc_conv_w[0], bsz, TM_PROMPT, TN_SCONV)
    sproj_s = _matmul(hb_s, sc_w_in[0], 3 * d, dec, 1024, F32, "sc_in_sample")
    sbuf_t = jnp.transpose(state_sconv[0], (1, 0, 2))
    sg_s, ch_s = _sconv_sample(sproj_s, sbuf_t, sc_conv_w[0])
    sconv_sample = jnp.concatenate([state_sconv[0][:, 1:], ch_s[:, None, :]], axis=1)

    n_experts = moe_w_router.shape[2]
    w_router = jnp.pad(moe_w_router[0], ((0, 0), (0, LANES - n_experts)))
    h_all, route = _matmul_ln(sg_p, sc_w_out[0], h_p, ln_g[1, 0], ln_b[1, 0], TM_PROMPT,
                              "sc_out", tail=(sg_s, h_s), router=(w_router, n_experts))

    y_p, y_s = _moe_layer(h_all, route, n_p, n_experts, moe_w_gu[0], moe_w_down[0], ln_g[1, 1], ln_b[1, 1])

    return (y_p.reshape(bsz, t, d), y_s.reshape(dec, 1, d), s_prompt[None], s_sample[None],
            conv_prompt[None], conv_sample[None], sconv_prompt[None], sconv_sample[None])
```
